```python
import math
import functools
import jax
import jax.numpy as jnp
from jax import lax
import numpy as np

D_MODEL = 1024
BATCH = 4
SEQ = 4096
DEPTH = 1
DEC_BATCH = 128
DEC_SEQ = 4
PAST_LEN = 8192
PAGE_SIZE = 128

MIX_WIDTH = D_MODEL
SSM_WIDTH = MIX_WIDTH // 2
SSM_GROUP = 16
SSM_GROUPS = SSM_WIDTH // SSM_GROUP
SSM_STATE = 64
ATT_WIDTH = MIX_WIDTH - SSM_WIDTH
D_NOPE = 64
D_ROPE = 32
D_V = 64
N_HEADS = ATT_WIDTH // D_V
Q_LORA = 3 * D_MODEL // 8
KV_LORA = D_MODEL // 4
IN_WIDTH = SSM_WIDTH + Q_LORA + KV_LORA + D_ROPE
ROPE_THETA = 10000.0
ATTN_SCALE = 1.0 / math.sqrt(D_NOPE + D_ROPE)
Q_BLOCK = 128
PEER_HEADS = 8
PEER_KEYS = 128
PEER_EXPERTS = PEER_KEYS * PEER_KEYS
PEER_QUERY = 256
PEER_HALF = PEER_QUERY // 2
PEER_TOPK = 16
PEER_BLOCK = 128
EPS = 1e-6

kernel_name = 'hymba_s5_mla_peer_step'


def _rmsnorm(x, g):
    x32 = x.astype(jnp.float32)
    y = x32 * lax.rsqrt(jnp.mean(x32 * x32, axis=-1, keepdims=True) + EPS)
    return (y * g.astype(jnp.float32)).astype(x.dtype)


def _rope_angles(pos):
    inv = ROPE_THETA ** (-jnp.arange(0, D_ROPE, 2, dtype=jnp.float32) / D_ROPE)
    ang = pos.astype(jnp.float32)[:, None] * inv[None, :]
    return jnp.cos(ang), jnp.sin(ang)


def _rope(x, cos, sin):
    half = x.shape[-1] // 2
    x32 = x.astype(jnp.float32)
    x1, x2 = x32[..., :half], x32[..., half:]
    return jnp.concatenate([x1 * cos - x2 * sin, x2 * cos + x1 * sin], axis=-1).astype(x.dtype)


def _linear_combine(e1, e2):
    a1, b1 = e1
    a2, b2 = e2
    return a1 * a2, a2 * b1 + b2


def _s5(u, h0, a_re, a_im, log_dt, b, c, d_skip):
    bsz, t, _ = u.shape
    f32 = jnp.float32
    u32 = u.astype(f32)
    lam = lax.complex(a_re.astype(f32), a_im.astype(f32))
    dt = jnp.exp(log_dt.astype(f32))[:, None]
    a_bar = jnp.exp(lam * dt)
    b_c = lax.complex(b[..., 0].astype(f32), b[..., 1].astype(f32))
    c_c = lax.complex(c[..., 0].astype(f32), c[..., 1].astype(f32))
    b_bar = ((a_bar - 1.0) / lam)[:, :, None] * b_c
    ug = u32.reshape(bsz, t, SSM_GROUPS, SSM_GROUP).astype(jnp.complex64)
    bu = jnp.einsum('btgh,gph->btgp', ug, b_bar)
    bu = bu.at[:, 0].add(a_bar[None] * h0)
    a_seq = jnp.broadcast_to(a_bar, bu.shape)
    _, h = lax.associative_scan(_linear_combine, (a_seq, bu), axis=1)
    y = jnp.einsum('btgp,ghp->btgh', h, c_c).real.reshape(bsz, t, SSM_WIDTH)
    y = y + d_skip.astype(f32) * u32
    return y.astype(u.dtype), h[:, -1]


def _latent_keys(c_kv, w_uk, g_kn):
    return _rmsnorm(jnp.einsum('btc,chd->bthd', c_kv, w_uk), g_kn)


def _attend(qn, qr, kn, kr, ckv, mask):
    s = jnp.einsum('bqhd,bkhd->bhqk', qn, kn) + jnp.einsum('bqhd,bkd->bhqk', qr, kr)
    s = jnp.where(mask, s.astype(jnp.float32) * ATTN_SCALE, -jnp.inf)
    p = jax.nn.softmax(s, axis=-1)
    return jnp.einsum('bhqk,bkc->bqhc', p.astype(ckv.dtype), ckv)


def _prompt_attention(q_nope, q_rope, c_kv, k_rope, w_uk, g_kn):
    bsz, s = q_nope.shape[:2]
    k_nope = _latent_keys(c_kv, w_uk, g_kn)
    kpos = jnp.arange(s)

    def block(i):
        start = i * Q_BLOCK
        qn = lax.dynamic_slice_in_dim(q_nope, start, Q_BLOCK, axis=1)
        qr = lax.dynamic_slice_in_dim(q_rope, start, Q_BLOCK, axis=1)
        mask = kpos[None, :] <= (start + jnp.arange(Q_BLOCK))[:, None]
        return _attend(qn, qr, k_nope, k_rope, c_kv, mask)

    o = lax.map(block, jnp.arange(s // Q_BLOCK))
    return jnp.moveaxis(o, 0, 1).reshape(bsz, s, N_HEADS, KV_LORA)


def _sample_attention(q_nope, q_rope, c_kv, k_rope, w_uk, g_kn, cache_lat, cache_kr, page_table, layer):
    past = page_table.shape[1] * PAGE_SIZE
    t = q_nope.shape[1]
    mask = jnp.arange(past + t)[None, :] <= (past + jnp.arange(t))[:, None]

    def one(args):
        pt, qn, qr, cn, krn = args
        lat = jnp.concatenate([cache_lat[layer, pt].reshape(past, KV_LORA).astype(cn.dtype), cn], axis=0)[None]
        krr = jnp.concatenate([cache_kr[layer, pt].reshape(past, D_ROPE).astype(krn.dtype), krn], axis=0)[None]
        kn = _latent_keys(lat, w_uk, g_kn)
        return _attend(qn[None], qr[None], kn, krr, lat, mask)[0]

    return lax.map(one, (page_table, q_nope, q_rope, c_kv, k_rope))


def _peer(x, wq, keys, u_tab, v_tab):
    shape = x.shape
    xf = x.reshape(-1, shape[-1])
    n = xf.shape[0]
    xf = jnp.pad(xf, ((0, (-n) % PEER_BLOCK), (0, 0)))
    kk = PEER_TOPK * PEER_TOPK

    def block(xb):
        q = (xb @ wq).reshape(PEER_BLOCK, PEER_HEADS, 2, PEER_HALF)
        s = jnp.einsum('qhcd,ckd->qhck', q, keys).astype(jnp.float32)
        s1, i1 = lax.top_k(s[:, :, 0], PEER_TOPK)
        s2, i2 = lax.top_k(s[:, :, 1], PEER_TOPK)
        cand = (s1[..., :, None] + s2[..., None, :]).reshape(PEER_BLOCK, PEER_HEADS, kk)
        cidx = (i1[..., :, None] * PEER_KEYS + i2[..., None, :]).reshape(PEER_BLOCK, PEER_HEADS, kk)
        top, sel = lax.top_k(cand, PEER_TOPK)
        idx = jnp.take_along_axis(cidx, sel, axis=-1)
        g = jax.nn.softmax(top, axis=-1)
        act = jax.nn.gelu(jnp.einsum('qhkd,qd->qhk', jnp.take(u_tab, idx, axis=0), xb).astype(jnp.float32))
        return jnp.einsum('qhk,qhkd->qd', (g * act).astype(xb.dtype), jnp.take(v_tab, idx, axis=0))

    out = lax.map(block, xf.reshape(-1, PEER_BLOCK, shape[-1]))
    return out.reshape(-1, shape[-1])[:n].reshape(shape)


def _layer(x, pos, h0, attend, norm_mix, w_in, norm_q_lora, w_uq, norm_kv_lora, w_uk, w_uv,
           g_qn, g_qr, g_kn, g_kr, a_re, a_im, log_dt, b, c, d_skip, w_glu, w_out,
           norm_ffn, peer_wq, peer_keys, peer_u, peer_v):
    xn = _rmsnorm(x, norm_mix)
    z = xn @ w_in
    u = z[..., :SSM_WIDTH]
    cq = z[..., SSM_WIDTH:SSM_WIDTH + Q_LORA]
    ckv = z[..., SSM_WIDTH + Q_LORA:SSM_WIDTH + Q_LORA + KV_LORA]
    kr = z[..., SSM_WIDTH + Q_LORA + KV_LORA:]
    cos, sin = _rope_angles(pos)
    q = (_rmsnorm(cq, norm_q_lora) @ w_uq).reshape(*cq.shape[:-1], N_HEADS, D_NOPE + D_ROPE)
    q_nope = _rmsnorm(q[..., :D_NOPE], g_qn)
    q_rope = _rope(_rmsnorm(q[..., D_NOPE:], g_qr), cos[:, None], sin[:, None])
    c_kv = _rmsnorm(ckv, norm_kv_lora)
    k_rope = _rope(_rmsnorm(kr, g_kr), cos, sin)
    o_lat = attend(q_nope, q_rope, c_kv, k_rope, w_uk, g_kn)
    o_att = jnp.einsum('bthc,chd->bthd', o_lat, w_uv).reshape(*x.shape[:-1], ATT_WIDTH)
    y_ssm, h_last = _s5(u, h0, a_re, a_im, log_dt, b, c, d_skip)
    gl = jax.nn.gelu(y_ssm) @ w_glu
    glu = gl[..., :SSM_WIDTH] * jax.nn.sigmoid(gl[..., SSM_WIDTH:])
    x = (x + jnp.concatenate([glu, o_att.astype(glu.dtype)], axis=-1) @ w_out).astype(x.dtype)
    x = (x + _peer(_rmsnorm(x, norm_ffn), peer_wq, peer_keys, peer_u, peer_v)).astype(x.dtype)
    return x, c_kv, k_rope, jnp.stack([h_last.real, h_last.imag], axis=-1)


def setup_inputs(seed: int = 0) -> dict:
    key = jax.random.key(seed)
    ks = jax.random.split(key, 32)
    f32 = jnp.float32
    n_pages = PAST_LEN // PAGE_SIZE
    n_used = DEC_BATCH * n_pages
    n_phys = n_used + (n_used + 3) // 4

    def nrm(k, shape, scale):
        return jax.random.normal(k, shape, f32) * scale

    def gain(k, shape):
        return 1.0 + 0.02 * jax.random.normal(k, shape, f32)

    page_table = jax.random.permutation(ks[5], n_phys)[:n_used].reshape(DEC_BATCH, n_pages).astype(jnp.int32)
    n_idx = jnp.arange(SSM_STATE, dtype=f32)
    return {
        'x_prompt': nrm(ks[0], (BATCH, SEQ, D_MODEL), 1.0),
        'x_sample': nrm(ks[1], (DEC_BATCH, DEC_SEQ, D_MODEL), 1.0),
        'cache_kv_latent': nrm(ks[2], (DEPTH, n_phys, PAGE_SIZE, KV_LORA), 1.0),
        'cache_k_rope': nrm(ks[3], (DEPTH, n_phys, PAGE_SIZE, D_ROPE), 1.0),
        'state_ssm': nrm(ks[4], (DEPTH, DEC_BATCH, SSM_GROUPS, SSM_STATE, 2), 0.1),
        'page_table': page_table,
        'norm_mix': gain(ks[6], (DEPTH, D_MODEL)),
        'w_in': nrm(ks[7], (DEPTH, D_MODEL, IN_WIDTH), D_MODEL ** -0.5),
        'norm_q_lora': gain(ks[8], (DEPTH, Q_LORA)),
        'w_uq': nrm(ks[9], (DEPTH, Q_LORA, N_HEADS * (D_NOPE + D_ROPE)), Q_LORA ** -0.5),
        'norm_kv_lora': gain(ks[10], (DEPTH, KV_LORA)),
        'w_uk': nrm(ks[11], (DEPTH, KV_LORA, N_HEADS, D_NOPE), KV_LORA ** -0.5),
        'w_uv': nrm(ks[12], (DEPTH, KV_LORA, N_HEADS, D_V), KV_LORA ** -0.5),
        'qk_gain_q_nope': gain(ks[13], (DEPTH, D_NOPE)),
        'qk_gain_q_rope': gain(ks[14], (DEPTH, D_ROPE)),
        'qk_gain_k_nope': gain(ks[15], (DEPTH, D_NOPE)),
        'qk_gain_k_rope': gain(ks[16], (DEPTH, D_ROPE)),
        'ssm_a_re': -0.5 + nrm(ks[17], (DEPTH, SSM_GROUPS, SSM_STATE), 0.01),
        'ssm_a_im': math.pi * n_idx + nrm(ks[18], (DEPTH, SSM_GROUPS, SSM_STATE), 0.01),
        'ssm_log_dt': jax.random.uniform(ks[19], (DEPTH, SSM_GROUPS), f32, math.log(1e-3), math.log(1e-1)),
        'ssm_b': nrm(ks[20], (DEPTH, SSM_GROUPS, SSM_STATE, SSM_GROUP, 2), (2.0 * SSM_GROUP) ** -0.5),
        'ssm_c': nrm(ks[21], (DEPTH, SSM_GROUPS, SSM_GROUP, SSM_STATE, 2), (2.0 * SSM_STATE) ** -0.5),
        'ssm_d': nrm(ks[22], (DEPTH, SSM_WIDTH), 1.0),
        'w_glu': nrm(ks[23], (DEPTH, SSM_WIDTH, 2 * SSM_WIDTH), SSM_WIDTH ** -0.5),
        'w_out': nrm(ks[24], (DEPTH, MIX_WIDTH, D_MODEL), MIX_WIDTH ** -0.5),
        'norm_ffn': gain(ks[25], (DEPTH, D_MODEL)),
        'peer_wq': nrm(ks[26], (DEPTH, D_MODEL, PEER_HEADS * PEER_QUERY), D_MODEL ** -0.5),
        'peer_keys': nrm(ks[27], (DEPTH, 2, PEER_KEYS, PEER_HALF), PEER_HALF ** -0.5),
        'peer_u': nrm(ks[28], (DEPTH, PEER_EXPERTS, D_MODEL), D_MODEL ** -0.5),
        'peer_v': nrm(ks[29], (DEPTH, PEER_EXPERTS, D_MODEL), (PEER_HEADS * PEER_TOPK) ** -0.5),
    }


def reference(x_prompt, x_sample, cache_kv_latent, cache_k_rope, state_ssm, page_table,
              norm_mix, w_in, norm_q_lora, w_uq, norm_kv_lora, w_uk, w_uv,
              qk_gain_q_nope, qk_gain_q_rope, qk_gain_k_nope, qk_gain_k_rope,
              ssm_a_re, ssm_a_im, ssm_log_dt, ssm_b, ssm_c, ssm_d, w_glu, w_out,
              norm_ffn, peer_wq, peer_keys, peer_u, peer_v):
    seq = x_prompt.shape[1]
    dec_seq = x_sample.shape[1]
    past = page_table.shape[1] * PAGE_SIZE
    pos_p = jnp.arange(seq)
    pos_s = past + jnp.arange(dec_seq)
    h0_p = jnp.zeros((x_prompt.shape[0], SSM_GROUPS, SSM_STATE), jnp.complex64)
    xp, xs = x_prompt, x_sample
    lat_p, kr_p, ssm_p, lat_s, kr_s, ssm_s = [], [], [], [], [], []
    for l in range(DEPTH):
        params = (norm_mix[l], w_in[l], norm_q_lora[l], w_uq[l], norm_kv_lora[l], w_uk[l], w_uv[l],
                  qk_gain_q_nope[l], qk_gain_q_rope[l], qk_gain_k_nope[l], qk_gain_k_rope[l],
                  ssm_a_re[l], ssm_a_im[l], ssm_log_dt[l], ssm_b[l], ssm_c[l], ssm_d[l],
                  w_glu[l], w_out[l], norm_ffn[l], peer_wq[l], peer_keys[l], peer_u[l], peer_v[l])
        st = state_ssm[l].astype(jnp.float32)
        h0_s = lax.complex(st[..., 0], st[..., 1])
        attend_s = functools.partial(_sample_attention, cache_lat=cache_kv_latent, cache_kr=cache_k_rope,
                                     page_table=page_table, layer=l)
        xp, c1, k1, s1 = _layer(xp, pos_p, h0_p, _prompt_attention, *params)
        xs, c2, k2, s2 = _layer(xs, pos_s, h0_s, attend_s, *params)
        lat_p.append(c1)
        kr_p.append(k1)
        ssm_p.append(s1)
        lat_s.append(c2)
        kr_s.append(k2)
        ssm_s.append(s2)
    return (xp, xs, jnp.stack(lat_p), jnp.stack(kr_p), jnp.stack(ssm_p),
            jnp.stack(lat_s), jnp.stack(kr_s), jnp.stack(ssm_s))
```

```python
import functools
import math

import jax
import jax.numpy as jnp
from jax import lax
from jax.experimental import pallas as pl
from jax.experimental.pallas import tpu as pltpu

F32 = jnp.float32
BF16 = jnp.bfloat16

D_MODEL = 1024
SSM_WIDTH = 512
SSM_GROUP = 16
SSM_GROUPS = 32
SSM_STATE = 64
STATE_WIDTH = 2 * SSM_GROUPS * SSM_STATE
D_NOPE = 64
D_ROPE = 32
D_V = 64
N_HEADS = 8
ATT_WIDTH = N_HEADS * D_V
Q_LORA = 384
KV_LORA = 256
IN_WIDTH = SSM_WIDTH + Q_LORA + KV_LORA + D_ROPE
ROPE_THETA = 10000.0
ATTN_SCALE = 1.0 / math.sqrt(D_NOPE + D_ROPE)
PAGE_SIZE = 128
PEER_HEADS = 8
PEER_KEYS = 128
PEER_HALF = 128
PEER_TOPK = 16
EPS = 1e-6

LANES = 128
HEAD_BLOCK = LANES
ROPE_LO = D_NOPE
IN_PAD = SSM_WIDTH + Q_LORA + KV_LORA + LANES
VMEM_LIMIT = 48 * 1024 * 1024

TOK_TILE = 256
Q_TILE = 512
S5_CHUNK = 8
PAGES_PER_STEP = 8
PEER_TOK_TILE = 512
PEER_EXP_TILE = 256
NEG_BIG = -1e30


def _cparams(sem):
    return pltpu.CompilerParams(dimension_semantics=sem, vmem_limit_bytes=VMEM_LIMIT)


def _dot(a, b):
    return jnp.dot(a, b, preferred_element_type=F32)


def _dot_nt(a, b):
    return lax.dot_general(a, b, (((1,), (1,)), ((), ())), preferred_element_type=F32)


def _split(x):
    hi = x.astype(BF16)
    lo = (x - hi.astype(F32)).astype(BF16)
    return hi, lo


def _mm(x, w):
    if w.dtype == BF16:
        return _dot(x.astype(BF16), w)
    xh, xl = _split(x)
    wh, wl = _split(w)
    return _dot(xh, wh) + (_dot(xl, wh) + _dot(xh, wl))


def _mm_nt(w, x):
    if w.dtype == BF16:
        return _dot_nt(w, x.astype(BF16))
    xh, xl = _split(x)
    wh, wl = _split(w)
    return _dot_nt(wh, xh) + (_dot_nt(wh, xl) + _dot_nt(wl, xh))


def _rms(x, gain):
    return x * lax.rsqrt(jnp.mean(x * x, axis=-1, keepdims=True) + EPS) * gain


def _segment_sumsq(x, seg):
    hi, lo = _split(x * x)
    return _dot(hi, seg) + _dot(lo, seg)


def _rope_block(x, cos, sin, first_half):
    swapped = jnp.where(first_half, pltpu.roll(x, LANES - D_ROPE // 2, 1), pltpu.roll(x, D_ROPE // 2, 1))
    return x * cos + swapped * sin


def _proj_body(x_ref, cos_ref, sin_ref, gmix_ref, win_ref, gq_ref, wuq_ref, gkv_ref, wuk_ref, wuvt_ref,
               seg_ref, icnt_ref, qgain_ref, kgain_ref, krgain_ref,
               u_ref, q_ref, ckv_ref, kr_ref, kcat_ref, vt_ref):
    x = x_ref[...]
    xn = _rms(x, gmix_ref[...])
    z = _mm(xn, win_ref[...])
    u_ref[...] = z[:, :SSM_WIDTH]
    cq = z[:, SSM_WIDTH:SSM_WIDTH + Q_LORA]
    ckv = z[:, SSM_WIDTH + Q_LORA:SSM_WIDTH + Q_LORA + KV_LORA]
    krc = z[:, SSM_WIDTH + Q_LORA + KV_LORA:]

    cos = cos_ref[...]
    sin = sin_ref[...]
    lane = lax.broadcasted_iota(jnp.int32, cos.shape, 1)
    first_half = lane < ROPE_LO + D_ROPE // 2
    seg = seg_ref[...]
    icnt = icnt_ref[...]

    ckvn = _rms(ckv, gkv_ref[...])
    ckv_ref[...] = ckvn

    krn = krc * lax.rsqrt(jnp.sum(krc * krc, axis=-1, keepdims=True) * (1.0 / D_ROPE) + EPS) * krgain_ref[...]
    krr = _rope_block(krn, cos, sin, first_half)
    kr_ref[...] = krr[:, ROPE_LO:ROPE_LO + D_ROPE]

    qp = _mm(_rms(cq, gq_ref[...]), wuq_ref[...])
    kp = _mm(ckvn, wuk_ref[...])
    qgain = qgain_ref[...]
    kgain = kgain_ref[...]
    q_blocks = []
    k_blocks = []
    for h in range(N_HEADS):
        sl = slice(h * HEAD_BLOCK, (h + 1) * HEAD_BLOCK)
        qh = qp[:, sl]
        qh = qh * lax.rsqrt(_segment_sumsq(qh, seg) * icnt + EPS) * qgain
        q_blocks.append(_rope_block(qh, cos, sin, first_half).astype(BF16))
        kh = kp[:, sl]
        kh = kh * lax.rsqrt(_segment_sumsq(kh, seg) * icnt + EPS) * kgain
        k_blocks.append((kh + krr).astype(BF16))
    q_ref[...] = jnp.concatenate(q_blocks, axis=1)
    kcat_ref[...] = jnp.concatenate(k_blocks, axis=1)

    vt = _mm_nt(wuvt_ref[...], ckvn)
    vt_ref[...] = vt.astype(BF16).reshape(N_HEADS, 1, D_V, vt.shape[1])


def _proj(x, cos, sin, w, precise):
    n = x.shape[0]
    tm = TOK_TILE
    nt = n // tm
    cos_tiles = cos.shape[0] // tm
    wsel = w['f32'] if precise else w['bf16']
    full = lambda a: pl.BlockSpec(a.shape, lambda i: (0,) * a.ndim)
    tok = lambda width: pl.BlockSpec((tm, width), lambda i: (i, 0))
    tab = pl.BlockSpec((tm, LANES), lambda i: (i % cos_tiles, 0))
    consts = (w['gmix'], wsel['win'], w['gq'], wsel['wuq'], w['gkv'], wsel['wuk'], wsel['wuvt'],
              w['seg'], w['icnt'], w['qgain'], w['kgain'], w['krgain'])
    out_shape = (
        jax.ShapeDtypeStruct((n, SSM_WIDTH), F32),
        jax.ShapeDtypeStruct((n, N_HEADS * HEAD_BLOCK), BF16),
        jax.ShapeDtypeStruct((n, KV_LORA), F32),
        jax.ShapeDtypeStruct((n, D_ROPE), F32),
        jax.ShapeDtypeStruct((n, N_HEADS * HEAD_BLOCK), BF16),
        jax.ShapeDtypeStruct((N_HEADS, nt, D_V, tm), BF16),
    )
    out_specs = (tok(SSM_WIDTH), tok(N_HEADS * HEAD_BLOCK), tok(KV_LORA), tok(D_ROPE),
                 tok(N_HEADS * HEAD_BLOCK),
                 pl.BlockSpec((N_HEADS, 1, D_V, tm), lambda i: (0, i, 0, 0)))
    return pl.pallas_call(
        _proj_body,
        grid=(nt,),
        in_specs=[tok(D_MODEL), tab, tab] + [full(a) for a in consts],
        out_specs=out_specs,
        out_shape=out_shape,
        compiler_params=_cparams(("parallel",)),
        name="proj",
    )(x, cos, sin, *consts)


def _prompt_attn_body(q_ref, k_ref, vt_ref, o_ref):
    tq = q_ref.shape[0]
    tk = vt_ref.shape[3]
    qi = pl.program_id(2)
    q = q_ref[...]
    per_q = tq // tk

    def step(c, carry, masked):
        m, l, acc = carry
        start = pl.multiple_of(c * tk, tk)
        k = k_ref[pl.ds(start, tk), :]
        s = _dot_nt(k, q)
        if masked:
            kpos = c * tk + lax.broadcasted_iota(jnp.int32, s.shape, 0)
            qpos = qi * tq + lax.broadcasted_iota(jnp.int32, s.shape, 1)
            s = jnp.where(kpos <= qpos, s, -jnp.inf)
        m_new = jnp.maximum(m, jnp.max(s, axis=0, keepdims=True))
        p = jnp.exp(s - m_new)
        alpha = jnp.exp(m - m_new)
        l = alpha * l + jnp.sum(p, axis=0, keepdims=True)
        acc = alpha * acc + _dot(vt_ref[0, c], p.astype(BF16))
        return m_new, l, acc

    init = (jnp.full((1, tq), NEG_BIG, F32), jnp.zeros((1, tq), F32), jnp.zeros((D_V, tq), F32))
    n_full = qi * per_q
    carry = lax.fori_loop(0, n_full, lambda c, cr: step(c, cr, False), init)
    for d in range(per_q):
        carry = step(n_full + d, carry, True)
    m, l, acc = carry
    o_ref[...] = (acc / l).astype(o_ref.dtype)


def _prompt_attention(q, kcat, vt4, batch, seq):
    tq = Q_TILE
    tk = vt4.shape[3]
    nq = seq // tq
    nkc = seq // tk
    return pl.pallas_call(
        _prompt_attn_body,
        grid=(batch, N_HEADS, nq),
        in_specs=[
            pl.BlockSpec((tq, HEAD_BLOCK), lambda b, h, i: (b * nq + i, h)),
            pl.BlockSpec((seq, HEAD_BLOCK), lambda b, h, i: (b, h)),
            pl.BlockSpec((1, nkc, D_V, tk), lambda b, h, i: (h, b, 0, 0)),
        ],
        out_specs=pl.BlockSpec((D_V, tq), lambda b, h, i: (h, b * nq + i)),
        out_shape=jax.ShapeDtypeStruct((ATT_WIDTH, batch * seq), BF16),
        compiler_params=_cparams(("parallel", "parallel", "arbitrary")),
        name="prompt_attn",
    )(q, kcat, vt4)


def _sample_attn_body(pt_ref, q_ref, newlat_ref, newkr_ref, hmask_ref, wabs_ref, rsel_ref, wukt_ref, wuv_ref,
                      *rest):
    g = PAGES_PER_STEP
    lat_refs = rest[:g]
    kr_refs = rest[g:2 * g]
    o_ref = rest[2 * g]
    qabs_sc, qr_sc, m_sc, l_sc, acc_sc = rest[2 * g + 1:]
    j = pl.program_id(1)
    t_new = q_ref.shape[1]
    rows = t_new * N_HEADS

    @pl.when(j == 0)
    def _():
        q = q_ref[0].astype(F32)
        qb = jnp.broadcast_to(q[:, None, :], (t_new, N_HEADS, q.shape[1])).reshape(rows, q.shape[1])
        qbd = (qb * hmask_ref[...]).astype(BF16)
        qabs_sc[...] = _dot(qbd, wabs_ref[...]).astype(BF16)
        qr_sc[...] = _dot(qbd, rsel_ref[...]).astype(BF16)
        m_sc[...] = jnp.full(m_sc.shape, NEG_BIG, F32)
        l_sc[...] = jnp.zeros(l_sc.shape, F32)
        acc_sc[...] = jnp.zeros(acc_sc.shape, F32)

    def attend(lat32, kr32, causal):
        lat = lat32.astype(BF16)
        tk = lat.shape[0]
        knt = _dot_nt(wukt_ref[...], lat)
        ssq = jnp.sum((knt * knt).reshape(N_HEADS, D_NOPE, tk), axis=1)
        rinv = lax.rsqrt(ssq * (1.0 / D_NOPE) + EPS)
        rinv = jnp.broadcast_to(rinv[None], (t_new, N_HEADS, tk)).reshape(rows, tk)
        s = _dot_nt(qabs_sc[...], lat) * rinv + _dot_nt(qr_sc[...], kr32.astype(BF16))
        if causal:
            q_tok = lax.broadcasted_iota(jnp.int32, (t_new, N_HEADS, tk), 0).reshape(rows, tk)
            k_tok = lax.broadcasted_iota(jnp.int32, (rows, tk), 1)
            s = jnp.where(k_tok <= q_tok, s, -jnp.inf)
        m = m_sc[...]
        m_new = jnp.maximum(m, jnp.max(s, axis=1, keepdims=True))
        p = jnp.exp(s - m_new)
        alpha = jnp.exp(m - m_new)
        l_sc[...] = alpha * l_sc[...] + jnp.sum(p, axis=1, keepdims=True)
        acc_sc[...] = alpha * acc_sc[...] + _dot(p.astype(BF16), lat)
        m_sc[...] = m_new

    attend(jnp.concatenate([r[0, 0] for r in lat_refs], axis=0),
           jnp.concatenate([r[0, 0] for r in kr_refs], axis=0), causal=False)

    @pl.when(j == pl.num_programs(1) - 1)
    def _():
        attend(newlat_ref[0], newkr_ref[0], causal=True)
        o_lat = acc_sc[...] / l_sc[...]
        full = _dot(o_lat.astype(BF16), wuv_ref[...])
        shape3 = (t_new, N_HEADS, ATT_WIDTH)
        col = lax.broadcasted_iota(jnp.int32, shape3, 2)
        lo = lax.broadcasted_iota(jnp.int32, shape3, 1) * D_V
        sel = jnp.where((col >= lo) & (col < lo + D_V), full.reshape(shape3), 0.0)
        o_ref[0] = jnp.sum(sel, axis=1)


def _sample_attention(q, new_lat, new_kr, cache_lat, cache_kr, page_table, w):
    bd, t_new, _ = q.shape
    n_pages = page_table.shape[1]
    g = PAGES_PER_STEP
    steps = n_pages // g
    rows = t_new * N_HEADS
    per_b = lambda a: pl.BlockSpec((1,) + a.shape[1:], lambda b, j, pt: (b, 0, 0))
    full = lambda a: pl.BlockSpec(a.shape, lambda b, j, pt: (0,) * a.ndim)

    def page_spec(width, k):
        return pl.BlockSpec((1, 1, PAGE_SIZE, width), lambda b, j, pt: (0, pt[b, j * g + k], 0, 0))

    consts = (w['hmask'], w['wabs'], w['rsel'], w['wukt'], w['wuv'])
    grid_spec = pltpu.PrefetchScalarGridSpec(
        num_scalar_prefetch=1,
        grid=(bd, steps),
        in_specs=[per_b(q), per_b(new_lat), per_b(new_kr)] + [full(a) for a in consts]
        + [page_spec(KV_LORA, k) for k in range(g)] + [page_spec(D_ROPE, k) for k in range(g)],
        out_specs=pl.BlockSpec((1, t_new, ATT_WIDTH), lambda b, j, pt: (b, 0, 0)),
        scratch_shapes=[
            pltpu.VMEM((rows, KV_LORA), BF16),
            pltpu.VMEM((rows, D_ROPE), BF16),
            pltpu.VMEM((rows, 1), F32),
            pltpu.VMEM((rows, 1), F32),
            pltpu.VMEM((rows, KV_LORA), F32),
        ],
    )
    return pl.pallas_call(
        _sample_attn_body,
        grid_spec=grid_spec,
        out_shape=jax.ShapeDtypeStruct((bd, t_new, ATT_WIDTH), F32),
        compiler_params=_cparams(("parallel", "arbitrary")),
        name="sample_attn",
    )(page_table, q, new_lat, new_kr, *consts, *([cache_lat] * g), *([cache_kr] * g))


def _matmul_body(a_ref, b_ref, o_ref, acc_ref):
    @pl.when(pl.program_id(2) == 0)
    def _():
        acc_ref[...] = jnp.zeros(acc_ref.shape, F32)

    acc_ref[...] += _mm(a_ref[...].astype(F32), b_ref[...])

    @pl.when(pl.program_id(2) == pl.num_programs(2) - 1)
    def _():
        o_ref[...] = acc_ref[...]


def _pick(n, pref):
    for t in pref:
        if n % t == 0:
            return t
    return n


def _matmul(a, b):
    m, k = a.shape
    n = b.shape[1]
    tm = _pick(m, (512, 256, 128))
    tn = _pick(n, (1024, 512, 256, 128))
    tk = _pick(k, (1024, 512, 256, 128))
    return pl.pallas_call(
        _matmul_body,
        grid=(m // tm, n // tn, k // tk),
        in_specs=[pl.BlockSpec((tm, tk), lambda i, j, l: (i, l)),
                  pl.BlockSpec((tk, tn), lambda i, j, l: (l, j))],
        out_specs=pl.BlockSpec((tm, tn), lambda i, j, l: (i, j)),
        out_shape=jax.ShapeDtypeStruct((m, n), F32),
        scratch_shapes=[pltpu.VMEM((tm, tn), F32)],
        compiler_params=_cparams(("parallel", "parallel", "arbitrary")),
        name="s5_matmul",
    )(a, b)


def _cmul_add(alr, ali, hr, hi, sr, si):
    return alr * hr - ali * hi + sr, alr * hi + ali * hr + si


def _scan_seq_body(al_ref, h0_ref, s_ref, hin_ref, hl_ref):
    half = STATE_WIDTH // 2
    alr = al_ref[:, :half]
    ali = al_ref[:, half:]
    nch = s_ref.shape[0]

    def body(c, carry):
        hr, hi = carry
        hin_ref[pl.ds(c, 1), :] = jnp.concatenate([hr, hi], axis=1)
        row = s_ref[pl.ds(c, 1), :]
        return _cmul_add(alr, ali, hr, hi, row[:, :half], row[:, half:])

    h0 = h0_ref[0]
    hr, hi = lax.fori_loop(0, nch, body, (h0[:, :half], h0[:, half:]))
    hl_ref[0] = jnp.concatenate([hr, hi], axis=1)


def _scan_single_body(al_ref, h0_ref, s_ref, hl_ref):
    half = STATE_WIDTH // 2
    h0 = h0_ref[...]
    s = s_ref[...]
    hr, hi = _cmul_add(al_ref[:, :half], al_ref[:, half:], h0[:, :half], h0[:, half:], s[:, :half], s[:, half:])
    hl_ref[...] = jnp.concatenate([hr, hi], axis=1)


def _chunk_scan(a_pow_l, h0, s_loc, nseq, nch):
    wdt = STATE_WIDTH
    if nch == 1:
        hl = pl.pallas_call(
            _scan_single_body,
            out_shape=jax.ShapeDtypeStruct((nseq, wdt), F32),
            compiler_params=pltpu.CompilerParams(vmem_limit_bytes=VMEM_LIMIT),
            name="s5_scan_single",
        )(a_pow_l, h0, s_loc)
        return h0, hl
    hin, hl = pl.pallas_call(
        _scan_seq_body,
        grid=(nseq,),
        in_specs=[pl.BlockSpec((1, wdt), lambda b: (0, 0)),
                  pl.BlockSpec((1, 1, wdt), lambda b: (b, 0, 0)),
                  pl.BlockSpec((nch, wdt), lambda b: (b, 0))],
        out_specs=(pl.BlockSpec((nch, wdt), lambda b: (b, 0)),
                   pl.BlockSpec((1, 1, wdt), lambda b: (b, 0, 0))),
        out_shape=(jax.ShapeDtypeStruct((nseq * nch, wdt), F32),
                   jax.ShapeDtypeStruct((nseq, 1, wdt), F32)),
        compiler_params=_cparams(("parallel",)),
        name="s5_scan",
    )(a_pow_l, h0.reshape(nseq, 1, wdt), s_loc)
    return hin, hl.reshape(nseq, wdt)


def _s5_weights(a_re, a_im, log_dt, b, c, d_skip, chunk):
    g, p, hch = SSM_GROUPS, SSM_STATE, SSM_GROUP
    a_re, a_im = a_re.astype(F32), a_im.astype(F32)
    dt = jnp.exp(log_dt.astype(F32))[:, None]
    lr, li = a_re * dt, a_im * dt
    ar, ai = jnp.exp(lr) * jnp.cos(li), jnp.exp(lr) * jnp.sin(li)
    den = a_re * a_re + a_im * a_im
    fr = ((ar - 1.0) * a_re + ai * a_im) / den
    fi = (ai * a_re - (ar - 1.0) * a_im) / den
    bre, bim = b[..., 0].astype(F32), b[..., 1].astype(F32)
    bbr = fr[..., None] * bre - fi[..., None] * bim
    bbi = fr[..., None] * bim + fi[..., None] * bre
    cre, cim = c[..., 0].astype(F32), c[..., 1].astype(F32)
    k = jnp.arange(chunk + 1, dtype=F32)[:, None, None]
    pr = jnp.exp(k * lr[None]) * jnp.cos(k * li[None])
    pi = jnp.exp(k * lr[None]) * jnp.sin(k * li[None])
    eye = jnp.eye(g, dtype=F32)

    rev = jnp.arange(chunk - 1, -1, -1)
    st_r = pr[rev][..., None] * bbr[None] - pi[rev][..., None] * bbi[None]
    st_i = pr[rev][..., None] * bbi[None] + pi[rev][..., None] * bbr[None]
    st = jnp.stack([st_r, st_i], axis=0)
    w_state = jnp.einsum('rsgph,gq->sghrqp', st, eye).reshape(chunk * SSM_WIDTH, STATE_WIDTH)

    ab_r = pr[:chunk, :, :, None] * bbr[None] - pi[:chunk, :, :, None] * bbi[None]
    ab_i = pr[:chunk, :, :, None] * bbi[None] + pi[:chunk, :, :, None] * bbr[None]
    kk = jnp.einsum('gop,kgph->kgoh', cre, ab_r) - jnp.einsum('gop,kgph->kgoh', cim, ab_i)
    kk = kk.at[0].add(d_skip.astype(F32).reshape(g, hch)[:, :, None] * jnp.eye(hch, dtype=F32)[None])
    s_idx = jnp.arange(chunk)[:, None]
    t_idx = jnp.arange(chunk)[None, :]
    lag = jnp.clip(t_idx - s_idx, 0, chunk - 1)
    kfull = jnp.where((s_idx <= t_idx)[:, :, None, None, None], kk[lag], 0.0)
    w_toep = jnp.einsum('stgoh,gq->sghtqo', kfull, eye).reshape(chunk * SSM_WIDTH, chunk * SSM_WIDTH)

    ca_r = cre[None] * pr[1:, :, None, :] - cim[None] * pi[1:, :, None, :]
    ca_i = cre[None] * pi[1:, :, None, :] + cim[None] * pr[1:, :, None, :]
    ca = jnp.stack([ca_r, -ca_i], axis=0)
    w_carry = jnp.einsum('rtgop,gq->rqptgo', ca, eye).reshape(STATE_WIDTH, chunk * SSM_WIDTH)

    a_pow_l = jnp.concatenate([pr[chunk].reshape(1, -1), pi[chunk].reshape(1, -1)], axis=1)
    return w_state, jnp.concatenate([w_toep, w_carry], axis=0), a_pow_l


def _s5(u, h0, ssm, nseq, t_len, chunk, precise):
    w_state, w_out, a_pow_l = ssm
    if not precise:
        w_state, w_out = w_state.astype(BF16), w_out.astype(BF16)
    nch = t_len // chunk
    u_cf = u.reshape(nseq * nch, chunk * SSM_WIDTH)
    s_loc = _matmul(u_cf, w_state)
    h_in, h_last = _chunk_scan(a_pow_l, h0, s_loc, nseq, nch)
    y_cf = _matmul(jnp.concatenate([u_cf, h_in], axis=1), w_out)
    return y_cf.reshape(nseq * t_len, SSM_WIDTH), h_last


def _mix_body(x_ref, y_ref, o_ref, wglu_ref, wout_ref, gffn_ref, x1_ref, xn_ref, *, att_transposed):
    gl = _dot(jax.nn.gelu(y_ref[...]).astype(BF16), wglu_ref[...])
    glu = gl[:, :SSM_WIDTH] * jax.nn.sigmoid(gl[:, SSM_WIDTH:])
    o = o_ref[...]
    if att_transposed:
        o = o.astype(F32).T
    x1 = x_ref[...] + _dot(glu.astype(BF16), wout_ref[:SSM_WIDTH, :]) + _dot(o.astype(BF16), wout_ref[SSM_WIDTH:, :])
    x1_ref[...] = x1
    xn_ref[...] = _rms(x1, gffn_ref[...]).astype(BF16)


def _mix(x, y, o, w, att_transposed):
    n = x.shape[0]
    tm = TOK_TILE
    tok = lambda width: pl.BlockSpec((tm, width), lambda i: (i, 0))
    full = lambda a: pl.BlockSpec(a.shape, lambda i: (0,) * a.ndim)
    o_spec = pl.BlockSpec((ATT_WIDTH, tm), lambda i: (0, i)) if att_transposed else tok(ATT_WIDTH)
    consts = (w['wglu'], w['wout'], w['gffn'])
    return pl.pallas_call(
        functools.partial(_mix_body, att_transposed=att_transposed),
        grid=(n // tm,),
        in_specs=[tok(D_MODEL), tok(SSM_WIDTH), o_spec] + [full(a) for a in consts],
        out_specs=(tok(D_MODEL), tok(D_MODEL)),
        out_shape=(jax.ShapeDtypeStruct((n, D_MODEL), F32), jax.ShapeDtypeStruct((n, D_MODEL), BF16)),
        compiler_params=_cparams(("parallel",)),
        name="mix",
    )(x, y, o, *consts)


def _desc_top(vals_fn, count):
    tops = []
    prev = None
    for _ in range(count):
        cur = vals_fn(prev)
        tops.append(cur)
        prev = cur
    return tops


def _peer_route_body(xn_ref, wq_ref, keys_ref, thr_ref, cc_ref, s2_ref, e2_ref):
    xn = xn_ref[...]
    q = _dot(xn, wq_ref[...]).astype(BF16)
    tm = xn.shape[0]
    k1 = keys_ref[0]
    k2 = keys_ref[1]
    tops1, tops2, s1_all, s2_all = [], [], [], []
    for h in range(PEER_HEADS):
        base = h * 2 * PEER_HALF
        s1 = _dot_nt(k1, q[:, base:base + PEER_HALF])
        s2 = _dot_nt(k2, q[:, base + PEER_HALF:base + 2 * PEER_HALF])
        s1_all.append(s1)
        s2_all.append(s2)
        for s, tops in ((s1, tops1), (s2, tops2)):
            def nxt(prev, s=s):
                cand = s if prev is None else jnp.where(s < prev, s, -jnp.inf)
                return jnp.max(cand, axis=0, keepdims=True)
            tops.append(_desc_top(nxt, PEER_TOPK))
    a = [jnp.concatenate([tops1[h][r] for h in range(PEER_HEADS)], axis=0) for r in range(PEER_TOPK)]
    b = [jnp.concatenate([tops2[h][r] for h in range(PEER_HEADS)], axis=0) for r in range(PEER_TOPK)]
    cands = [a[i] + b[j] for i in range(PEER_TOPK) for j in range(PEER_TOPK) if (i + 1) * (j + 1) <= PEER_TOPK]

    def nxt_sum(prev):
        best = None
        for cnd in cands:
            v = cnd if prev is None else jnp.where(cnd < prev, cnd, -jnp.inf)
            best = v if best is None else jnp.maximum(best, v)
        return best

    tsum = _desc_top(nxt_sum, PEER_TOPK)
    tau = tsum[-1]
    zsum = tsum[0] * 0.0
    for t in tsum:
        zsum = zsum + jnp.exp(t - tsum[0])
    inv_z = 1.0 / zsum
    for h in range(PEER_HEADS):
        row = slice(h, h + 1)
        thr = jnp.full(s1_all[h].shape, jnp.inf, F32)
        for r in range(PEER_TOPK):
            thr = jnp.where(s1_all[h] + b[r][row] >= tau[row], b[r][row], thr)
        thr_ref[h] = thr
        cc_ref[h] = jnp.exp(s1_all[h] - a[0][row]) * inv_z[row]
        s2_ref[h] = s2_all[h]
        e2_ref[h] = jnp.exp(s2_all[h] - b[0][row])


def _peer_route(xn, w):
    n = xn.shape[0]
    tm = TOK_TILE
    spec = pl.BlockSpec((PEER_HEADS, PEER_KEYS, tm), lambda i: (0, 0, i))
    shp = jax.ShapeDtypeStruct((PEER_HEADS, PEER_KEYS, n), F32)
    return pl.pallas_call(
        _peer_route_body,
        grid=(n // tm,),
        in_specs=[pl.BlockSpec((tm, D_MODEL), lambda i: (i, 0)),
                  pl.BlockSpec(w['peer_wq'].shape, lambda i: (0, 0)),
                  pl.BlockSpec(w['peer_keys'].shape, lambda i: (0, 0, 0))],
        out_specs=(spec, spec, spec, spec),
        out_shape=(shp, shp, shp, shp),
        compiler_params=_cparams(("parallel",)),
        name="peer_route",
    )(xn, w['peer_wq'], w['peer_keys'])


def _peer_expert_body(xn_ref, x1_ref, u_ref, vt_ref, thr_ref, cc_ref, s2_ref, e2_ref, o_ref, acc_ref):
    j = pl.program_id(1)
    te = u_ref.shape[0]

    @pl.when(j == 0)
    def _():
        acc_ref[...] = jnp.zeros(acc_ref.shape, F32)

    act = jax.nn.gelu(_dot_nt(u_ref[...], xn_ref[...]))
    gates = []
    for sub in range(te // PEER_KEYS):
        i1 = j * (te // PEER_KEYS) + sub
        gate = None
        for h in range(PEER_HEADS):
            thr = thr_ref[h, pl.ds(i1, 1), :]
            cc = cc_ref[h, pl.ds(i1, 1), :]
            wgt = jnp.where(s2_ref[h] >= thr, e2_ref[h] * cc, 0.0)
            gate = wgt if gate is None else gate + wgt
        gates.append(gate)
    ga = (jnp.concatenate(gates, axis=0) * act).astype(BF16)
    acc_ref[...] += _dot(vt_ref[...], ga)

    @pl.when(j == pl.num_programs(1) - 1)
    def _():
        o_ref[...] = x1_ref[...] + acc_ref[...].T


def _peer_experts(xn, x1, routes, w):
    n = xn.shape[0]
    tm = _pick(n, (PEER_TOK_TILE, TOK_TILE))
    te = PEER_EXP_TILE
    n_exp = w['peer_u'].shape[0]
    tok = pl.BlockSpec((tm, D_MODEL), lambda i, j: (i, 0))
    route = pl.BlockSpec((PEER_HEADS, PEER_KEYS, tm), lambda i, j: (0, 0, i))
    return pl.pallas_call(
        _peer_expert_body,
        grid=(n // tm, n_exp // te),
        in_specs=[tok, tok,
                  pl.BlockSpec((te, D_MODEL), lambda i, j: (j, 0)),
                  pl.BlockSpec((D_MODEL, te), lambda i, j: (0, j)),
                  route, route, route, route],
        out_specs=tok,
        out_shape=jax.ShapeDtypeStruct((n, D_MODEL), F32),
        scratch_shapes=[pltpu.VMEM((D_MODEL, tm), F32)],
        compiler_params=_cparams(("parallel", "arbitrary")),
        name="peer_experts",
    )(xn, x1, w['peer_u'], w['peer_vt'], *routes)


def _rope_tables(pos):
    inv = ROPE_THETA ** (-jnp.arange(0, D_ROPE, 2, dtype=F32) / D_ROPE)
    ang = pos.astype(F32)[:, None] * inv[None, :]
    cos, sin = jnp.cos(ang), jnp.sin(ang)
    t = pos.shape[0]
    ones = jnp.ones((t, ROPE_LO), F32)
    tail = HEAD_BLOCK - ROPE_LO - D_ROPE
    cos_t = jnp.concatenate([ones, cos, cos, jnp.ones((t, tail), F32)], axis=1)
    sin_t = jnp.concatenate([0 * ones, -sin, sin, jnp.zeros((t, tail), F32)], axis=1)
    return cos_t, sin_t


def _head_blocks(wmat, lo, width):
    pad = jnp.zeros(wmat.shape[:2] + (HEAD_BLOCK,), wmat.dtype)
    return pad.at[:, :, lo:lo + width].set(wmat).reshape(wmat.shape[0], N_HEADS * HEAD_BLOCK)


def _prep_weights(norm_mix, w_in, norm_q_lora, w_uq, norm_kv_lora, w_uk, w_uv, g_qn, g_qr, g_kn, g_kr,
                  w_glu, w_out, norm_ffn, peer_wq, peer_keys, peer_u, peer_v):
    row = lambda v: v.astype(F32).reshape(1, -1)
    kr_cols = jnp.zeros((D_MODEL, LANES), F32).at[:, ROPE_LO:ROPE_LO + D_ROPE].set(w_in[:, IN_WIDTH - D_ROPE:])
    win = jnp.concatenate([w_in[:, :IN_WIDTH - D_ROPE], kr_cols], axis=1)
    wuq3 = w_uq.reshape(Q_LORA, N_HEADS, D_NOPE + D_ROPE)
    wuq = _head_blocks(wuq3, 0, D_NOPE + D_ROPE)
    wuk = _head_blocks(w_uk, 0, D_NOPE)
    wuvt = w_uv.reshape(KV_LORA, ATT_WIDTH).T
    lane = jnp.arange(HEAD_BLOCK)
    is_nope = lane < ROPE_LO
    is_rope = (lane >= ROPE_LO) & (lane < ROPE_LO + D_ROPE)
    seg = ((is_nope[:, None] & is_nope[None, :]) | (is_rope[:, None] & is_rope[None, :])).astype(BF16)
    icnt = jnp.where(is_nope, 1.0 / D_NOPE, jnp.where(is_rope, 1.0 / D_ROPE, 1.0)).astype(F32).reshape(1, -1)
    blockvec = lambda a, bvec: jnp.zeros((HEAD_BLOCK,), F32).at[:ROPE_LO].set(a).at[ROPE_LO:ROPE_LO + D_ROPE].set(bvec)
    qgain = (blockvec(g_qn, g_qr) * ATTN_SCALE).reshape(1, -1)
    kgain = blockvec(g_kn, jnp.zeros((D_ROPE,), F32)).reshape(1, -1)
    krgain = blockvec(jnp.zeros((D_NOPE,), F32), g_kr).reshape(1, -1)
    wabs = jnp.einsum('chd,d->hdc', w_uk, g_kn)
    wabs = jnp.zeros((N_HEADS, HEAD_BLOCK, KV_LORA), F32).at[:, :D_NOPE, :].set(wabs)
    wabs = wabs.reshape(N_HEADS * HEAD_BLOCK, KV_LORA)
    rsel = jnp.zeros((N_HEADS, HEAD_BLOCK, D_ROPE), F32).at[:, ROPE_LO:ROPE_LO + D_ROPE, :].set(
        jnp.broadcast_to(jnp.eye(D_ROPE, dtype=F32), (N_HEADS, D_ROPE, D_ROPE)))
    rsel = rsel.reshape(N_HEADS * HEAD_BLOCK, D_ROPE)
    head_of_lane = jnp.arange(N_HEADS * HEAD_BLOCK) // HEAD_BLOCK
    hmask8 = (head_of_lane[None, :] == jnp.arange(N_HEADS)[:, None]).astype(F32)
    return {
        'gmix': row(norm_mix), 'gq': row(norm_q_lora), 'gkv': row(norm_kv_lora),
        'seg': seg, 'icnt': icnt, 'qgain': qgain, 'kgain': kgain, 'krgain': krgain,
        'f32': {'win': win, 'wuq': wuq, 'wuk': wuk, 'wuvt': wuvt},
        'bf16': {'win': win.astype(BF16), 'wuq': wuq.astype(BF16), 'wuk': wuk.astype(BF16),
                 'wuvt': wuvt.astype(BF16)},
        'hmask8': hmask8, 'wabs': wabs.astype(BF16), 'rsel': rsel.astype(BF16),
        'wukt': w_uk.reshape(KV_LORA, N_HEADS * D_NOPE).T.astype(BF16),
        'wuv': w_uv.reshape(KV_LORA, ATT_WIDTH).astype(BF16),
        'wglu': w_glu.astype(BF16), 'wout': w_out.astype(BF16), 'gffn': row(norm_ffn),
        'peer_wq': peer_wq.astype(BF16), 'peer_keys': peer_keys.astype(BF16),
        'peer_u': peer_u.astype(BF16), 'peer_vt': peer_v.astype(BF16).T,
    }


def _peer(xn, x1, w):
    return _peer_experts(xn, x1, _peer_route(xn, w), w)


def kernel(x_prompt, x_sample, cache_kv_latent, cache_k_rope, state_ssm, page_table, norm_mix, w_in, norm_q_lora, w_uq, norm_kv_lora, w_uk, w_uv, qk_gain_q_nope, qk_gain_q_rope, qk_gain_k_nope, qk_gain_k_rope, ssm_a_re, ssm_a_im, ssm_log_dt, ssm_b, ssm_c, ssm_d, w_glu, w_out, norm_ffn, peer_wq, peer_keys, peer_u, peer_v):
    depth = norm_mix.shape[0]
    batch, seq, _ = x_prompt.shape
    dec_batch, dec_seq, _ = x_sample.shape
    past = page_table.shape[1] * PAGE_SIZE
    assert seq % Q_TILE == 0 and seq % S5_CHUNK == 0 and (dec_batch * dec_seq) % TOK_TILE == 0
    assert page_table.shape[1] % PAGES_PER_STEP == 0

    cos_p, sin_p = _rope_tables(jnp.arange(seq))
    cos_s, sin_s = _rope_tables(past + jnp.arange(dec_seq))
    reps = TOK_TILE // dec_seq
    cos_s, sin_s = jnp.tile(cos_s, (reps, 1)), jnp.tile(sin_s, (reps, 1))

    xp = x_prompt.reshape(batch * seq, D_MODEL)
    xs = x_sample.reshape(dec_batch * dec_seq, D_MODEL)
    outs = {k: [] for k in ('lat_p', 'kr_p', 'ssm_p', 'lat_s', 'kr_s', 'ssm_s')}
    for l in range(depth):
        w = _prep_weights(norm_mix[l], w_in[l], norm_q_lora[l], w_uq[l], norm_kv_lora[l], w_uk[l], w_uv[l],
                          qk_gain_q_nope[l], qk_gain_q_rope[l], qk_gain_k_nope[l], qk_gain_k_rope[l],
                          w_glu[l], w_out[l], norm_ffn[l], peer_wq[l], peer_keys[l], peer_u[l], peer_v[l])
        w['hmask'] = jnp.tile(w['hmask8'], (dec_seq, 1))
        ssm_args = (ssm_a_re[l], ssm_a_im[l], ssm_log_dt[l], ssm_b[l], ssm_c[l], ssm_d[l])

        u, q, ckv, kr, kcat, vt4 = _proj(xp, cos_p, sin_p, w, precise=False)
        o_t = _prompt_attention(q, kcat, vt4, batch, seq)
        h0 = jnp.zeros((batch, STATE_WIDTH), F32)
        y, h_last = _s5(u, h0, _s5_weights(*ssm_args, S5_CHUNK), batch, seq, S5_CHUNK, precise=False)
        x1, xn = _mix(xp, y, o_t, w, att_transposed=True)
        xp = _peer(xn, x1, w)
        outs['lat_p'].append(ckv.reshape(batch, seq, KV_LORA))
        outs['kr_p'].append(kr.reshape(batch, seq, D_ROPE))
        outs['ssm_p'].append(_state_out(h_last))

        u, q, ckv, kr, _, _ = _proj(xs, cos_s, sin_s, w, precise=True)
        pad_page = lambda a: jnp.pad(a.reshape(dec_batch, dec_seq, -1), ((0, 0), (0, PAGE_SIZE - dec_seq), (0, 0)))
        o = _sample_attention(q.reshape(dec_batch, dec_seq, -1), pad_page(ckv), pad_page(kr),
                              cache_kv_latent[l:l + 1], cache_k_rope[l:l + 1], page_table, w)
        h0 = _state_in(state_ssm[l])
        y, h_last = _s5(u, h0, _s5_weights(*ssm_args, dec_seq), dec_batch, dec_seq, dec_seq, precise=True)
        x1, xn = _mix(xs, y, o.reshape(dec_batch * dec_seq, ATT_WIDTH), w, att_transposed=False)
        xs = _peer(xn, x1, w)
        outs['lat_s'].append(ckv.reshape(dec_batch, dec_seq, KV_LORA))
        outs['kr_s'].append(kr.reshape(dec_batch, dec_seq, D_ROPE))
        outs['ssm_s'].append(_state_out(h_last))

    return (xp.reshape(batch, seq, D_MODEL), xs.reshape(dec_batch, dec_seq, D_MODEL),
            jnp.stack(outs['lat_p']), jnp.stack(outs['kr_p']), jnp.stack(outs['ssm_p']),
            jnp.stack(outs['lat_s']), jnp.stack(outs['kr_s']), jnp.stack(outs['ssm_s']))


def _state_in(st):
    bsz = st.shape[0]
    return jnp.moveaxis(st.astype(F32), -1, 1).reshape(bsz, STATE_WIDTH)


def _state_out(h):
    bsz = h.shape[0]
    return jnp.moveaxis(h.reshape(bsz, 2, SSM_GROUPS, SSM_STATE), 1, -1)
```

```python
import functools
import math

import jax
import jax.numpy as jnp
from jax import lax
from jax.experimental import pallas as pl
from jax.experimental.pallas import tpu as pltpu

F32 = jnp.float32
BF16 = jnp.bfloat16

D_MODEL = 1024
SSM_WIDTH = 512
SSM_GROUP = 16
SSM_GROUPS = 32
SSM_STATE = 64
STATE_HALF = SSM_GROUPS * SSM_STATE
STATE_WIDTH = 2 * STATE_HALF
D_NOPE = 64
D_ROPE = 32
D_V = 64
N_HEADS = 8
ATT_WIDTH = N_HEADS * D_V
Q_LORA = 384
KV_LORA = 256
IN_WIDTH = SSM_WIDTH + Q_LORA + KV_LORA + D_ROPE
ROPE_THETA = 10000.0
ATTN_SCALE = 1.0 / math.sqrt(D_NOPE + D_ROPE)
PAGE_SIZE = 128
PEER_HEADS = 8
PEER_KEYS = 128
PEER_HALF = 128
PEER_TOPK = 16
EPS = 1e-6

LANES = 128
SUBLANES = 8
BF16_ROWS = 2 * SUBLANES
HEAD_BLOCK = LANES
ROPE_LO = D_NOPE
VMEM_LIMIT = 48 * 1024 * 1024

TOK_TILE = 256
KV_CHUNK = 2 * TOK_TILE
Q_TILE = 512
S5_TIME_TILE = 128
S5_ROW_SKEW = 4
PAGES_PER_STEP = 16
PEER_TOK_TILE = 512
PEER_EXP_TILE = SUBLANES * PEER_KEYS
NEG_BIG = -1e30


def _cparams(sem):
    return pltpu.CompilerParams(dimension_semantics=sem, vmem_limit_bytes=VMEM_LIMIT)


def _dot(a, b):
    return jnp.dot(a, b, preferred_element_type=F32)


def _dot_nt(a, b):
    return lax.dot_general(a, b, (((1,), (1,)), ((), ())), preferred_element_type=F32)


def _rms(x, gain):
    return x * lax.rsqrt(jnp.mean(x * x, axis=-1, keepdims=True) + EPS) * gain


def _segment_sumsq(x, seg):
    sq = x * x
    hi = sq.astype(BF16)
    lo = (sq - hi.astype(F32)).astype(BF16)
    return _dot(hi, seg) + _dot(lo, seg)


def _rope_block(x, cos, sin, first_half):
    swapped = jnp.where(first_half, pltpu.roll(x, LANES - D_ROPE // 2, 1), pltpu.roll(x, D_ROPE // 2, 1))
    return x * cos + swapped * sin


def _proj_body(x_ref, cos_ref, sin_ref, gmix_ref, win_ref, gq_ref, wuq_ref, gkv_ref, wuk_ref, wuvt_ref,
               seg_ref, icnt_ref, qgain_ref, kgain_ref, krgain_ref,
               u_ref, q_ref, ckv_ref, kr_ref, kcat_ref, vt_ref):
    xn = _rms(x_ref[...], gmix_ref[...])
    z = _dot(xn.astype(BF16), win_ref[...])
    u_ref[...] = z[:, :SSM_WIDTH]
    cq = z[:, SSM_WIDTH:SSM_WIDTH + Q_LORA]
    ckv = z[:, SSM_WIDTH + Q_LORA:SSM_WIDTH + Q_LORA + KV_LORA]
    krc = z[:, SSM_WIDTH + Q_LORA + KV_LORA:]

    cos = cos_ref[...]
    sin = sin_ref[...]
    lane = lax.broadcasted_iota(jnp.int32, cos.shape, 1)
    first_half = lane < ROPE_LO + D_ROPE // 2
    seg = seg_ref[...]
    icnt = icnt_ref[...]

    ckvn = _rms(ckv, gkv_ref[...])
    ckv_ref[...] = ckvn
    ckvb = ckvn.astype(BF16)

    krn = krc * lax.rsqrt(jnp.sum(krc * krc, axis=-1, keepdims=True) * (1.0 / D_ROPE) + EPS) * krgain_ref[...]
    krr = _rope_block(krn, cos, sin, first_half)
    kr_ref[...] = krr[:, ROPE_LO:ROPE_LO + D_ROPE]

    qp = _dot(_rms(cq, gq_ref[...]).astype(BF16), wuq_ref[...])
    kp = _dot(ckvb, wuk_ref[...])
    qgain = qgain_ref[...]
    kgain = kgain_ref[...]
    q_blocks = []
    k_blocks = []
    for h in range(N_HEADS):
        sl = slice(h * HEAD_BLOCK, (h + 1) * HEAD_BLOCK)
        qh = qp[:, sl]
        qh = qh * lax.rsqrt(_segment_sumsq(qh, seg) * icnt + EPS) * qgain
        q_blocks.append(_rope_block(qh, cos, sin, first_half).astype(BF16))
        kh = kp[:, sl]
        kh = kh * lax.rsqrt(_segment_sumsq(kh, seg) * icnt + EPS) * kgain
        k_blocks.append((kh + krr).astype(BF16))
    q_ref[...] = jnp.concatenate(q_blocks, axis=1)
    kcat_ref[...] = jnp.concatenate(k_blocks, axis=1)

    vt = _dot_nt(wuvt_ref[...], ckvb)
    vt_ref[...] = vt.astype(BF16).reshape(N_HEADS, 1, D_V, vt.shape[1])


def _proj(x, cos, sin, w):
    n = x.shape[0]
    tm = TOK_TILE
    nt = n // tm
    cos_tiles = cos.shape[0] // tm
    full = lambda a: pl.BlockSpec(a.shape, lambda i: (0,) * a.ndim)
    tok = lambda width: pl.BlockSpec((tm, width), lambda i: (i, 0))
    tab = pl.BlockSpec((tm, LANES), lambda i: (i % cos_tiles, 0))
    consts = (w['gmix'], w['win'], w['gq'], w['wuq'], w['gkv'], w['wuk'], w['wuvt'],
              w['seg'], w['icnt'], w['qgain'], w['kgain'], w['krgain'])
    out_shape = (
        jax.ShapeDtypeStruct((n, SSM_WIDTH), F32),
        jax.ShapeDtypeStruct((n, N_HEADS * HEAD_BLOCK), BF16),
        jax.ShapeDtypeStruct((n, KV_LORA), F32),
        jax.ShapeDtypeStruct((n, D_ROPE), F32),
        jax.ShapeDtypeStruct((n, N_HEADS * HEAD_BLOCK), BF16),
        jax.ShapeDtypeStruct((N_HEADS, nt, D_V, tm), BF16),
    )
    out_specs = (tok(SSM_WIDTH), tok(N_HEADS * HEAD_BLOCK), tok(KV_LORA), tok(D_ROPE),
                 tok(N_HEADS * HEAD_BLOCK),
                 pl.BlockSpec((N_HEADS, 1, D_V, tm), lambda i: (0, i, 0, 0)))
    return pl.pallas_call(
        _proj_body,
        grid=(nt,),
        in_specs=[tok(D_MODEL), tab, tab] + [full(a) for a in consts],
        out_specs=out_specs,
        out_shape=out_shape,
        compiler_params=_cparams(("parallel",)),
        name="proj",
    )(x, cos, sin, *consts)


def _prompt_attn_body(q_ref, k_ref, vt_ref, o_ref):
    tq = q_ref.shape[0]
    tk = KV_CHUNK
    per_chunk = tk // vt_ref.shape[3]
    qi = pl.program_id(2)
    q = q_ref[...]

    def step(c, carry, masked):
        m, l, acc = carry
        start = pl.multiple_of(c * tk, tk)
        s = _dot_nt(k_ref[pl.ds(start, tk), :], q)
        if masked:
            kpos = c * tk + lax.broadcasted_iota(jnp.int32, s.shape, 0)
            qpos = qi * tq + lax.broadcasted_iota(jnp.int32, s.shape, 1)
            s = jnp.where(kpos <= qpos, s, -jnp.inf)
        m_new = jnp.maximum(m, jnp.max(s, axis=0, keepdims=True))
        p = jnp.exp(s - m_new)
        alpha = jnp.exp(m - m_new)
        l = alpha * l + jnp.sum(p, axis=0, keepdims=True)
        vt = jnp.concatenate([vt_ref[0, c * per_chunk + r] for r in range(per_chunk)], axis=1)
        acc = alpha * acc + _dot(vt, p.astype(BF16))
        return m_new, l, acc

    init = (jnp.full((1, tq), NEG_BIG, F32), jnp.zeros((1, tq), F32), jnp.zeros((D_V, tq), F32))
    n_full = qi * (tq // tk)
    carry = lax.fori_loop(0, n_full, lambda c, cr: step(c, cr, False), init)
    for d in range(tq // tk):
        carry = step(n_full + d, carry, True)
    m, l, acc = carry
    o_ref[...] = (acc / l).astype(o_ref.dtype)


def _prompt_attention(q, kcat, vt4, batch, seq):
    tq = Q_TILE
    tile = vt4.shape[3]
    nq = seq // tq
    return pl.pallas_call(
        _prompt_attn_body,
        grid=(batch, N_HEADS, nq),
        in_specs=[
            pl.BlockSpec((tq, HEAD_BLOCK), lambda b, h, i: (b * nq + i, h)),
            pl.BlockSpec((seq, HEAD_BLOCK), lambda b, h, i: (b, h)),
            pl.BlockSpec((1, seq // tile, D_V, tile), lambda b, h, i: (h, b, 0, 0)),
        ],
        out_specs=pl.BlockSpec((D_V, tq), lambda b, h, i: (h, b * nq + i)),
        out_shape=jax.ShapeDtypeStruct((ATT_WIDTH, batch * seq), BF16),
        compiler_params=_cparams(("parallel", "parallel", "arbitrary")),
        name="prompt_attn",
    )(q, kcat, vt4)


def _sample_attn_body(pt_ref, q_ref, newlat_ref, newkrt_ref, hmask_ref, wabs_ref, rsel_ref, wukt_ref, wuv_ref,
                      *rest):
    g = PAGES_PER_STEP
    lat_refs = rest[:g]
    krt_refs = rest[g:2 * g]
    o_ref = rest[2 * g]
    qabs_sc, qr_sc, m_sc, l_sc, acc_sc = rest[2 * g + 1:]
    j = pl.program_id(1)
    t_new = q_ref.shape[1]
    rows = t_new * N_HEADS

    @pl.when(j == 0)
    def _():
        q = q_ref[0].astype(F32)
        qb = jnp.broadcast_to(q[:, None, :], (t_new, N_HEADS, q.shape[1])).reshape(rows, q.shape[1])
        qbd = (qb * hmask_ref[...]).astype(BF16)
        qabs_sc[...] = _dot(qbd, wabs_ref[...]).astype(BF16)
        qr_sc[...] = _dot(qbd, rsel_ref[...]).astype(BF16)
        m_sc[...] = jnp.full(m_sc.shape, NEG_BIG, F32)
        l_sc[...] = jnp.zeros(l_sc.shape, F32)
        acc_sc[...] = jnp.zeros(acc_sc.shape, F32)

    def attend(lat32, krt32, causal):
        lat = lat32.astype(BF16)
        tk = lat.shape[0]
        knt = _dot_nt(wukt_ref[...], lat)
        ssq = jnp.sum((knt * knt).reshape(N_HEADS, D_NOPE, tk), axis=1)
        rinv = lax.rsqrt(ssq * (1.0 / D_NOPE) + EPS)
        rinv = jnp.broadcast_to(rinv[None], (t_new, N_HEADS, tk)).reshape(rows, tk)
        s = _dot_nt(qabs_sc[...], lat) * rinv + _dot(qr_sc[...], krt32.astype(BF16))
        if causal:
            q_tok = lax.broadcasted_iota(jnp.int32, (t_new, N_HEADS, tk), 0).reshape(rows, tk)
            k_tok = lax.broadcasted_iota(jnp.int32, (rows, tk), 1)
            s = jnp.where(k_tok <= q_tok, s, -jnp.inf)
        m = m_sc[...]
        m_new = jnp.maximum(m, jnp.max(s, axis=1, keepdims=True))
        p = jnp.exp(s - m_new)
        alpha = jnp.exp(m - m_new)
        l_sc[...] = alpha * l_sc[...] + jnp.sum(p, axis=1, keepdims=True)
        acc_sc[...] = alpha * acc_sc[...] + _dot(p.astype(BF16), lat)
        m_sc[...] = m_new

    attend(jnp.concatenate([r[0, 0] for r in lat_refs], axis=0),
           jnp.concatenate([r[0, 0] for r in krt_refs], axis=1), causal=False)

    @pl.when(j == pl.num_programs(1) - 1)
    def _():
        attend(newlat_ref[0], newkrt_ref[0], causal=True)
        o_lat = acc_sc[...] / l_sc[...]
        full = _dot(o_lat.astype(BF16), wuv_ref[...])
        shape3 = (t_new, N_HEADS, ATT_WIDTH)
        col = lax.broadcasted_iota(jnp.int32, shape3, 2)
        lo = lax.broadcasted_iota(jnp.int32, shape3, 1) * D_V
        sel = jnp.where(col >= lo, jnp.where(col < lo + D_V, full.reshape(shape3), 0.0), 0.0)
        o_ref[0] = jnp.sum(sel, axis=1)


def _sample_attention(q, new_lat, new_krt, cache_lat, cache_krt, page_table, w):
    bd, t_new, _ = q.shape
    n_pages = page_table.shape[1]
    g = PAGES_PER_STEP
    steps = n_pages // g
    rows = t_new * N_HEADS
    per_b = lambda a: pl.BlockSpec((1,) + a.shape[1:], lambda b, j, pt: (b, 0, 0))
    full = lambda a: pl.BlockSpec(a.shape, lambda b, j, pt: (0,) * a.ndim)

    def page_spec(shape, k):
        return pl.BlockSpec((1, 1) + shape, lambda b, j, pt: (0, pt[b, j * g + k], 0, 0))

    consts = (w['hmask'], w['wabs'], w['rsel'], w['wukt'], w['wuv'])
    grid_spec = pltpu.PrefetchScalarGridSpec(
        num_scalar_prefetch=1,
        grid=(bd, steps),
        in_specs=[per_b(q), per_b(new_lat), per_b(new_krt)] + [full(a) for a in consts]
        + [page_spec((PAGE_SIZE, KV_LORA), k) for k in range(g)]
        + [page_spec((D_ROPE, PAGE_SIZE), k) for k in range(g)],
        out_specs=pl.BlockSpec((1, t_new, ATT_WIDTH), lambda b, j, pt: (b, 0, 0)),
        scratch_shapes=[
            pltpu.VMEM((rows, KV_LORA), BF16),
            pltpu.VMEM((rows, D_ROPE), BF16),
            pltpu.VMEM((rows, 1), F32),
            pltpu.VMEM((rows, 1), F32),
            pltpu.VMEM((rows, KV_LORA), F32),
        ],
    )
    return pl.pallas_call(
        _sample_attn_body,
        grid_spec=grid_spec,
        out_shape=jax.ShapeDtypeStruct((bd, t_new, ATT_WIDTH), F32),
        compiler_params=_cparams(("parallel", "arbitrary")),
        name="sample_attn",
    )(page_table, q, new_lat, new_krt, *consts, *([cache_lat] * g), *([cache_krt] * g))


def _s5_body(a_ref, h0_ref, u_ref, bbig_ref, cbig_ref, d_ref, y_ref, hl_ref, st_ref, h_ref,
             *, nseq, steps, pitch, group_pitch):
    groups, rows, _ = u_ref.shape
    slabs = STATE_WIDTH // LANES
    half = slabs // 2

    @pl.when(pl.program_id(0) == 0)
    def _():
        h_ref[...] = h0_ref[...]

    for g in range(groups):
        inc = _dot(u_ref[g].astype(BF16), bbig_ref[...])
        for k in range(slabs):
            st_ref[k, g * group_pitch:g * group_pitch + rows, :] = inc[:, k * LANES:(k + 1) * LANES]

    a = a_ref[...]
    ar = [a[:, k * LANES:(k + 1) * LANES] for k in range(half)]
    ai = [a[:, (half + k) * LANES:(half + k + 1) * LANES] for k in range(half)]

    def body(t, carry):
        out = []
        for k in range(half):
            hr, hi = carry[k], carry[half + k]
            sel = pl.ds(t, nseq, stride=pitch)
            nr = ar[k] * hr - ai[k] * hi + st_ref[k, sel, :]
            ni = ar[k] * hi + ai[k] * hr + st_ref[half + k, sel, :]
            st_ref[k, sel, :] = nr
            st_ref[half + k, sel, :] = ni
            out.append((nr, ni))
        return tuple(o[0] for o in out) + tuple(o[1] for o in out)

    h = h_ref[...]
    final = lax.fori_loop(0, steps, body, tuple(h[:, k * LANES:(k + 1) * LANES] for k in range(slabs)))
    h_new = jnp.concatenate(final, axis=1)
    h_ref[...] = h_new
    hl_ref[...] = h_new

    d = d_ref[...]
    for g in range(groups):
        lo = g * group_pitch
        states = jnp.concatenate([st_ref[k, lo:lo + rows, :].astype(BF16) for k in range(slabs)], axis=1)
        y_ref[g] = _dot(states, cbig_ref[...]) + d * u_ref[g]


def _s5(u3, h0, ssm, nseq, steps, pitch, group_pitch):
    a_bar, bbig, cbig, d_row = ssm
    groups, rows_total, _ = u3.shape
    rows = steps if groups == nseq else rows_total
    n_tiles = rows_total // rows
    scratch_rows = (groups - 1) * group_pitch + rows
    full = lambda arr: pl.BlockSpec(arr.shape, lambda i: (0,) * arr.ndim)
    blk = pl.BlockSpec((groups, rows, SSM_WIDTH), lambda i: (0, i, 0))
    return pl.pallas_call(
        functools.partial(_s5_body, nseq=nseq, steps=steps, pitch=pitch, group_pitch=group_pitch),
        grid=(n_tiles,),
        in_specs=[full(a_bar), full(h0), blk, full(bbig), full(cbig), full(d_row)],
        out_specs=(blk, full(h0)),
        out_shape=(jax.ShapeDtypeStruct(u3.shape, F32), jax.ShapeDtypeStruct(h0.shape, F32)),
        scratch_shapes=[pltpu.VMEM((STATE_WIDTH // LANES, scratch_rows, LANES), F32),
                        pltpu.VMEM(h0.shape, F32)],
        compiler_params=_cparams(("arbitrary",)),
        name="s5",
    )(a_bar, h0, u3, bbig, cbig, d_row)


def _s5_weights(a_re, a_im, log_dt, b, c, d_skip):
    g = SSM_GROUPS
    a_re, a_im = a_re.astype(F32), a_im.astype(F32)
    dt = jnp.exp(log_dt.astype(F32))[:, None]
    lr, li = a_re * dt, a_im * dt
    ar, ai = jnp.exp(lr) * jnp.cos(li), jnp.exp(lr) * jnp.sin(li)
    den = a_re * a_re + a_im * a_im
    fr = ((ar - 1.0) * a_re + ai * a_im) / den
    fi = (ai * a_re - (ar - 1.0) * a_im) / den
    bre, bim = b[..., 0].astype(F32), b[..., 1].astype(F32)
    bbr = fr[..., None] * bre - fi[..., None] * bim
    bbi = fr[..., None] * bim + fi[..., None] * bre
    cre, cim = c[..., 0].astype(F32), c[..., 1].astype(F32)
    eye = jnp.eye(g, dtype=F32)
    bbig = jnp.einsum('rgph,gq->ghrqp', jnp.stack([bbr, bbi]), eye).reshape(SSM_WIDTH, STATE_WIDTH)
    cbig = jnp.einsum('rgop,gq->rqpgo', jnp.stack([cre, -cim]), eye).reshape(STATE_WIDTH, SSM_WIDTH)
    a_bar = jnp.concatenate([ar.reshape(1, -1), ai.reshape(1, -1)], axis=1)
    return a_bar, bbig.astype(BF16), cbig.astype(BF16), d_skip.astype(F32).reshape(1, -1)


def _mix_body(x_ref, y_ref, o_ref, wglu_ref, wout_ref, gffn_ref, x1_ref, xn_ref, *, att_transposed):
    gl = _dot(jax.nn.gelu(y_ref[...]).astype(BF16), wglu_ref[...])
    glu = gl[:, :SSM_WIDTH] * jax.nn.sigmoid(gl[:, SSM_WIDTH:])
    o = o_ref[...]
    if att_transposed:
        o = o.astype(F32).T
    x1 = x_ref[...] + _dot(glu.astype(BF16), wout_ref[:SSM_WIDTH, :]) + _dot(o.astype(BF16), wout_ref[SSM_WIDTH:, :])
    x1_ref[...] = x1
    xn_ref[...] = _rms(x1, gffn_ref[...]).astype(BF16)


def _mix(x, y, o, w, att_transposed):
    n = x.shape[0]
    tm = TOK_TILE
    tok = lambda width: pl.BlockSpec((tm, width), lambda i: (i, 0))
    full = lambda a: pl.BlockSpec(a.shape, lambda i: (0,) * a.ndim)
    o_spec = pl.BlockSpec((ATT_WIDTH, tm), lambda i: (0, i)) if att_transposed else tok(ATT_WIDTH)
    consts = (w['wglu'], w['wout'], w['gffn'])
    return pl.pallas_call(
        functools.partial(_mix_body, att_transposed=att_transposed),
        grid=(n // tm,),
        in_specs=[tok(D_MODEL), tok(SSM_WIDTH), o_spec] + [full(a) for a in consts],
        out_specs=(tok(D_MODEL), tok(D_MODEL)),
        out_shape=(jax.ShapeDtypeStruct((n, D_MODEL), F32), jax.ShapeDtypeStruct((n, D_MODEL), BF16)),
        compiler_params=_cparams(("parallel",)),
        name="mix",
    )(x, y, o, *consts)


def _desc_top(vals_fn, count):
    tops = []
    prev = None
    for _ in range(count):
        cur = vals_fn(prev)
        tops.append(cur)
        prev = cur
    return tops


def _peer_route_body(xn_ref, wq_ref, keys_ref, rank_ref, e2_ref, lim_ref, cc_ref):
    xn = xn_ref[...]
    q = _dot(xn, wq_ref[...]).astype(BF16)
    k1 = keys_ref[0]
    k2 = keys_ref[1]
    tops1, tops2, s1_all, s2_all = [], [], [], []
    for h in range(PEER_HEADS):
        base = h * 2 * PEER_HALF
        s1 = _dot_nt(k1, q[:, base:base + PEER_HALF])
        s2 = _dot_nt(k2, q[:, base + PEER_HALF:base + 2 * PEER_HALF])
        s1_all.append(s1)
        s2_all.append(s2)
        for s, tops in ((s1, tops1), (s2, tops2)):
            def nxt(prev, s=s):
                cand = s if prev is None else jnp.where(s < prev, s, -jnp.inf)
                return jnp.max(cand, axis=0, keepdims=True)
            tops.append(_desc_top(nxt, PEER_TOPK))
    a = [jnp.concatenate([tops1[h][r] for h in range(PEER_HEADS)], axis=0) for r in range(PEER_TOPK)]
    b = [jnp.concatenate([tops2[h][r] for h in range(PEER_HEADS)], axis=0) for r in range(PEER_TOPK)]
    cands = [a[i] + b[j] for i in range(PEER_TOPK) for j in range(PEER_TOPK) if (i + 1) * (j + 1) <= PEER_TOPK]

    def nxt_sum(prev):
        best = None
        for cnd in cands:
            v = cnd if prev is None else jnp.where(cnd < prev, cnd, -jnp.inf)
            best = v if best is None else jnp.maximum(best, v)
        return best

    tsum = _desc_top(nxt_sum, PEER_TOPK)
    tau = tsum[-1]
    zsum = jnp.zeros_like(tau)
    for t in tsum:
        zsum = zsum + jnp.exp(t - tsum[0])
    inv_z = 1.0 / zsum
    count = []
    for i in range(PEER_TOPK):
        cnt = jnp.zeros_like(tau)
        for j in range(PEER_TOPK):
            if (i + 1) * (j + 1) <= PEER_TOPK:
                cnt = cnt + jnp.where(a[i] + b[j] >= tau, 1.0, 0.0)
        count.append(cnt)
    for h in range(PEER_HEADS):
        row = slice(h, h + 1)
        lim = jnp.zeros(s1_all[h].shape, F32)
        rank = jnp.full(s2_all[h].shape, float(PEER_KEYS), F32)
        for r in range(PEER_TOPK):
            lim = jnp.where(s1_all[h] == a[r][row], count[r][row], lim)
            rank = jnp.where(s2_all[h] == b[r][row], float(r), rank)
        lim_ref[h] = lim
        rank_ref[h] = rank.astype(BF16)
        cc_ref[h] = jnp.exp(s1_all[h] - a[0][row]) * inv_z[row]
        e2_ref[h] = jnp.exp(s2_all[h] - b[0][row]).astype(BF16)


def _peer_route(xn, w):
    n = xn.shape[0]
    tm = TOK_TILE
    spec = pl.BlockSpec((PEER_HEADS, PEER_KEYS, tm), lambda i: (0, 0, i))
    shp = lambda dt: jax.ShapeDtypeStruct((PEER_HEADS, PEER_KEYS, n), dt)
    return pl.pallas_call(
        _peer_route_body,
        grid=(n // tm,),
        in_specs=[pl.BlockSpec((tm, D_MODEL), lambda i: (i, 0)),
                  pl.BlockSpec(w['peer_wq'].shape, lambda i: (0, 0)),
                  pl.BlockSpec(w['peer_keys'].shape, lambda i: (0, 0, 0))],
        out_specs=(spec, spec, spec, spec),
        out_shape=(shp(BF16), shp(BF16), shp(F32), shp(F32)),
        compiler_params=_cparams(("parallel",)),
        name="peer_route",
    )(xn, w['peer_wq'], w['peer_keys'])


def _peer_expert_body(xn_ref, x1_ref, u_ref, vt_ref, rank_ref, e2_ref, lim_ref, cc_ref, o_ref, acc_ref, ga_ref):
    j = pl.program_id(1)
    te = u_ref.shape[0]
    tm = xn_ref.shape[0]
    nsub = te // PEER_KEYS
    tiles = PEER_KEYS // BF16_ROWS

    @pl.when(j == 0)
    def _():
        acc_ref[...] = jnp.zeros(acc_ref.shape, F32)

    act = jax.nn.gelu(_dot_nt(u_ref[...], xn_ref[...])).astype(BF16)
    zero = jnp.zeros((), BF16)
    first = pl.multiple_of(j * nsub, SUBLANES)
    for c in range(tm // LANES):
        ls = slice(c * LANES, (c + 1) * LANES)
        for sub in range(nsub):
            gate = None
            for h in range(PEER_HEADS):
                lim = lim_ref[h, pl.ds(first, nsub), ls][sub:sub + 1]
                cc = cc_ref[h, pl.ds(first, nsub), ls][sub:sub + 1]
                lim = jnp.broadcast_to(lim, (BF16_ROWS, LANES)).astype(BF16)
                cc = jnp.broadcast_to(cc, (BF16_ROWS, LANES)).astype(BF16)
                rank = rank_ref[h, :, ls].reshape(tiles, BF16_ROWS, LANES)
                e2 = e2_ref[h, :, ls].reshape(tiles, BF16_ROWS, LANES)
                wgt = jnp.where(rank < lim[None], e2 * cc[None], zero)
                gate = wgt if gate is None else gate + wgt
            rs = slice(sub * PEER_KEYS, (sub + 1) * PEER_KEYS)
            ga_ref[rs, ls] = gate.reshape(PEER_KEYS, LANES) * act[rs, ls]
    acc_ref[...] += _dot(vt_ref[...], ga_ref[...])

    @pl.when(j == pl.num_programs(1) - 1)
    def _():
        o_ref[...] = x1_ref[...] + acc_ref[...].T


def _peer_experts(xn, x1, routes, w):
    n = xn.shape[0]
    tm = PEER_TOK_TILE
    te = PEER_EXP_TILE
    n_exp = w['peer_u'].shape[0]
    tok = pl.BlockSpec((tm, D_MODEL), lambda i, j: (i, 0))
    route = pl.BlockSpec((PEER_HEADS, PEER_KEYS, tm), lambda i, j: (0, 0, i))
    return pl.pallas_call(
        _peer_expert_body,
        grid=(n // tm, n_exp // te),
        in_specs=[tok, tok,
                  pl.BlockSpec((te, D_MODEL), lambda i, j: (j, 0)),
                  pl.BlockSpec((D_MODEL, te), lambda i, j: (0, j)),
                  route, route, route, route],
        out_specs=tok,
        out_shape=jax.ShapeDtypeStruct((n, D_MODEL), F32),
        scratch_shapes=[pltpu.VMEM((D_MODEL, tm), F32), pltpu.VMEM((te, tm), BF16)],
        compiler_params=_cparams(("parallel", "arbitrary")),
        name="peer_experts",
    )(xn, x1, w['peer_u'], w['peer_vt'], *routes)


def _rope_tables(pos):
    inv = ROPE_THETA ** (-jnp.arange(0, D_ROPE, 2, dtype=F32) / D_ROPE)
    ang = pos.astype(F32)[:, None] * inv[None, :]
    cos, sin = jnp.cos(ang), jnp.sin(ang)
    t = pos.shape[0]
    ones = jnp.ones((t, ROPE_LO), F32)
    tail = HEAD_BLOCK - ROPE_LO - D_ROPE
    cos_t = jnp.concatenate([ones, cos, cos, jnp.ones((t, tail), F32)], axis=1)
    sin_t = jnp.concatenate([0 * ones, -sin, sin, jnp.zeros((t, tail), F32)], axis=1)
    return cos_t, sin_t


def _head_blocks(wmat, lo, width):
    pad = jnp.zeros(wmat.shape[:2] + (HEAD_BLOCK,), wmat.dtype)
    return pad.at[:, :, lo:lo + width].set(wmat).reshape(wmat.shape[0], N_HEADS * HEAD_BLOCK)


def _prep_weights(norm_mix, w_in, norm_q_lora, w_uq, norm_kv_lora, w_uk, w_uv, g_qn, g_qr, g_kn, g_kr,
                  w_glu, w_out, norm_ffn, peer_wq, peer_keys, peer_u, peer_v):
    row = lambda v: v.astype(F32).reshape(1, -1)
    kr_cols = jnp.zeros((D_MODEL, LANES), F32).at[:, ROPE_LO:ROPE_LO + D_ROPE].set(w_in[:, IN_WIDTH - D_ROPE:])
    win = jnp.concatenate([w_in[:, :IN_WIDTH - D_ROPE], kr_cols], axis=1)
    wuq = _head_blocks(w_uq.reshape(Q_LORA, N_HEADS, D_NOPE + D_ROPE), 0, D_NOPE + D_ROPE)
    wuk = _head_blocks(w_uk, 0, D_NOPE)
    wuvt = w_uv.reshape(KV_LORA, ATT_WIDTH).T
    lane = jnp.arange(HEAD_BLOCK)
    is_nope = lane < ROPE_LO
    is_rope = (lane >= ROPE_LO) & (lane < ROPE_LO + D_ROPE)
    seg = ((is_nope[:, None] & is_nope[None, :]) | (is_rope[:, None] & is_rope[None, :])).astype(BF16)
    icnt = jnp.where(is_nope, 1.0 / D_NOPE, jnp.where(is_rope, 1.0 / D_ROPE, 1.0)).astype(F32).reshape(1, -1)
    blockvec = lambda a, bvec: jnp.zeros((HEAD_BLOCK,), F32).at[:ROPE_LO].set(a).at[ROPE_LO:ROPE_LO + D_ROPE].set(bvec)
    qgain = (blockvec(g_qn, g_qr) * ATTN_SCALE).reshape(1, -1)
    kgain = blockvec(g_kn, jnp.zeros((D_ROPE,), F32)).reshape(1, -1)
    krgain = blockvec(jnp.zeros((D_NOPE,), F32), g_kr).reshape(1, -1)
    wabs = jnp.einsum('chd,d->hdc', w_uk, g_kn)
    wabs = jnp.zeros((N_HEADS, HEAD_BLOCK, KV_LORA), F32).at[:, :D_NOPE, :].set(wabs)
    wabs = wabs.reshape(N_HEADS * HEAD_BLOCK, KV_LORA)
    rsel = jnp.zeros((N_HEADS, HEAD_BLOCK, D_ROPE), F32).at[:, ROPE_LO:ROPE_LO + D_ROPE, :].set(
        jnp.broadcast_to(jnp.eye(D_ROPE, dtype=F32), (N_HEADS, D_ROPE, D_ROPE)))
    rsel = rsel.reshape(N_HEADS * HEAD_BLOCK, D_ROPE)
    head_of_lane = jnp.arange(N_HEADS * HEAD_BLOCK) // HEAD_BLOCK
    hmask8 = (head_of_lane[None, :] == jnp.arange(N_HEADS)[:, None]).astype(F32)
    return {
        'gmix': row(norm_mix), 'gq': row(norm_q_lora), 'gkv': row(norm_kv_lora),
        'seg': seg, 'icnt': icnt, 'qgain': qgain, 'kgain': kgain, 'krgain': krgain,
        'win': win.astype(BF16), 'wuq': wuq.astype(BF16), 'wuk': wuk.astype(BF16), 'wuvt': wuvt.astype(BF16),
        'hmask8': hmask8, 'wabs': wabs.astype(BF16), 'rsel': rsel.astype(BF16),
        'wukt': w_uk.reshape(KV_LORA, N_HEADS * D_NOPE).T.astype(BF16),
        'wuv': w_uv.reshape(KV_LORA, ATT_WIDTH).astype(BF16),
        'wglu': w_glu.astype(BF16), 'wout': w_out.astype(BF16), 'gffn': row(norm_ffn),
        'peer_wq': peer_wq.astype(BF16), 'peer_keys': peer_keys.astype(BF16),
        'peer_u': peer_u.astype(BF16), 'peer_vt': peer_v.astype(BF16).T,
    }


def _peer(xn, x1, w):
    return _peer_experts(xn, x1, _peer_route(xn, w), w)


def _state_in(st):
    return jnp.moveaxis(st.astype(F32), -1, 1).reshape(st.shape[0], STATE_WIDTH)


def _state_out(h):
    return jnp.moveaxis(h.reshape(h.shape[0], 2, SSM_GROUPS, SSM_STATE), 1, -1)


def kernel(x_prompt, x_sample, cache_kv_latent, cache_k_rope, state_ssm, page_table, norm_mix, w_in, norm_q_lora, w_uq, norm_kv_lora, w_uk, w_uv, qk_gain_q_nope, qk_gain_q_rope, qk_gain_k_nope, qk_gain_k_rope, ssm_a_re, ssm_a_im, ssm_log_dt, ssm_b, ssm_c, ssm_d, w_glu, w_out, norm_ffn, peer_wq, peer_keys, peer_u, peer_v):
    depth = norm_mix.shape[0]
    batch, seq, _ = x_prompt.shape
    dec_batch, dec_seq, _ = x_sample.shape
    n_dec = dec_batch * dec_seq
    past = page_table.shape[1] * PAGE_SIZE
    assert seq % Q_TILE == 0 and seq % S5_TIME_TILE == 0 and n_dec % PEER_TOK_TILE == 0
    assert page_table.shape[1] % PAGES_PER_STEP == 0 and TOK_TILE % dec_seq == 0 and dec_seq <= PAGE_SIZE

    cos_p, sin_p = _rope_tables(jnp.arange(seq))
    cos_s, sin_s = _rope_tables(past + jnp.arange(dec_seq))
    cos_s, sin_s = jnp.tile(cos_s, (TOK_TILE // dec_seq, 1)), jnp.tile(sin_s, (TOK_TILE // dec_seq, 1))
    cache_krt = jnp.swapaxes(cache_k_rope, 2, 3)

    xp = x_prompt.reshape(batch * seq, D_MODEL)
    xs = x_sample.reshape(n_dec, D_MODEL)
    outs = {k: [] for k in ('lat_p', 'kr_p', 'ssm_p', 'lat_s', 'kr_s', 'ssm_s')}
    for l in range(depth):
        w = _prep_weights(norm_mix[l], w_in[l], norm_q_lora[l], w_uq[l], norm_kv_lora[l], w_uk[l], w_uv[l],
                          qk_gain_q_nope[l], qk_gain_q_rope[l], qk_gain_k_nope[l], qk_gain_k_rope[l],
                          w_glu[l], w_out[l], norm_ffn[l], peer_wq[l], peer_keys[l], peer_u[l], peer_v[l])
        w['hmask'] = jnp.tile(w['hmask8'], (dec_seq, 1))
        ssm = _s5_weights(ssm_a_re[l], ssm_a_im[l], ssm_log_dt[l], ssm_b[l], ssm_c[l], ssm_d[l])

        u, q, ckv, kr, kcat, vt4 = _proj(xp, cos_p, sin_p, w)
        o_t = _prompt_attention(q, kcat, vt4, batch, seq)
        pitch = S5_TIME_TILE + S5_ROW_SKEW
        y, h_last = _s5(u.reshape(batch, seq, SSM_WIDTH), jnp.zeros((batch, STATE_WIDTH), F32), ssm,
                        nseq=batch, steps=S5_TIME_TILE, pitch=pitch, group_pitch=pitch)
        x1, xn = _mix(xp, y.reshape(batch * seq, SSM_WIDTH), o_t, w, att_transposed=True)
        xp = _peer(xn, x1, w)
        outs['lat_p'].append(ckv.reshape(batch, seq, KV_LORA))
        outs['kr_p'].append(kr.reshape(batch, seq, D_ROPE))
        outs['ssm_p'].append(_state_out(h_last))

        u, q, ckv, kr, _, _ = _proj(xs, cos_s, sin_s, w)
        pad_rows = ((0, 0), (0, PAGE_SIZE - dec_seq), (0, 0))
        new_lat = jnp.pad(ckv.reshape(dec_batch, dec_seq, KV_LORA), pad_rows)
        new_krt = jnp.swapaxes(jnp.pad(kr.reshape(dec_batch, dec_seq, D_ROPE), pad_rows), 1, 2)
        o = _sample_attention(q.reshape(dec_batch, dec_seq, -1), new_lat, new_krt,
                              cache_kv_latent[l:l + 1], cache_krt[l:l + 1], page_table, w)
        y, h_last = _s5(u.reshape(1, n_dec, SSM_WIDTH), _state_in(state_ssm[l]), ssm,
                        nseq=dec_batch, steps=dec_seq, pitch=dec_seq, group_pitch=n_dec)
        x1, xn = _mix(xs, y.reshape(n_dec, SSM_WIDTH), o.reshape(n_dec, ATT_WIDTH), w, att_transposed=False)
        xs = _peer(xn, x1, w)
        outs['lat_s'].append(ckv.reshape(dec_batch, dec_seq, KV_LORA))
        outs['kr_s'].append(kr.reshape(dec_batch, dec_seq, D_ROPE))
        outs['ssm_s'].append(_state_out(h_last))

    return (xp.reshape(batch, seq, D_MODEL), xs.reshape(dec_batch, dec_seq, D_MODEL),
            jnp.stack(outs['lat_p']), jnp.stack(outs['kr_p']), jnp.stack(outs['ssm_p']),
            jnp.stack(outs['lat_s']), jnp.stack(outs['kr_s']), jnp.stack(outs['ssm_s']))
```

```python
import functools
import math

import jax
import jax.numpy as jnp
from jax import lax
from jax.experimental import pallas as pl
from jax.experimental.pallas import tpu as pltpu

F32 = jnp.float32
BF16 = jnp.bfloat16
U32 = jnp.uint32

D_MODEL = 1024
SSM_WIDTH = 512
SSM_GROUP = 16
SSM_GROUPS = 32
SSM_STATE = 64
STATE_HALF = SSM_GROUPS * SSM_STATE
STATE_WIDTH = 2 * STATE_HALF
D_NOPE = 64
D_ROPE = 32
D_V = 64
N_HEADS = 8
ATT_WIDTH = N_HEADS * D_V
Q_LORA = 384
KV_LORA = 256
IN_WIDTH = SSM_WIDTH + Q_LORA + KV_LORA + D_ROPE
ROPE_THETA = 10000.0
ATTN_SCALE = 1.0 / math.sqrt(D_NOPE + D_ROPE)
PAGE_SIZE = 128
PEER_HEADS = 8
PEER_KEYS = 128
PEER_HALF = 128
PEER_TOPK = 16
EPS = 1e-6

LANES = 128
SUBLANES = 8
BF16_ROWS = 2 * SUBLANES
MXU_DEPTH = 256
HEAD_BLOCK = LANES
ROPE_LO = D_NOPE
VMEM_LIMIT = 48 * 1024 * 1024

TOK_TILE = 256
KV_CHUNK = 2 * TOK_TILE
Q_TILE = 512
HEADS_PER_STEP = 2
S5_TIME_TILE = 128
S5_ROW_SKEW = 4
PAGES_PER_STEP = 16
PEER_TOK_TILE = 512
PEER_EXP_TILE = SUBLANES * PEER_KEYS
NEG_BIG = -1e30


def _cparams(sem):
    return pltpu.CompilerParams(dimension_semantics=sem, vmem_limit_bytes=VMEM_LIMIT)


def _dot(a, b):
    return jnp.dot(a, b, preferred_element_type=F32)


def _dot_nt(a, b):
    return lax.dot_general(a, b, (((1,), (1,)), ((), ())), preferred_element_type=F32)


def _rms(x, gain):
    return x * lax.rsqrt(jnp.mean(x * x, axis=-1, keepdims=True) + EPS) * gain


def _segment_sumsq(x, seg):
    sq = x * x
    hi = sq.astype(BF16)
    lo = (sq - hi.astype(F32)).astype(BF16)
    return _dot(hi, seg) + _dot(lo, seg)


def _rope_block(x, cos, sin, first_half):
    swapped = jnp.where(first_half, pltpu.roll(x, LANES - D_ROPE // 2, 1), pltpu.roll(x, D_ROPE // 2, 1))
    return x * cos + swapped * sin


def _proj_body(x_ref, cos_ref, sin_ref, gmix_ref, win_ref, gq_ref, wuq_ref, gkv_ref, wuk_ref, wuvt_ref,
               seg_ref, icnt_ref, qgain_ref, kgain_ref, krgain_ref,
               u_ref, q_ref, ckv_ref, kr_ref, kcat_ref, vt_ref):
    xn = _rms(x_ref[...], gmix_ref[...])
    z = _dot(xn.astype(BF16), win_ref[...])
    u_ref[...] = z[:, :SSM_WIDTH]
    cq = z[:, SSM_WIDTH:SSM_WIDTH + Q_LORA]
    ckv = z[:, SSM_WIDTH + Q_LORA:SSM_WIDTH + Q_LORA + KV_LORA]
    krc = z[:, SSM_WIDTH + Q_LORA + KV_LORA:]

    cos = cos_ref[...]
    sin = sin_ref[...]
    lane = lax.broadcasted_iota(jnp.int32, cos.shape, 1)
    first_half = lane < ROPE_LO + D_ROPE // 2
    seg = seg_ref[...]
    icnt = icnt_ref[...]

    ckvn = _rms(ckv, gkv_ref[...])
    ckv_ref[...] = ckvn
    ckvb = ckvn.astype(BF16)

    krn = krc * lax.rsqrt(jnp.sum(krc * krc, axis=-1, keepdims=True) * (1.0 / D_ROPE) + EPS) * krgain_ref[...]
    krr = _rope_block(krn, cos, sin, first_half)
    kr_ref[...] = krr[:, ROPE_LO:ROPE_LO + D_ROPE]

    qp = _dot(_rms(cq, gq_ref[...]).astype(BF16), wuq_ref[...])
    kp = _dot(ckvb, wuk_ref[...])
    qgain = qgain_ref[...]
    kgain = kgain_ref[...]
    q_blocks = []
    k_blocks = []
    for h in range(N_HEADS):
        sl = slice(h * HEAD_BLOCK, (h + 1) * HEAD_BLOCK)
        qh = qp[:, sl]
        qh = qh * lax.rsqrt(_segment_sumsq(qh, seg) * icnt + EPS) * qgain
        q_blocks.append(_rope_block(qh, cos, sin, first_half).astype(BF16))
        kh = kp[:, sl]
        kh = kh * lax.rsqrt(_segment_sumsq(kh, seg) * icnt + EPS) * kgain
        k_blocks.append((kh + krr).astype(BF16))
    q_ref[...] = jnp.concatenate(q_blocks, axis=1)
    kcat_ref[...] = jnp.concatenate(k_blocks, axis=1)

    vt = _dot_nt(wuvt_ref[...], ckvb)
    vt_ref[...] = vt.astype(BF16).reshape(N_HEADS, 1, D_V, vt.shape[1])


def _proj(x, cos, sin, w):
    n = x.shape[0]
    tm = TOK_TILE
    nt = n // tm
    cos_tiles = cos.shape[0] // tm
    full = lambda a: pl.BlockSpec(a.shape, lambda i: (0,) * a.ndim)
    tok = lambda width: pl.BlockSpec((tm, width), lambda i: (i, 0))
    tab = pl.BlockSpec((tm, LANES), lambda i: (i % cos_tiles, 0))
    consts = (w['gmix'], w['win'], w['gq'], w['wuq'], w['gkv'], w['wuk'], w['wuvt'],
              w['seg'], w['icnt'], w['qgain'], w['kgain'], w['krgain'])
    out_shape = (
        jax.ShapeDtypeStruct((n, SSM_WIDTH), F32),
        jax.ShapeDtypeStruct((n, N_HEADS * HEAD_BLOCK), BF16),
        jax.ShapeDtypeStruct((n, KV_LORA), F32),
        jax.ShapeDtypeStruct((n, D_ROPE), F32),
        jax.ShapeDtypeStruct((n, N_HEADS * HEAD_BLOCK), BF16),
        jax.ShapeDtypeStruct((N_HEADS, nt, D_V, tm), BF16),
    )
    out_specs = (tok(SSM_WIDTH), tok(N_HEADS * HEAD_BLOCK), tok(KV_LORA), tok(D_ROPE),
                 tok(N_HEADS * HEAD_BLOCK),
                 pl.BlockSpec((N_HEADS, 1, D_V, tm), lambda i: (0, i, 0, 0)))
    return pl.pallas_call(
        _proj_body,
        grid=(nt,),
        in_specs=[tok(D_MODEL), tab, tab] + [full(a) for a in consts],
        out_specs=out_specs,
        out_shape=out_shape,
        compiler_params=_cparams(("parallel",)),
        name="proj",
    )(x, cos, sin, *consts)


def _prompt_attn_body(q_ref, k_ref, vt_ref, o_ref, s_ref):
    tq = q_ref.shape[0]
    tk = KV_CHUNK
    per_chunk = tk // vt_ref.shape[3]
    heads = vt_ref.shape[0]
    qi = pl.program_id(2)

    def produce(c, hh):
        lanes = slice(hh * HEAD_BLOCK, (hh + 1) * HEAD_BLOCK)
        start = pl.multiple_of(c * tk, tk)
        s = _dot_nt(k_ref[pl.ds(start, tk), lanes], q_ref[:, lanes])
        s_ref[hh * 2 + c % 2] = s
        return jnp.max(s, axis=0, keepdims=True)

    def consume(c, hh, cmax, state, diagonal):
        m, l, acc = state
        s = s_ref[hh * 2 + c % 2]
        if diagonal:
            kpos = c * tk + lax.broadcasted_iota(jnp.int32, s.shape, 0)
            qpos = qi * tq + lax.broadcasted_iota(jnp.int32, s.shape, 1)
            s = jnp.where(kpos <= qpos, s, -jnp.inf)
            cmax = jnp.max(s, axis=0, keepdims=True)
        m_new = jnp.maximum(m, cmax)
        p = jnp.exp(s - m_new)
        alpha = jnp.exp(m - m_new)
        l = alpha * l + jnp.sum(p, axis=0, keepdims=True)
        vt = jnp.concatenate([vt_ref[hh, c * per_chunk + r] for r in range(per_chunk)], axis=1)
        acc = alpha * acc + _dot(vt, p.astype(BF16))
        return m_new, l, acc

    def body(c, carry):
        nxt = tuple(produce(c + 1, hh) for hh in range(heads))
        states = tuple(consume(c, hh, carry[0][hh], carry[1][hh], False) for hh in range(heads))
        return nxt, states

    states = tuple((jnp.full((1, tq), NEG_BIG, F32), jnp.zeros((1, tq), F32), jnp.zeros((D_V, tq), F32))
                   for _ in range(heads))
    first = tuple(produce(0, hh) for hh in range(heads))
    cmax, states = lax.fori_loop(0, qi, body, (first, states))
    for hh in range(heads):
        m, l, acc = consume(qi, hh, cmax[hh], states[hh], True)
        o_ref[hh * D_V:(hh + 1) * D_V, :] = (acc / l).astype(o_ref.dtype)


def _prompt_attention(q, kcat, vt4, batch, seq):
    tq = Q_TILE
    tile = vt4.shape[3]
    nq = seq // tq
    hs = HEADS_PER_STEP
    return pl.pallas_call(
        _prompt_attn_body,
        grid=(batch, N_HEADS // hs, nq),
        in_specs=[
            pl.BlockSpec((tq, hs * HEAD_BLOCK), lambda b, h, i: (b * nq + i, h)),
            pl.BlockSpec((seq, hs * HEAD_BLOCK), lambda b, h, i: (b, h)),
            pl.BlockSpec((hs, seq // tile, D_V, tile), lambda b, h, i: (h, b, 0, 0)),
        ],
        out_specs=pl.BlockSpec((hs * D_V, tq), lambda b, h, i: (h, b * nq + i)),
        out_shape=jax.ShapeDtypeStruct((ATT_WIDTH, batch * seq), BF16),
        scratch_shapes=[pltpu.VMEM((2 * hs, KV_CHUNK, tq), F32)],
        compiler_params=_cparams(("parallel", "parallel", "arbitrary")),
        name="prompt_attn",
    )(q, kcat, vt4)


def _sample_attn_body(pt_ref, q_ref, newlat_ref, newkrt_ref, hmask_ref, wabs_ref, rsel_ref, wukt_ref, wuv_ref,
                      *rest):
    g = PAGES_PER_STEP
    lat_refs = rest[:g]
    krt_refs = rest[g:2 * g]
    o_ref = rest[2 * g]
    qabs_sc, qr_sc, m_sc, l_sc, acc_sc = rest[2 * g + 1:]
    j = pl.program_id(1)
    t_new = q_ref.shape[1]
    rows = t_new * N_HEADS

    @pl.when(j == 0)
    def _():
        q = q_ref[0].astype(F32)
        qb = jnp.broadcast_to(q[:, None, :], (t_new, N_HEADS, q.shape[1])).reshape(rows, q.shape[1])
        qbd = (qb * hmask_ref[...]).astype(BF16)
        qabs_sc[...] = _dot(qbd, wabs_ref[...]).astype(BF16)
        qr_sc[...] = _dot(qbd, rsel_ref[...]).astype(BF16)
        m_sc[...] = jnp.full(m_sc.shape, NEG_BIG, F32)
        l_sc[...] = jnp.zeros(l_sc.shape, F32)
        acc_sc[...] = jnp.zeros(acc_sc.shape, F32)

    def attend(lat32, krt32, causal):
        lat = lat32.astype(BF16)
        tk = lat.shape[0]
        knt = _dot_nt(wukt_ref[...], lat)
        ssq = jnp.sum((knt * knt).reshape(N_HEADS, D_NOPE, tk), axis=1)
        rinv = lax.rsqrt(ssq * (1.0 / D_NOPE) + EPS)
        rinv = jnp.broadcast_to(rinv[None], (t_new, N_HEADS, tk)).reshape(rows, tk)
        s = _dot_nt(qabs_sc[...], lat) * rinv + _dot(qr_sc[...], krt32.astype(BF16))
        if causal:
            q_tok = lax.broadcasted_iota(jnp.int32, (t_new, N_HEADS, tk), 0).reshape(rows, tk)
            k_tok = lax.broadcasted_iota(jnp.int32, (rows, tk), 1)
            s = jnp.where(k_tok <= q_tok, s, -jnp.inf)
        m = m_sc[...]
        m_new = jnp.maximum(m, jnp.max(s, axis=1, keepdims=True))
        p = jnp.exp(s - m_new)
        alpha = jnp.exp(m - m_new)
        l_sc[...] = alpha * l_sc[...] + jnp.sum(p, axis=1, keepdims=True)
        acc_sc[...] = alpha * acc_sc[...] + _dot(p.astype(BF16), lat)
        m_sc[...] = m_new

    attend(jnp.concatenate([r[0, 0] for r in lat_refs], axis=0),
           jnp.concatenate([r[0, 0] for r in krt_refs], axis=1), causal=False)

    @pl.when(j == pl.num_programs(1) - 1)
    def _():
        attend(newlat_ref[0], newkrt_ref[0], causal=True)
        o_lat = acc_sc[...] / l_sc[...]
        full = _dot(o_lat.astype(BF16), wuv_ref[...])
        shape3 = (t_new, N_HEADS, ATT_WIDTH)
        col = lax.broadcasted_iota(jnp.int32, shape3, 2)
        lo = lax.broadcasted_iota(jnp.int32, shape3, 1) * D_V
        sel = jnp.where(col >= lo, jnp.where(col < lo + D_V, full.reshape(shape3), 0.0), 0.0)
        o_ref[0] = jnp.sum(sel, axis=1)


def _sample_attention(q, new_lat, new_krt, cache_lat, cache_krt, page_table, w):
    bd, t_new, _ = q.shape
    n_pages = page_table.shape[1]
    g = PAGES_PER_STEP
    steps = n_pages // g
    rows = t_new * N_HEADS
    per_b = lambda a: pl.BlockSpec((1,) + a.shape[1:], lambda b, j, pt: (b, 0, 0))
    full = lambda a: pl.BlockSpec(a.shape, lambda b, j, pt: (0,) * a.ndim)

    def page_spec(shape, k):
        return pl.BlockSpec((1, 1) + shape, lambda b, j, pt: (0, pt[b, j * g + k], 0, 0))

    consts = (w['hmask'], w['wabs'], w['rsel'], w['wukt'], w['wuv'])
    grid_spec = pltpu.PrefetchScalarGridSpec(
        num_scalar_prefetch=1,
        grid=(bd, steps),
        in_specs=[per_b(q), per_b(new_lat), per_b(new_krt)] + [full(a) for a in consts]
        + [page_spec((PAGE_SIZE, KV_LORA), k) for k in range(g)]
        + [page_spec((D_ROPE, PAGE_SIZE), k) for k in range(g)],
        out_specs=pl.BlockSpec((1, t_new, ATT_WIDTH), lambda b, j, pt: (b, 0, 0)),
        scratch_shapes=[
            pltpu.VMEM((rows, KV_LORA), BF16),
            pltpu.VMEM((rows, D_ROPE), BF16),
            pltpu.VMEM((rows, 1), F32),
            pltpu.VMEM((rows, 1), F32),
            pltpu.VMEM((rows, KV_LORA), F32),
        ],
    )
    return pl.pallas_call(
        _sample_attn_body,
        grid_spec=grid_spec,
        out_shape=jax.ShapeDtypeStruct((bd, t_new, ATT_WIDTH), F32),
        compiler_params=_cparams(("parallel", "arbitrary")),
        name="sample_attn",
    )(page_table, q, new_lat, new_krt, *consts, *([cache_lat] * g), *([cache_krt] * g))


def _s5_body(a_ref, h0_ref, u_ref, bbig_ref, cbig_ref, d_ref, y_ref, hl_ref, st_ref, h_ref,
             *, nseq, steps, pitch, group_pitch):
    groups, rows, _ = u_ref.shape
    slabs = STATE_WIDTH // LANES
    half = slabs // 2

    @pl.when(pl.program_id(0) == 0)
    def _():
        h_ref[...] = h0_ref[...]

    for g in range(groups):
        inc = _dot(u_ref[g].astype(BF16), bbig_ref[...])
        for k in range(slabs):
            st_ref[k, g * group_pitch:g * group_pitch + rows, :] = inc[:, k * LANES:(k + 1) * LANES]

    a = a_ref[...]
    ar = [a[:, k * LANES:(k + 1) * LANES] for k in range(half)]
    ai = [a[:, (half + k) * LANES:(half + k + 1) * LANES] for k in range(half)]

    def body(t, carry):
        out = []
        for k in range(half):
            hr, hi = carry[k], carry[half + k]
            sel = pl.ds(t, nseq, stride=pitch)
            nr = ar[k] * hr - ai[k] * hi + st_ref[k, sel, :]
            ni = ar[k] * hi + ai[k] * hr + st_ref[half + k, sel, :]
            st_ref[k, sel, :] = nr
            st_ref[half + k, sel, :] = ni
            out.append((nr, ni))
        return tuple(o[0] for o in out) + tuple(o[1] for o in out)

    h = h_ref[...]
    final = lax.fori_loop(0, steps, body, tuple(h[:, k * LANES:(k + 1) * LANES] for k in range(slabs)))
    h_new = jnp.concatenate(final, axis=1)
    h_ref[...] = h_new
    hl_ref[...] = h_new

    d = d_ref[...]
    for g in range(groups):
        lo = g * group_pitch
        states = jnp.concatenate([st_ref[k, lo:lo + rows, :].astype(BF16) for k in range(slabs)], axis=1)
        y_ref[g] = _dot(states, cbig_ref[...]) + d * u_ref[g]


def _s5(u3, h0, ssm, nseq, steps, pitch, group_pitch):
    a_bar, bbig, cbig, d_row = ssm
    groups, rows_total, _ = u3.shape
    rows = steps if groups == nseq else rows_total
    n_tiles = rows_total // rows
    scratch_rows = (groups - 1) * group_pitch + rows
    full = lambda arr: pl.BlockSpec(arr.shape, lambda i: (0,) * arr.ndim)
    blk = pl.BlockSpec((groups, rows, SSM_WIDTH), lambda i: (0, i, 0))
    return pl.pallas_call(
        functools.partial(_s5_body, nseq=nseq, steps=steps, pitch=pitch, group_pitch=group_pitch),
        grid=(n_tiles,),
        in_specs=[full(a_bar), full(h0), blk, full(bbig), full(cbig), full(d_row)],
        out_specs=(blk, full(h0)),
        out_shape=(jax.ShapeDtypeStruct(u3.shape, F32), jax.ShapeDtypeStruct(h0.shape, F32)),
        scratch_shapes=[pltpu.VMEM((STATE_WIDTH // LANES, scratch_rows, LANES), F32),
                        pltpu.VMEM(h0.shape, F32)],
        compiler_params=_cparams(("arbitrary",)),
        name="s5",
    )(a_bar, h0, u3, bbig, cbig, d_row)


def _s5_weights(a_re, a_im, log_dt, b, c, d_skip):
    g = SSM_GROUPS
    a_re, a_im = a_re.astype(F32), a_im.astype(F32)
    dt = jnp.exp(log_dt.astype(F32))[:, None]
    lr, li = a_re * dt, a_im * dt
    ar, ai = jnp.exp(lr) * jnp.cos(li), jnp.exp(lr) * jnp.sin(li)
    den = a_re * a_re + a_im * a_im
    fr = ((ar - 1.0) * a_re + ai * a_im) / den
    fi = (ai * a_re - (ar - 1.0) * a_im) / den
    bre, bim = b[..., 0].astype(F32), b[..., 1].astype(F32)
    bbr = fr[..., None] * bre - fi[..., None] * bim
    bbi = fr[..., None] * bim + fi[..., None] * bre
    cre, cim = c[..., 0].astype(F32), c[..., 1].astype(F32)
    eye = jnp.eye(g, dtype=F32)
    bbig = jnp.einsum('rgph,gq->ghrqp', jnp.stack([bbr, bbi]), eye).reshape(SSM_WIDTH, STATE_WIDTH)
    cbig = jnp.einsum('rgop,gq->rqpgo', jnp.stack([cre, -cim]), eye).reshape(STATE_WIDTH, SSM_WIDTH)
    a_bar = jnp.concatenate([ar.reshape(1, -1), ai.reshape(1, -1)], axis=1)
    return a_bar, bbig.astype(BF16), cbig.astype(BF16), d_skip.astype(F32).reshape(1, -1)


def _mix_body(x_ref, y_ref, o_ref, wglu_ref, wout_ref, gffn_ref, x1_ref, xn_ref, *, att_transposed):
    gl = _dot(jax.nn.gelu(y_ref[...]).astype(BF16), wglu_ref[...])
    glu = gl[:, :SSM_WIDTH] * jax.nn.sigmoid(gl[:, SSM_WIDTH:])
    o = o_ref[...]
    if att_transposed:
        o = o.astype(F32).T
    x1 = x_ref[...] + _dot(glu.astype(BF16), wout_ref[:SSM_WIDTH, :]) + _dot(o.astype(BF16), wout_ref[SSM_WIDTH:, :])
    x1_ref[...] = x1
    xn_ref[...] = _rms(x1, gffn_ref[...]).astype(BF16)


def _mix(x, y, o, w, att_transposed):
    n = x.shape[0]
    tm = TOK_TILE
    tok = lambda width: pl.BlockSpec((tm, width), lambda i: (i, 0))
    full = lambda a: pl.BlockSpec(a.shape, lambda i: (0,) * a.ndim)
    o_spec = pl.BlockSpec((ATT_WIDTH, tm), lambda i: (0, i)) if att_transposed else tok(ATT_WIDTH)
    consts = (w['wglu'], w['wout'], w['gffn'])
    return pl.pallas_call(
        functools.partial(_mix_body, att_transposed=att_transposed),
        grid=(n // tm,),
        in_specs=[tok(D_MODEL), tok(SSM_WIDTH), o_spec] + [full(a) for a in consts],
        out_specs=(tok(D_MODEL), tok(D_MODEL)),
        out_shape=(jax.ShapeDtypeStruct((n, D_MODEL), F32), jax.ShapeDtypeStruct((n, D_MODEL), BF16)),
        compiler_params=_cparams(("parallel",)),
        name="mix",
    )(x, y, o, *consts)


def _desc_top(vals_fn, count):
    tops = []
    prev = None
    for _ in range(count):
        cur = vals_fn(prev)
        tops.append(cur)
        prev = cur
    return tops


def _bf16_bits(x):
    u = lax.bitcast_convert_type(x, U32)
    return (u + jnp.uint32(0x7FFF) + ((u >> 16) & jnp.uint32(1))) >> 16


def _pack_bf16_pair(even, odd):
    return _bf16_bits(even) | (_bf16_bits(odd) << 16)


def _peer_route_body(xn_ref, wq_ref, keys_ref, rank_ref, e2_ref, lim_ref, cc_ref):
    xn = xn_ref[...]
    q = _dot(xn, wq_ref[...]).astype(BF16)
    k1 = keys_ref[0]
    k2 = keys_ref[1]
    tops1, tops2, s1_all, s2_all = [], [], [], []
    for h in range(PEER_HEADS):
        base = h * 2 * PEER_HALF
        s1 = _dot_nt(k1, q[:, base:base + PEER_HALF])
        s2 = _dot_nt(k2, q[:, base + PEER_HALF:base + 2 * PEER_HALF])
        s1_all.append(s1)
        s2_all.append(s2)
        for s, tops in ((s1, tops1), (s2, tops2)):
            def nxt(prev, s=s):
                cand = s if prev is None else jnp.where(s < prev, s, -jnp.inf)
                return jnp.max(cand, axis=0, keepdims=True)
            tops.append(_desc_top(nxt, PEER_TOPK))
    a = [jnp.concatenate([tops1[h][r] for h in range(PEER_HEADS)], axis=0) for r in range(PEER_TOPK)]
    b = [jnp.concatenate([tops2[h][r] for h in range(PEER_HEADS)], axis=0) for r in range(PEER_TOPK)]
    cands = [a[i] + b[j] for i in range(PEER_TOPK) for j in range(PEER_TOPK) if (i + 1) * (j + 1) <= PEER_TOPK]

    def nxt_sum(prev):
        best = None
        for cnd in cands:
            v = cnd if prev is None else jnp.where(cnd < prev, cnd, -jnp.inf)
            best = v if best is None else jnp.maximum(best, v)
        return best

    tsum = _desc_top(nxt_sum, PEER_TOPK)
    tau = tsum[-1]
    zsum = jnp.zeros_like(tau)
    for t in tsum:
        zsum = zsum + jnp.exp(t - tsum[0])
    inv_z = 1.0 / zsum
    count = []
    for i in range(PEER_TOPK):
        cnt = jnp.zeros_like(tau)
        for j in range(PEER_TOPK):
            if (i + 1) * (j + 1) <= PEER_TOPK:
                cnt = cnt + jnp.where(a[i] + b[j] >= tau, 1.0, 0.0)
        count.append(cnt)
    for h in range(PEER_HEADS):
        row = slice(h, h + 1)
        lim = jnp.zeros(s1_all[h].shape, F32)
        rank = jnp.full(s2_all[h].shape, float(PEER_KEYS), F32)
        for r in range(PEER_TOPK):
            lim = jnp.where(s1_all[h] == a[r][row], count[r][row], lim)
            rank = jnp.where(s2_all[h] == b[r][row], float(r), rank)
        cc = jnp.exp(s1_all[h] - a[0][row]) * inv_z[row]
        e2 = jnp.exp(s2_all[h] - b[0][row])
        half = PEER_KEYS // 2
        lim_ref[h] = _pack_bf16_pair(lim, lim)
        cc_ref[h] = _pack_bf16_pair(cc, cc)
        rank_ref[h] = _pack_bf16_pair(rank[:half], rank[half:])
        e2_ref[h] = _pack_bf16_pair(e2[:half], e2[half:])


def _peer_route(xn, w):
    n = xn.shape[0]
    tm = TOK_TILE
    spec = lambda rows: pl.BlockSpec((PEER_HEADS, rows, tm), lambda i: (0, 0, i))
    shp = lambda rows: jax.ShapeDtypeStruct((PEER_HEADS, rows, n), U32)
    half = PEER_KEYS // 2
    return pl.pallas_call(
        _peer_route_body,
        grid=(n // tm,),
        in_specs=[pl.BlockSpec((tm, D_MODEL), lambda i: (i, 0)),
                  pl.BlockSpec(w['peer_wq'].shape, lambda i: (0, 0)),
                  pl.BlockSpec(w['peer_keys'].shape, lambda i: (0, 0, 0))],
        out_specs=(spec(half), spec(half), spec(PEER_KEYS), spec(PEER_KEYS)),
        out_shape=(shp(half), shp(half), shp(PEER_KEYS), shp(PEER_KEYS)),
        compiler_params=_cparams(("parallel",)),
        name="peer_route",
    )(xn, w['peer_wq'], w['peer_keys'])


def _peer_expert_body(xn_ref, x1_ref, u_ref, vt_ref, rank_ref, e2_ref, lim_ref, cc_ref, o_ref, acc_ref, ga_ref):
    j = pl.program_id(1)
    te = u_ref.shape[0]
    tm = xn_ref.shape[0]
    nsub = te // PEER_KEYS
    tiles = PEER_KEYS // BF16_ROWS

    @pl.when(j == 0)
    def _():
        acc_ref[...] = jnp.zeros(acc_ref.shape, F32)

    zero = jnp.zeros((), BF16)
    first = pl.multiple_of(j * nsub, SUBLANES)
    per_grp = MXU_DEPTH // PEER_KEYS
    xn = xn_ref[...]
    for grp in range(nsub // per_grp):
        rows = slice(grp * MXU_DEPTH, (grp + 1) * MXU_DEPTH)
        ga_ref[rows, :] = jax.nn.gelu(_dot_nt(u_ref[rows, :], xn)).astype(BF16)
    for grp in range(nsub // per_grp):
        rows = slice(grp * MXU_DEPTH, (grp + 1) * MXU_DEPTH)
        for sub in range(grp * per_grp, (grp + 1) * per_grp):
            rs = slice(sub * PEER_KEYS, (sub + 1) * PEER_KEYS)
            for c in range(tm // LANES):
                ls = slice(c * LANES, (c + 1) * LANES)
                gate = None
                for h in range(PEER_HEADS):
                    lim = lim_ref[h, pl.ds(first, nsub), ls][sub:sub + 1]
                    cc = cc_ref[h, pl.ds(first, nsub), ls][sub:sub + 1]
                    lim = pltpu.bitcast(jnp.broadcast_to(lim, (SUBLANES, LANES)), BF16)
                    cc = pltpu.bitcast(jnp.broadcast_to(cc, (SUBLANES, LANES)), BF16)
                    rank = pltpu.bitcast(rank_ref[h, :, ls], BF16).reshape(tiles, BF16_ROWS, LANES)
                    e2 = pltpu.bitcast(e2_ref[h, :, ls], BF16).reshape(tiles, BF16_ROWS, LANES)
                    wgt = jnp.where(rank < lim[None], e2 * cc[None], zero)
                    gate = wgt if gate is None else gate + wgt
                ga_ref[rs, ls] = gate.reshape(PEER_KEYS, LANES) * ga_ref[rs, ls]
    acc_ref[...] += _dot(vt_ref[...], ga_ref[...])

    @pl.when(j == pl.num_programs(1) - 1)
    def _():
        o_ref[...] = x1_ref[...] + acc_ref[...].T


def _peer_experts(xn, x1, routes, w):
    n = xn.shape[0]
    tm = PEER_TOK_TILE
    te = PEER_EXP_TILE
    n_exp = w['peer_u'].shape[0]
    tok = pl.BlockSpec((tm, D_MODEL), lambda i, j: (i, 0))
    route = lambda a: pl.BlockSpec((PEER_HEADS, a.shape[1], tm), lambda i, j: (0, 0, i))
    return pl.pallas_call(
        _peer_expert_body,
        grid=(n // tm, n_exp // te),
        in_specs=[tok, tok,
                  pl.BlockSpec((te, D_MODEL), lambda i, j: (j, 0)),
                  pl.BlockSpec((D_MODEL, te), lambda i, j: (0, j)),
                  ] + [route(a) for a in routes],
        out_specs=tok,
        out_shape=jax.ShapeDtypeStruct((n, D_MODEL), F32),
        scratch_shapes=[pltpu.VMEM((D_MODEL, tm), F32), pltpu.VMEM((te, tm), BF16)],
        compiler_params=_cparams(("parallel", "arbitrary")),
        name="peer_experts",
    )(xn, x1, w['peer_u'], w['peer_vt'], *routes)


def _rope_tables(pos):
    inv = ROPE_THETA ** (-jnp.arange(0, D_ROPE, 2, dtype=F32) / D_ROPE)
    ang = pos.astype(F32)[:, None] * inv[None, :]
    cos, sin = jnp.cos(ang), jnp.sin(ang)
    t = pos.shape[0]
    ones = jnp.ones((t, ROPE_LO), F32)
    tail = HEAD_BLOCK - ROPE_LO - D_ROPE
    cos_t = jnp.concatenate([ones, cos, cos, jnp.ones((t, tail), F32)], axis=1)
    sin_t = jnp.concatenate([0 * ones, -sin, sin, jnp.zeros((t, tail), F32)], axis=1)
    return cos_t, sin_t


def _head_blocks(wmat, lo, width):
    pad = jnp.zeros(wmat.shape[:2] + (HEAD_BLOCK,), wmat.dtype)
    return pad.at[:, :, lo:lo + width].set(wmat).reshape(wmat.shape[0], N_HEADS * HEAD_BLOCK)


def _interleave_halves(table):
    n, d = table.shape
    half = PEER_KEYS // 2
    return table.reshape(n // PEER_KEYS, 2, half, d).transpose(0, 2, 1, 3).reshape(n, d)


def _prep_weights(norm_mix, w_in, norm_q_lora, w_uq, norm_kv_lora, w_uk, w_uv, g_qn, g_qr, g_kn, g_kr,
                  w_glu, w_out, norm_ffn, peer_wq, peer_keys, peer_u, peer_v):
    row = lambda v: v.astype(F32).reshape(1, -1)
    kr_cols = jnp.zeros((D_MODEL, LANES), F32).at[:, ROPE_LO:ROPE_LO + D_ROPE].set(w_in[:, IN_WIDTH - D_ROPE:])
    win = jnp.concatenate([w_in[:, :IN_WIDTH - D_ROPE], kr_cols], axis=1)
    wuq = _head_blocks(w_uq.reshape(Q_LORA, N_HEADS, D_NOPE + D_ROPE), 0, D_NOPE + D_ROPE)
    wuk = _head_blocks(w_uk, 0, D_NOPE)
    wuvt = w_uv.reshape(KV_LORA, ATT_WIDTH).T
    lane = jnp.arange(HEAD_BLOCK)
    is_nope = lane < ROPE_LO
    is_rope = (lane >= ROPE_LO) & (lane < ROPE_LO + D_ROPE)
    seg = ((is_nope[:, None] & is_nope[None, :]) | (is_rope[:, None] & is_rope[None, :])).astype(BF16)
    icnt = jnp.where(is_nope, 1.0 / D_NOPE, jnp.where(is_rope, 1.0 / D_ROPE, 1.0)).astype(F32).reshape(1, -1)
    blockvec = lambda a, bvec: jnp.zeros((HEAD_BLOCK,), F32).at[:ROPE_LO].set(a).at[ROPE_LO:ROPE_LO + D_ROPE].set(bvec)
    qgain = (blockvec(g_qn, g_qr) * ATTN_SCALE).reshape(1, -1)
    kgain = blockvec(g_kn, jnp.zeros((D_ROPE,), F32)).reshape(1, -1)
    krgain = blockvec(jnp.zeros((D_NOPE,), F32), g_kr).reshape(1, -1)
    wabs = jnp.einsum('chd,d->hdc', w_uk, g_kn)
    wabs = jnp.zeros((N_HEADS, HEAD_BLOCK, KV_LORA), F32).at[:, :D_NOPE, :].set(wabs)
    wabs = wabs.reshape(N_HEADS * HEAD_BLOCK, KV_LORA)
    rsel = jnp.zeros((N_HEADS, HEAD_BLOCK, D_ROPE), F32).at[:, ROPE_LO:ROPE_LO + D_ROPE, :].set(
        jnp.broadcast_to(jnp.eye(D_ROPE, dtype=F32), (N_HEADS, D_ROPE, D_ROPE)))
    rsel = rsel.reshape(N_HEADS * HEAD_BLOCK, D_ROPE)
    head_of_lane = jnp.arange(N_HEADS * HEAD_BLOCK) // HEAD_BLOCK
    hmask8 = (head_of_lane[None, :] == jnp.arange(N_HEADS)[:, None]).astype(F32)
    return {
        'gmix': row(norm_mix), 'gq': row(norm_q_lora), 'gkv': row(norm_kv_lora),
        'seg': seg, 'icnt': icnt, 'qgain': qgain, 'kgain': kgain, 'krgain': krgain,
        'win': win.astype(BF16), 'wuq': wuq.astype(BF16), 'wuk': wuk.astype(BF16), 'wuvt': wuvt.astype(BF16),
        'hmask8': hmask8, 'wabs': wabs.astype(BF16), 'rsel': rsel.astype(BF16),
        'wukt': w_uk.reshape(KV_LORA, N_HEADS * D_NOPE).T.astype(BF16),
        'wuv': w_uv.reshape(KV_LORA, ATT_WIDTH).astype(BF16),
        'wglu': w_glu.astype(BF16), 'wout': w_out.astype(BF16), 'gffn': row(norm_ffn),
        'peer_wq': peer_wq.astype(BF16), 'peer_keys': peer_keys.astype(BF16),
        'peer_u': _interleave_halves(peer_u.astype(BF16)),
        'peer_vt': _interleave_halves(peer_v.astype(BF16)).T,
    }


def _peer(xn, x1, w):
    return _peer_experts(xn, x1, _peer_route(xn, w), w)


def _state_in(st):
    return jnp.moveaxis(st.astype(F32), -1, 1).reshape(st.shape[0], STATE_WIDTH)


def _state_out(h):
    return jnp.moveaxis(h.reshape(h.shape[0], 2, SSM_GROUPS, SSM_STATE), 1, -1)


def kernel(x_prompt, x_sample, cache_kv_latent, cache_k_rope, state_ssm, page_table, norm_mix, w_in, norm_q_lora, w_uq, norm_kv_lora, w_uk, w_uv, qk_gain_q_nope, qk_gain_q_rope, qk_gain_k_nope, qk_gain_k_rope, ssm_a_re, ssm_a_im, ssm_log_dt, ssm_b, ssm_c, ssm_d, w_glu, w_out, norm_ffn, peer_wq, peer_keys, peer_u, peer_v):
    depth = norm_mix.shape[0]
    batch, seq, _ = x_prompt.shape
    dec_batch, dec_seq, _ = x_sample.shape
    n_dec = dec_batch * dec_seq
    past = page_table.shape[1] * PAGE_SIZE
    assert seq % Q_TILE == 0 and Q_TILE == KV_CHUNK and seq % S5_TIME_TILE == 0 and n_dec % PEER_TOK_TILE == 0
    assert page_table.shape[1] % PAGES_PER_STEP == 0 and TOK_TILE % dec_seq == 0 and dec_seq <= PAGE_SIZE

    cos_p, sin_p = _rope_tables(jnp.arange(seq))
    cos_s, sin_s = _rope_tables(past + jnp.arange(dec_seq))
    cos_s, sin_s = jnp.tile(cos_s, (TOK_TILE // dec_seq, 1)), jnp.tile(sin_s, (TOK_TILE // dec_seq, 1))
    cache_krt = jnp.swapaxes(cache_k_rope, 2, 3)

    xp = x_prompt.reshape(batch * seq, D_MODEL)
    xs = x_sample.reshape(n_dec, D_MODEL)
    outs = {k: [] for k in ('lat_p', 'kr_p', 'ssm_p', 'lat_s', 'kr_s', 'ssm_s')}
    for l in range(depth):
        w = _prep_weights(norm_mix[l], w_in[l], norm_q_lora[l], w_uq[l], norm_kv_lora[l], w_uk[l], w_uv[l],
                          qk_gain_q_nope[l], qk_gain_q_rope[l], qk_gain_k_nope[l], qk_gain_k_rope[l],
                          w_glu[l], w_out[l], norm_ffn[l], peer_wq[l], peer_keys[l], peer_u[l], peer_v[l])
        w['hmask'] = jnp.tile(w['hmask8'], (dec_seq, 1))
        ssm = _s5_weights(ssm_a_re[l], ssm_a_im[l], ssm_log_dt[l], ssm_b[l], ssm_c[l], ssm_d[l])

        u, q, ckv, kr, kcat, vt4 = _proj(xp, cos_p, sin_p, w)
        o_t = _prompt_attention(q, kcat, vt4, batch, seq)
        pitch = S5_TIME_TILE + S5_ROW_SKEW
        y, h_last = _s5(u.reshape(batch, seq, SSM_WIDTH), jnp.zeros((batch, STATE_WIDTH), F32), ssm,
                        nseq=batch, steps=S5_TIME_TILE, pitch=pitch, group_pitch=pitch)
        x1, xn = _mix(xp, y.reshape(batch * seq, SSM_WIDTH), o_t, w, att_transposed=True)
        xp = _peer(xn, x1, w)
        outs['lat_p'].append(ckv.reshape(batch, seq, KV_LORA))
        outs['kr_p'].append(kr.reshape(batch, seq, D_ROPE))
        outs['ssm_p'].append(_state_out(h_last))

        u, q, ckv, kr, _, _ = _proj(xs, cos_s, sin_s, w)
        pad_rows = ((0, 0), (0, PAGE_SIZE - dec_seq), (0, 0))
        new_lat = jnp.pad(ckv.reshape(dec_batch, dec_seq, KV_LORA), pad_rows)
        new_krt = jnp.swapaxes(jnp.pad(kr.reshape(dec_batch, dec_seq, D_ROPE), pad_rows), 1, 2)
        o = _sample_attention(q.reshape(dec_batch, dec_seq, -1), new_lat, new_krt,
                              cache_kv_latent[l:l + 1], cache_krt[l:l + 1], page_table, w)
        y, h_last = _s5(u.reshape(1, n_dec, SSM_WIDTH), _state_in(state_ssm[l]), ssm,
                        nseq=dec_batch, steps=dec_seq, pitch=dec_seq, group_pitch=n_dec)
        x1, xn = _mix(xs, y.reshape(n_dec, SSM_WIDTH), o.reshape(n_dec, ATT_WIDTH), w, att_transposed=False)
        xs = _peer(xn, x1, w)
        outs['lat_s'].append(ckv.reshape(dec_batch, dec_seq, KV_LORA))
        outs['kr_s'].append(kr.reshape(dec_batch, dec_seq, D_ROPE))
        outs['ssm_s'].append(_state_out(h_last))

    return (xp.reshape(batch, seq, D_MODEL), xs.reshape(dec_batch, dec_seq, D_MODEL),
            jnp.stack(outs['lat_p']), jnp.stack(outs['kr_p']), jnp.stack(outs['ssm_p']),
            jnp.stack(outs['lat_s']), jnp.stack(outs['kr_s']), jnp.stack(outs['ssm_s']))
```

```python
import functools
import math

import jax
import jax.numpy as jnp
from jax import lax
from jax.experimental import pallas as pl
from jax.experimental.pallas import tpu as pltpu

F32 = jnp.float32
BF16 = jnp.bfloat16

D_MODEL = 1024
SSM_WIDTH = 512
SSM_GROUP = 16
SSM_GROUPS = 32
SSM_STATE = 64
STATE_HALF = SSM_GROUPS * SSM_STATE
STATE_WIDTH = 2 * STATE_HALF
D_NOPE = 64
D_ROPE = 32
D_V = 64
N_HEADS = 8
ATT_WIDTH = N_HEADS * D_V
Q_LORA = 384
KV_LORA = 256
IN_WIDTH = SSM_WIDTH + Q_LORA + KV_LORA + D_ROPE
ROPE_THETA = 10000.0
ATTN_SCALE = 1.0 / math.sqrt(D_NOPE + D_ROPE)
PAGE_SIZE = 128
PEER_HEADS = 8
PEER_KEYS = 128
PEER_HALF = 128
PEER_TOPK = 16
EPS = 1e-6

LANES = 128
SUBLANES = 8
BF16_ROWS = 2 * SUBLANES
HEAD_BLOCK = LANES
ROPE_LO = D_NOPE
VMEM_LIMIT = 48 * 1024 * 1024

TOK_TILE = 256
KV_CHUNK = 2 * TOK_TILE
Q_TILE = 512
HEADS_PER_STEP = 2
S5_TIME_TILE = 128
S5_ROW_SKEW = 4
PAGES_PER_STEP = 32
PEER_TOK_TILE = 512
PEER_EXP_TILE = SUBLANES * PEER_KEYS
GATE_GROUP = 8
GATE_ROWS = 64
NEG_BIG = -1e30


def _cparams(sem):
    return pltpu.CompilerParams(dimension_semantics=sem, vmem_limit_bytes=VMEM_LIMIT)


def _dot(a, b):
    return jnp.dot(a, b, preferred_element_type=F32)


def _dot_nt(a, b):
    return lax.dot_general(a, b, (((1,), (1,)), ((), ())), preferred_element_type=F32)


def _rms(x, gain):
    return x * lax.rsqrt(jnp.mean(x * x, axis=-1, keepdims=True) + EPS) * gain


def _segment_sumsq(x, seg):
    sq = x * x
    hi = sq.astype(BF16)
    lo = (sq - hi.astype(F32)).astype(BF16)
    return _dot(hi, seg) + _dot(lo, seg)


def _rope_block(x, cos, sin, first_half):
    swapped = jnp.where(first_half, pltpu.roll(x, LANES - D_ROPE // 2, 1), pltpu.roll(x, D_ROPE // 2, 1))
    return x * cos + swapped * sin


def _proj_body(x_ref, cos_ref, sin_ref, gmix_ref, win_ref, gq_ref, wuq_ref, gkv_ref, wuk_ref, wuvt_ref,
               seg_ref, icnt_ref, qgain_ref, kgain_ref, krgain_ref,
               u_ref, q_ref, ckv_ref, kr_ref, kcat_ref, vt_ref):
    xn = _rms(x_ref[...], gmix_ref[...])
    z = _dot(xn.astype(BF16), win_ref[...])
    u_ref[...] = z[:, :SSM_WIDTH]
    cq = z[:, SSM_WIDTH:SSM_WIDTH + Q_LORA]
    ckv = z[:, SSM_WIDTH + Q_LORA:SSM_WIDTH + Q_LORA + KV_LORA]
    krc = z[:, SSM_WIDTH + Q_LORA + KV_LORA:]

    cos = cos_ref[...]
    sin = sin_ref[...]
    lane = lax.broadcasted_iota(jnp.int32, cos.shape, 1)
    first_half = lane < ROPE_LO + D_ROPE // 2
    seg = seg_ref[...]
    icnt = icnt_ref[...]

    ckvn = _rms(ckv, gkv_ref[...])
    ckv_ref[...] = ckvn
    ckvb = ckvn.astype(BF16)

    krn = krc * lax.rsqrt(jnp.sum(krc * krc, axis=-1, keepdims=True) * (1.0 / D_ROPE) + EPS) * krgain_ref[...]
    krr = _rope_block(krn, cos, sin, first_half)
    kr_ref[...] = krr[:, ROPE_LO:ROPE_LO + D_ROPE]

    qp = _dot(_rms(cq, gq_ref[...]).astype(BF16), wuq_ref[...])
    kp = _dot(ckvb, wuk_ref[...])
    qgain = qgain_ref[...]
    kgain = kgain_ref[...]
    q_blocks = []
    k_blocks = []
    for h in range(N_HEADS):
        sl = slice(h * HEAD_BLOCK, (h + 1) * HEAD_BLOCK)
        qh = qp[:, sl]
        qh = qh * lax.rsqrt(_segment_sumsq(qh, seg) * icnt + EPS) * qgain
        q_blocks.append(_rope_block(qh, cos, sin, first_half).astype(BF16))
        kh = kp[:, sl]
        kh = kh * lax.rsqrt(_segment_sumsq(kh, seg) * icnt + EPS) * kgain
        k_blocks.append((kh + krr).astype(BF16))
    q_ref[...] = jnp.concatenate(q_blocks, axis=1)
    kcat_ref[...] = jnp.concatenate(k_blocks, axis=1)

    vt = _dot_nt(wuvt_ref[...], ckvb)
    vt_ref[...] = vt.astype(BF16).reshape(N_HEADS, 1, D_V, vt.shape[1])


def _proj(x, cos, sin, w):
    n = x.shape[0]
    tm = TOK_TILE
    nt = n // tm
    cos_tiles = cos.shape[0] // tm
    full = lambda a: pl.BlockSpec(a.shape, lambda i: (0,) * a.ndim)
    tok = lambda width: pl.BlockSpec((tm, width), lambda i: (i, 0))
    tab = pl.BlockSpec((tm, LANES), lambda i: (i % cos_tiles, 0))
    consts = (w['gmix'], w['win'], w['gq'], w['wuq'], w['gkv'], w['wuk'], w['wuvt'],
              w['seg'], w['icnt'], w['qgain'], w['kgain'], w['krgain'])
    out_shape = (
        jax.ShapeDtypeStruct((n, SSM_WIDTH), F32),
        jax.ShapeDtypeStruct((n, N_HEADS * HEAD_BLOCK), BF16),
        jax.ShapeDtypeStruct((n, KV_LORA), F32),
        jax.ShapeDtypeStruct((n, D_ROPE), F32),
        jax.ShapeDtypeStruct((n, N_HEADS * HEAD_BLOCK), BF16),
        jax.ShapeDtypeStruct((N_HEADS, nt, D_V, tm), BF16),
    )
    out_specs = (tok(SSM_WIDTH), tok(N_HEADS * HEAD_BLOCK), tok(KV_LORA), tok(D_ROPE),
                 tok(N_HEADS * HEAD_BLOCK),
                 pl.BlockSpec((N_HEADS, 1, D_V, tm), lambda i: (0, i, 0, 0)))
    return pl.pallas_call(
        _proj_body,
        grid=(nt,),
        in_specs=[tok(D_MODEL), tab, tab] + [full(a) for a in consts],
        out_specs=out_specs,
        out_shape=out_shape,
        compiler_params=_cparams(("parallel",)),
        name="proj",
    )(x, cos, sin, *consts)


def _prompt_attn_body(q_ref, k_ref, vt_ref, o_ref, s_ref):
    tq = q_ref.shape[0]
    tk = KV_CHUNK
    per_chunk = tk // vt_ref.shape[3]
    heads = vt_ref.shape[0]
    qi = pl.program_id(2)

    def produce(c, hh):
        lanes = slice(hh * HEAD_BLOCK, (hh + 1) * HEAD_BLOCK)
        start = pl.multiple_of(c * tk, tk)
        s = _dot_nt(k_ref[pl.ds(start, tk), lanes], q_ref[:, lanes])
        s_ref[hh * 2 + c % 2] = s
        return jnp.max(s, axis=0, keepdims=True)

    def consume(c, hh, cmax, state, diagonal):
        m, l, acc = state
        s = s_ref[hh * 2 + c % 2]
        if diagonal:
            kpos = c * tk + lax.broadcasted_iota(jnp.int32, s.shape, 0)
            qpos = qi * tq + lax.broadcasted_iota(jnp.int32, s.shape, 1)
            s = jnp.where(kpos <= qpos, s, -jnp.inf)
            cmax = jnp.max(s, axis=0, keepdims=True)
        m_new = jnp.maximum(m, cmax)
        p = jnp.exp(s - m_new)
        alpha = jnp.exp(m - m_new)
        l = alpha * l + jnp.sum(p, axis=0, keepdims=True)
        vt = jnp.concatenate([vt_ref[hh, c * per_chunk + r] for r in range(per_chunk)], axis=1)
        acc = alpha * acc + _dot(vt, p.astype(BF16))
        return m_new, l, acc

    def body(c, carry):
        nxt = tuple(produce(c + 1, hh) for hh in range(heads))
        states = tuple(consume(c, hh, carry[0][hh], carry[1][hh], False) for hh in range(heads))
        return nxt, states

    states = tuple((jnp.full((1, tq), NEG_BIG, F32), jnp.zeros((1, tq), F32), jnp.zeros((D_V, tq), F32))
                   for _ in range(heads))
    first = tuple(produce(0, hh) for hh in range(heads))
    cmax, states = lax.fori_loop(0, qi, body, (first, states))
    for hh in range(heads):
        m, l, acc = consume(qi, hh, cmax[hh], states[hh], True)
        o_ref[hh * D_V:(hh + 1) * D_V, :] = (acc / l).astype(o_ref.dtype)


def _prompt_attention(q, kcat, vt4, batch, seq):
    tq = Q_TILE
    tile = vt4.shape[3]
    nq = seq // tq
    hs = HEADS_PER_STEP
    return pl.pallas_call(
        _prompt_attn_body,
        grid=(batch, N_HEADS // hs, nq),
        in_specs=[
            pl.BlockSpec((tq, hs * HEAD_BLOCK), lambda b, h, i: (b * nq + i, h)),
            pl.BlockSpec((seq, hs * HEAD_BLOCK), lambda b, h, i: (b, h)),
            pl.BlockSpec((hs, seq // tile, D_V, tile), lambda b, h, i: (h, b, 0, 0)),
        ],
        out_specs=pl.BlockSpec((hs * D_V, tq), lambda b, h, i: (h, b * nq + i)),
        out_shape=jax.ShapeDtypeStruct((ATT_WIDTH, batch * seq), BF16),
        scratch_shapes=[pltpu.VMEM((2 * hs, KV_CHUNK, tq), F32)],
        compiler_params=_cparams(("parallel", "parallel", "arbitrary")),
        name="prompt_attn",
    )(q, kcat, vt4)


def _sample_attn_body(pt_ref, q_ref, newlat_ref, newkrt_ref, hmask_ref, wabs_ref, rsel_ref, wukt_ref, wuv_ref,
                      *rest):
    g = PAGES_PER_STEP
    lat_refs = rest[:g]
    krt_refs = rest[g:2 * g]
    o_ref = rest[2 * g]
    qabs_sc, qr_sc, m_sc, l_sc, acc_sc = rest[2 * g + 1:]
    j = pl.program_id(1)
    t_new = q_ref.shape[1]
    rows = t_new * N_HEADS

    @pl.when(j == 0)
    def _():
        q = q_ref[0].astype(F32)
        qb = jnp.broadcast_to(q[:, None, :], (t_new, N_HEADS, q.shape[1])).reshape(rows, q.shape[1])
        qbd = (qb * hmask_ref[...]).astype(BF16)
        qabs_sc[...] = _dot(qbd, wabs_ref[...]).astype(BF16)
        qr_sc[...] = _dot(qbd, rsel_ref[...]).astype(BF16)
        m_sc[...] = jnp.full(m_sc.shape, NEG_BIG, F32)
        l_sc[...] = jnp.zeros(l_sc.shape, F32)
        acc_sc[...] = jnp.zeros(acc_sc.shape, F32)

    def attend(lat32, krt32, causal):
        lat = lat32.astype(BF16)
        tk = lat.shape[0]
        knt = _dot_nt(wukt_ref[...], lat)
        ssq = jnp.sum((knt * knt).reshape(N_HEADS, D_NOPE, tk), axis=1)
        rinv = lax.rsqrt(ssq * (1.0 / D_NOPE) + EPS)
        rinv = jnp.broadcast_to(rinv[None], (t_new, N_HEADS, tk)).reshape(rows, tk)
        s = _dot_nt(qabs_sc[...], lat) * rinv + _dot(qr_sc[...], krt32.astype(BF16))
        if causal:
            q_tok = lax.broadcasted_iota(jnp.int32, (t_new, N_HEADS, tk), 0).reshape(rows, tk)
            k_tok = lax.broadcasted_iota(jnp.int32, (rows, tk), 1)
            s = jnp.where(k_tok <= q_tok, s, -jnp.inf)
        m = m_sc[...]
        m_new = jnp.maximum(m, jnp.max(s, axis=1, keepdims=True))
        p = jnp.exp(s - m_new)
        alpha = jnp.exp(m - m_new)
        l_sc[...] = alpha * l_sc[...] + jnp.sum(p, axis=1, keepdims=True)
        acc_sc[...] = alpha * acc_sc[...] + _dot(p.astype(BF16), lat)
        m_sc[...] = m_new

    attend(jnp.concatenate([r[0, 0] for r in lat_refs], axis=0),
           jnp.concatenate([r[0, 0] for r in krt_refs], axis=1), causal=False)

    @pl.when(j == pl.num_programs(1) - 1)
    def _():
        attend(newlat_ref[0], newkrt_ref[0], causal=True)
        o_lat = acc_sc[...] / l_sc[...]
        full = _dot(o_lat.astype(BF16), wuv_ref[...])
        shape3 = (t_new, N_HEADS, ATT_WIDTH)
        col = lax.broadcasted_iota(jnp.int32, shape3, 2)
        lo = lax.broadcasted_iota(jnp.int32, shape3, 1) * D_V
        sel = jnp.where(col >= lo, jnp.where(col < lo + D_V, full.reshape(shape3), 0.0), 0.0)
        o_ref[0] = jnp.sum(sel, axis=1)


def _sample_attention(q, new_lat, new_krt, cache_lat, cache_krt, page_table, w):
    bd, t_new, _ = q.shape
    n_pages = page_table.shape[1]
    g = PAGES_PER_STEP
    steps = n_pages // g
    rows = t_new * N_HEADS
    per_b = lambda a: pl.BlockSpec((1,) + a.shape[1:], lambda b, j, pt: (b, 0, 0))
    full = lambda a: pl.BlockSpec(a.shape, lambda b, j, pt: (0,) * a.ndim)

    def page_spec(shape, k):
        return pl.BlockSpec((1, 1) + shape, lambda b, j, pt: (0, pt[b, j * g + k], 0, 0))

    consts = (w['hmask'], w['wabs'], w['rsel'], w['wukt'], w['wuv'])
    grid_spec = pltpu.PrefetchScalarGridSpec(
        num_scalar_prefetch=1,
        grid=(bd, steps),
        in_specs=[per_b(q), per_b(new_lat), per_b(new_krt)] + [full(a) for a in consts]
        + [page_spec((PAGE_SIZE, KV_LORA), k) for k in range(g)]
        + [page_spec((D_ROPE, PAGE_SIZE), k) for k in range(g)],
        out_specs=pl.BlockSpec((1, t_new, ATT_WIDTH), lambda b, j, pt: (b, 0, 0)),
        scratch_shapes=[
            pltpu.VMEM((rows, KV_LORA), BF16),
            pltpu.VMEM((rows, D_ROPE), BF16),
            pltpu.VMEM((rows, 1), F32),
            pltpu.VMEM((rows, 1), F32),
            pltpu.VMEM((rows, KV_LORA), F32),
        ],
    )
    return pl.pallas_call(
        _sample_attn_body,
        grid_spec=grid_spec,
        out_shape=jax.ShapeDtypeStruct((bd, t_new, ATT_WIDTH), F32),
        compiler_params=_cparams(("parallel", "arbitrary")),
        name="sample_attn",
    )(page_table, q, new_lat, new_krt, *consts, *([cache_lat] * g), *([cache_krt] * g))


def _s5_body(a_ref, h0_ref, u_ref, bbig_ref, cbig_ref, d_ref, y_ref, hl_ref, st_ref, h_ref,
             *, nseq, steps, pitch, group_pitch):
    groups, rows, _ = u_ref.shape
    slabs = STATE_WIDTH // LANES
    half = slabs // 2

    @pl.when(pl.program_id(0) == 0)
    def _():
        h_ref[...] = h0_ref[...]

    for g in range(groups):
        inc = _dot(u_ref[g].astype(BF16), bbig_ref[...])
        for k in range(slabs):
            st_ref[k, g * group_pitch:g * group_pitch + rows, :] = inc[:, k * LANES:(k + 1) * LANES]

    a = a_ref[...]
    ar = [a[:, k * LANES:(k + 1) * LANES] for k in range(half)]
    ai = [a[:, (half + k) * LANES:(half + k + 1) * LANES] for k in range(half)]

    def body(t, carry):
        out = []
        for k in range(half):
            hr, hi = carry[k], carry[half + k]
            sel = pl.ds(t, nseq, stride=pitch)
            nr = ar[k] * hr - ai[k] * hi + st_ref[k, sel, :]
            ni = ar[k] * hi + ai[k] * hr + st_ref[half + k, sel, :]
            st_ref[k, sel, :] = nr
            st_ref[half + k, sel, :] = ni
            out.append((nr, ni))
        return tuple(o[0] for o in out) + tuple(o[1] for o in out)

    h = h_ref[...]
    final = lax.fori_loop(0, steps, body, tuple(h[:, k * LANES:(k + 1) * LANES] for k in range(slabs)))
    h_new = jnp.concatenate(final, axis=1)
    h_ref[...] = h_new
    hl_ref[...] = h_new

    d = d_ref[...]
    for g in range(groups):
        lo = g * group_pitch
        states = jnp.concatenate([st_ref[k, lo:lo + rows, :].astype(BF16) for k in range(slabs)], axis=1)
        y_ref[g] = _dot(states, cbig_ref[...]) + d * u_ref[g]


def _s5(u3, h0, ssm, nseq, steps, pitch, group_pitch):
    a_bar, bbig, cbig, d_row = ssm
    groups, rows_total, _ = u3.shape
    rows = steps if groups == nseq else rows_total
    n_tiles = rows_total // rows
    scratch_rows = (groups - 1) * group_pitch + rows
    full = lambda arr: pl.BlockSpec(arr.shape, lambda i: (0,) * arr.ndim)
    blk = pl.BlockSpec((groups, rows, SSM_WIDTH), lambda i: (0, i, 0))
    return pl.pallas_call(
        functools.partial(_s5_body, nseq=nseq, steps=steps, pitch=pitch, group_pitch=group_pitch),
        grid=(n_tiles,),
        in_specs=[full(a_bar), full(h0), blk, full(bbig), full(cbig), full(d_row)],
        out_specs=(blk, full(h0)),
        out_shape=(jax.ShapeDtypeStruct(u3.shape, F32), jax.ShapeDtypeStruct(h0.shape, F32)),
        scratch_shapes=[pltpu.VMEM((STATE_WIDTH // LANES, scratch_rows, LANES), F32),
                        pltpu.VMEM(h0.shape, F32)],
        compiler_params=_cparams(("arbitrary",)),
        name="s5",
    )(a_bar, h0, u3, bbig, cbig, d_row)


def _s5_weights(a_re, a_im, log_dt, b, c, d_skip):
    g = SSM_GROUPS
    a_re, a_im = a_re.astype(F32), a_im.astype(F32)
    dt = jnp.exp(log_dt.astype(F32))[:, None]
    lr, li = a_re * dt, a_im * dt
    ar, ai = jnp.exp(lr) * jnp.cos(li), jnp.exp(lr) * jnp.sin(li)
    den = a_re * a_re + a_im * a_im
    fr = ((ar - 1.0) * a_re + ai * a_im) / den
    fi = (ai * a_re - (ar - 1.0) * a_im) / den
    bre, bim = b[..., 0].astype(F32), b[..., 1].astype(F32)
    bbr = fr[..., None] * bre - fi[..., None] * bim
    bbi = fr[..., None] * bim + fi[..., None] * bre
    cre, cim = c[..., 0].astype(F32), c[..., 1].astype(F32)
    eye = jnp.eye(g, dtype=F32)
    bbig = jnp.einsum('rgph,gq->ghrqp', jnp.stack([bbr, bbi]), eye).reshape(SSM_WIDTH, STATE_WIDTH)
    cbig = jnp.einsum('rgop,gq->rqpgo', jnp.stack([cre, -cim]), eye).reshape(STATE_WIDTH, SSM_WIDTH)
    a_bar = jnp.concatenate([ar.reshape(1, -1), ai.reshape(1, -1)], axis=1)
    return a_bar, bbig.astype(BF16), cbig.astype(BF16), d_skip.astype(F32).reshape(1, -1)


def _mix_body(x_ref, y_ref, o_ref, wglu_ref, wout_ref, gffn_ref, x1_ref, xn_ref, *, att_transposed):
    gl = _dot(jax.nn.gelu(y_ref[...]).astype(BF16), wglu_ref[...])
    glu = gl[:, :SSM_WIDTH] * jax.nn.sigmoid(gl[:, SSM_WIDTH:])
    o = o_ref[...]
    if att_transposed:
        o = o.astype(F32).T
    x1 = x_ref[...] + _dot(glu.astype(BF16), wout_ref[:SSM_WIDTH, :]) + _dot(o.astype(BF16), wout_ref[SSM_WIDTH:, :])
    x1_ref[...] = x1
    xn_ref[...] = _rms(x1, gffn_ref[...]).astype(BF16)


def _mix(x, y, o, w, att_transposed):
    n = x.shape[0]
    tm = TOK_TILE
    tok = lambda width: pl.BlockSpec((tm, width), lambda i: (i, 0))
    full = lambda a: pl.BlockSpec(a.shape, lambda i: (0,) * a.ndim)
    o_spec = pl.BlockSpec((ATT_WIDTH, tm), lambda i: (0, i)) if att_transposed else tok(ATT_WIDTH)
    consts = (w['wglu'], w['wout'], w['gffn'])
    return pl.pallas_call(
        functools.partial(_mix_body, att_transposed=att_transposed),
        grid=(n // tm,),
        in_specs=[tok(D_MODEL), tok(SSM_WIDTH), o_spec] + [full(a) for a in consts],
        out_specs=(tok(D_MODEL), tok(D_MODEL)),
        out_shape=(jax.ShapeDtypeStruct((n, D_MODEL), F32), jax.ShapeDtypeStruct((n, D_MODEL), BF16)),
        compiler_params=_cparams(("parallel",)),
        name="mix",
    )(x, y, o, *consts)


def _desc_top(vals_fn, count):
    tops = []
    prev = None
    for _ in range(count):
        cur = vals_fn(prev)
        tops.append(cur)
        prev = cur
    return tops


def _peer_route_body(xn_ref, wq_ref, keys_ref, rank_ref, e2_ref, lim_ref, cc_ref):
    xn = xn_ref[...]
    q = _dot(xn, wq_ref[...]).astype(BF16)
    k1 = keys_ref[0]
    k2 = keys_ref[1]
    tops1, tops2, s1_all, s2_all = [], [], [], []
    for h in range(PEER_HEADS):
        base = h * 2 * PEER_HALF
        s1 = _dot_nt(k1, q[:, base:base + PEER_HALF])
        s2 = _dot_nt(k2, q[:, base + PEER_HALF:base + 2 * PEER_HALF])
        s1_all.append(s1)
        s2_all.append(s2)
        for s, tops in ((s1, tops1), (s2, tops2)):
            def nxt(prev, s=s):
                cand = s if prev is None else jnp.where(s < prev, s, -jnp.inf)
                return jnp.max(cand, axis=0, keepdims=True)
            tops.append(_desc_top(nxt, PEER_TOPK))
    a = [jnp.concatenate([tops1[h][r] for h in range(PEER_HEADS)], axis=0) for r in range(PEER_TOPK)]
    b = [jnp.concatenate([tops2[h][r] for h in range(PEER_HEADS)], axis=0) for r in range(PEER_TOPK)]
    cands = [a[i] + b[j] for i in range(PEER_TOPK) for j in range(PEER_TOPK) if (i + 1) * (j + 1) <= PEER_TOPK]

    def nxt_sum(prev):
        best = None
        for cnd in cands:
            v = cnd if prev is None else jnp.where(cnd < prev, cnd, -jnp.inf)
            best = v if best is None else jnp.maximum(best, v)
        return best

    tsum = _desc_top(nxt_sum, PEER_TOPK)
    tau = tsum[-1]
    zsum = jnp.zeros_like(tau)
    for t in tsum:
        zsum = zsum + jnp.exp(t - tsum[0])
    inv_z = 1.0 / zsum
    count = []
    for i in range(PEER_TOPK):
        cnt = jnp.zeros_like(tau)
        for j in range(PEER_TOPK):
            if (i + 1) * (j + 1) <= PEER_TOPK:
                cnt = cnt + jnp.where(a[i] + b[j] >= tau, 1.0, 0.0)
        count.append(cnt)
    for h in range(PEER_HEADS):
        row = slice(h, h + 1)
        lim = jnp.zeros(s1_all[h].shape, F32)
        rank = jnp.full(s2_all[h].shape, float(PEER_KEYS), F32)
        for r in range(PEER_TOPK):
            lim = jnp.where(s1_all[h] == a[r][row], count[r][row], lim)
            rank = jnp.where(s2_all[h] == b[r][row], float(r), rank)
        lim_ref[h] = lim
        rank_ref[h] = rank
        cc_ref[h] = jnp.exp(s1_all[h] - a[0][row]) * inv_z[row]
        e2_ref[h] = jnp.exp(s2_all[h] - b[0][row])


def _peer_route(xn, w):
    n = xn.shape[0]
    tm = TOK_TILE
    spec = pl.BlockSpec((PEER_HEADS, PEER_KEYS, tm), lambda i: (0, 0, i))
    shp = jax.ShapeDtypeStruct((PEER_HEADS, PEER_KEYS, n), F32)
    return pl.pallas_call(
        _peer_route_body,
        grid=(n // tm,),
        in_specs=[pl.BlockSpec((tm, D_MODEL), lambda i: (i, 0)),
                  pl.BlockSpec(w['peer_wq'].shape, lambda i: (0, 0)),
                  pl.BlockSpec(w['peer_keys'].shape, lambda i: (0, 0, 0))],
        out_specs=(spec, spec, spec, spec),
        out_shape=(shp, shp, shp, shp),
        compiler_params=_cparams(("parallel",)),
        name="peer_route",
    )(xn, w['peer_wq'], w['peer_keys'])


def _peer_expert_body(xn_ref, x1_ref, u_ref, vt_ref, rank_ref, e2_ref, lim_ref, cc_ref, o_ref, acc_ref, act_ref,
                      ga_ref):
    j = pl.program_id(1)
    te = u_ref.shape[0]
    tm = xn_ref.shape[0]
    nsub = te // PEER_KEYS
    tiles = GATE_ROWS // BF16_ROWS

    @pl.when(j == 0)
    def _():
        acc_ref[...] = jnp.zeros(acc_ref.shape, F32)

    act_ref[...] = jax.nn.gelu(_dot_nt(u_ref[...], xn_ref[...]))
    zero = jnp.zeros((), BF16)
    first = pl.multiple_of(j * nsub, SUBLANES)
    for c in range(tm // LANES):
        ls = slice(c * LANES, (c + 1) * LANES)
        for grp in range(nsub // GATE_GROUP):
            for lo in range(0, PEER_KEYS, GATE_ROWS):
                ks = slice(lo, lo + GATE_ROWS)
                gates = [None] * GATE_GROUP
                for h in range(PEER_HEADS):
                    rank = rank_ref[h, ks, ls].astype(BF16).reshape(tiles, BF16_ROWS, LANES)
                    e2 = e2_ref[h, ks, ls].astype(BF16).reshape(tiles, BF16_ROWS, LANES)
                    lim8 = lim_ref[h, pl.ds(first, nsub), ls]
                    cc8 = cc_ref[h, pl.ds(first, nsub), ls]
                    for s in range(GATE_GROUP):
                        sub = grp * GATE_GROUP + s
                        lim = jnp.broadcast_to(lim8[sub:sub + 1], (BF16_ROWS, LANES)).astype(BF16)
                        cc = jnp.broadcast_to(cc8[sub:sub + 1], (BF16_ROWS, LANES)).astype(BF16)
                        wgt = jnp.where(rank < lim[None], e2 * cc[None], zero)
                        gates[s] = wgt if gates[s] is None else gates[s] + wgt
                for s in range(GATE_GROUP):
                    rs = slice((grp * GATE_GROUP + s) * PEER_KEYS + lo, (grp * GATE_GROUP + s) * PEER_KEYS + lo + GATE_ROWS)
                    ga_ref[rs, ls] = gates[s].reshape(GATE_ROWS, LANES) * act_ref[rs, ls].astype(BF16)
    acc_ref[...] += _dot(vt_ref[0], ga_ref[...])

    @pl.when(j == pl.num_programs(1) - 1)
    def _():
        o_ref[...] = x1_ref[...] + acc_ref[...].T


def _peer_experts(xn, x1, routes, w):
    n = xn.shape[0]
    tm = PEER_TOK_TILE
    te = PEER_EXP_TILE
    n_exp = w['peer_u'].shape[0]
    tok = pl.BlockSpec((tm, D_MODEL), lambda i, j: (i, 0))
    route = lambda a: pl.BlockSpec((PEER_HEADS, a.shape[1], tm), lambda i, j: (0, 0, i))
    return pl.pallas_call(
        _peer_expert_body,
        grid=(n // tm, n_exp // te),
        in_specs=[tok, tok,
                  pl.BlockSpec((te, D_MODEL), lambda i, j: (j, 0)),
                  pl.BlockSpec((1, D_MODEL, te), lambda i, j: (j, 0, 0)),
                  ] + [route(a) for a in routes],
        out_specs=tok,
        out_shape=jax.ShapeDtypeStruct((n, D_MODEL), F32),
        scratch_shapes=[pltpu.VMEM((D_MODEL, tm), F32), pltpu.VMEM((te, tm), F32), pltpu.VMEM((te, tm), BF16)],
        compiler_params=_cparams(("parallel", "arbitrary")),
        name="peer_experts",
    )(xn, x1, w['peer_u'], w['peer_vt'], *routes)


def _rope_tables(pos):
    inv = ROPE_THETA ** (-jnp.arange(0, D_ROPE, 2, dtype=F32) / D_ROPE)
    ang = pos.astype(F32)[:, None] * inv[None, :]
    cos, sin = jnp.cos(ang), jnp.sin(ang)
    t = pos.shape[0]
    ones = jnp.ones((t, ROPE_LO), F32)
    tail = HEAD_BLOCK - ROPE_LO - D_ROPE
    cos_t = jnp.concatenate([ones, cos, cos, jnp.ones((t, tail), F32)], axis=1)
    sin_t = jnp.concatenate([0 * ones, -sin, sin, jnp.zeros((t, tail), F32)], axis=1)
    return cos_t, sin_t


def _head_blocks(wmat, lo, width):
    pad = jnp.zeros(wmat.shape[:2] + (HEAD_BLOCK,), wmat.dtype)
    return pad.at[:, :, lo:lo + width].set(wmat).reshape(wmat.shape[0], N_HEADS * HEAD_BLOCK)


def _prep_weights(norm_mix, w_in, norm_q_lora, w_uq, norm_kv_lora, w_uk, w_uv, g_qn, g_qr, g_kn, g_kr,
                  w_glu, w_out, norm_ffn, peer_wq, peer_keys, peer_u, peer_v):
    row = lambda v: v.astype(F32).reshape(1, -1)
    kr_cols = jnp.zeros((D_MODEL, LANES), F32).at[:, ROPE_LO:ROPE_LO + D_ROPE].set(w_in[:, IN_WIDTH - D_ROPE:])
    win = jnp.concatenate([w_in[:, :IN_WIDTH - D_ROPE], kr_cols], axis=1)
    wuq = _head_blocks(w_uq.reshape(Q_LORA, N_HEADS, D_NOPE + D_ROPE), 0, D_NOPE + D_ROPE)
    wuk = _head_blocks(w_uk, 0, D_NOPE)
    wuvt = w_uv.reshape(KV_LORA, ATT_WIDTH).T
    lane = jnp.arange(HEAD_BLOCK)
    is_nope = lane < ROPE_LO
    is_rope = (lane >= ROPE_LO) & (lane < ROPE_LO + D_ROPE)
    seg = ((is_nope[:, None] & is_nope[None, :]) | (is_rope[:, None] & is_rope[None, :])).astype(BF16)
    icnt = jnp.where(is_nope, 1.0 / D_NOPE, jnp.where(is_rope, 1.0 / D_ROPE, 1.0)).astype(F32).reshape(1, -1)
    blockvec = lambda a, bvec: jnp.zeros((HEAD_BLOCK,), F32).at[:ROPE_LO].set(a).at[ROPE_LO:ROPE_LO + D_ROPE].set(bvec)
    qgain = (blockvec(g_qn, g_qr) * ATTN_SCALE).reshape(1, -1)
    kgain = blockvec(g_kn, jnp.zeros((D_ROPE,), F32)).reshape(1, -1)
    krgain = blockvec(jnp.zeros((D_NOPE,), F32), g_kr).reshape(1, -1)
    wabs = jnp.einsum('chd,d->hdc', w_uk, g_kn)
    wabs = jnp.zeros((N_HEADS, HEAD_BLOCK, KV_LORA), F32).at[:, :D_NOPE, :].set(wabs)
    wabs = wabs.reshape(N_HEADS * HEAD_BLOCK, KV_LORA)
    rsel = jnp.zeros((N_HEADS, HEAD_BLOCK, D_ROPE), F32).at[:, ROPE_LO:ROPE_LO + D_ROPE, :].set(
        jnp.broadcast_to(jnp.eye(D_ROPE, dtype=F32), (N_HEADS, D_ROPE, D_ROPE)))
    rsel = rsel.reshape(N_HEADS * HEAD_BLOCK, D_ROPE)
    head_of_lane = jnp.arange(N_HEADS * HEAD_BLOCK) // HEAD_BLOCK
    hmask8 = (head_of_lane[None, :] == jnp.arange(N_HEADS)[:, None]).astype(F32)
    return {
        'gmix': row(norm_mix), 'gq': row(norm_q_lora), 'gkv': row(norm_kv_lora),
        'seg': seg, 'icnt': icnt, 'qgain': qgain, 'kgain': kgain, 'krgain': krgain,
        'win': win.astype(BF16), 'wuq': wuq.astype(BF16), 'wuk': wuk.astype(BF16), 'wuvt': wuvt.astype(BF16),
        'hmask8': hmask8, 'wabs': wabs.astype(BF16), 'rsel': rsel.astype(BF16),
        'wukt': w_uk.reshape(KV_LORA, N_HEADS * D_NOPE).T.astype(BF16),
        'wuv': w_uv.reshape(KV_LORA, ATT_WIDTH).astype(BF16),
        'wglu': w_glu.astype(BF16), 'wout': w_out.astype(BF16), 'gffn': row(norm_ffn),
        'peer_wq': peer_wq.astype(BF16), 'peer_keys': peer_keys.astype(BF16),
        'peer_u': peer_u.astype(BF16),
        'peer_vt': peer_v.astype(BF16).reshape(-1, PEER_EXP_TILE, D_MODEL).transpose(0, 2, 1),
    }


def _peer(xn, x1, w):
    return _peer_experts(xn, x1, _peer_route(xn, w), w)


def _state_in(st):
    return jnp.moveaxis(st.astype(F32), -1, 1).reshape(st.shape[0], STATE_WIDTH)


def _state_out(h):
    return jnp.moveaxis(h.reshape(h.shape[0], 2, SSM_GROUPS, SSM_STATE), 1, -1)


def kernel(x_prompt, x_sample, cache_kv_latent, cache_k_rope, state_ssm, page_table, norm_mix, w_in, norm_q_lora, w_uq, norm_kv_lora, w_uk, w_uv, qk_gain_q_nope, qk_gain_q_rope, qk_gain_k_nope, qk_gain_k_rope, ssm_a_re, ssm_a_im, ssm_log_dt, ssm_b, ssm_c, ssm_d, w_glu, w_out, norm_ffn, peer_wq, peer_keys, peer_u, peer_v):
    depth = norm_mix.shape[0]
    batch, seq, _ = x_prompt.shape
    dec_batch, dec_seq, _ = x_sample.shape
    n_dec = dec_batch * dec_seq
    past = page_table.shape[1] * PAGE_SIZE
    assert seq % Q_TILE == 0 and Q_TILE == KV_CHUNK and seq % S5_TIME_TILE == 0 and n_dec % PEER_TOK_TILE == 0
    assert page_table.shape[1] % PAGES_PER_STEP == 0 and TOK_TILE % dec_seq == 0 and dec_seq <= PAGE_SIZE

    cos_p, sin_p = _rope_tables(jnp.arange(seq))
    cos_s, sin_s = _rope_tables(past + jnp.arange(dec_seq))
    cos_s, sin_s = jnp.tile(cos_s, (TOK_TILE // dec_seq, 1)), jnp.tile(sin_s, (TOK_TILE // dec_seq, 1))
    cache_krt = jnp.swapaxes(cache_k_rope, 2, 3)

    xp = x_prompt.reshape(batch * seq, D_MODEL)
    xs = x_sample.reshape(n_dec, D_MODEL)
    outs = {k: [] for k in ('lat_p', 'kr_p', 'ssm_p', 'lat_s', 'kr_s', 'ssm_s')}
    for l in range(depth):
        w = _prep_weights(norm_mix[l], w_in[l], norm_q_lora[l], w_uq[l], norm_kv_lora[l], w_uk[l], w_uv[l],
                          qk_gain_q_nope[l], qk_gain_q_rope[l], qk_gain_k_nope[l], qk_gain_k_rope[l],
                          w_glu[l], w_out[l], norm_ffn[l], peer_wq[l], peer_keys[l], peer_u[l], peer_v[l])
        w['hmask'] = jnp.tile(w['hmask8'], (dec_seq, 1))
        ssm = _s5_weights(ssm_a_re[l], ssm_a_im[l], ssm_log_dt[l], ssm_b[l], ssm_c[l], ssm_d[l])

        u, q, ckv, kr, kcat, vt4 = _proj(xp, cos_p, sin_p, w)
        o_t = _prompt_attention(q, kcat, vt4, batch, seq)
        pitch = S5_TIME_TILE + S5_ROW_SKEW
        y, h_last = _s5(u.reshape(batch, seq, SSM_WIDTH), jnp.zeros((batch, STATE_WIDTH), F32), ssm,
                        nseq=batch, steps=S5_TIME_TILE, pitch=pitch, group_pitch=pitch)
        x1, xn = _mix(xp, y.reshape(batch * seq, SSM_WIDTH), o_t, w, att_transposed=True)
        xp = _peer(xn, x1, w)
        outs['lat_p'].append(ckv.reshape(batch, seq, KV_LORA))
        outs['kr_p'].append(kr.reshape(batch, seq, D_ROPE))
        outs['ssm_p'].append(_state_out(h_last))

        u, q, ckv, kr, _, _ = _proj(xs, cos_s, sin_s, w)
        pad_rows = ((0, 0), (0, PAGE_SIZE - dec_seq), (0, 0))
        new_lat = jnp.pad(ckv.reshape(dec_batch, dec_seq, KV_LORA), pad_rows)
        new_krt = jnp.swapaxes(jnp.pad(kr.reshape(dec_batch, dec_seq, D_ROPE), pad_rows), 1, 2)
        o = _sample_attention(q.reshape(dec_batch, dec_seq, -1), new_lat, new_krt,
                              cache_kv_latent[l:l + 1], cache_krt[l:l + 1], page_table, w)
        y, h_last = _s5(u.reshape(1, n_dec, SSM_WIDTH), _state_in(state_ssm[l]), ssm,
                        nseq=dec_batch, steps=dec_seq, pitch=dec_seq, group_pitch=n_dec)
        x1, xn = _mix(xs, y.reshape(n_dec, SSM_WIDTH), o.reshape(n_dec, ATT_WIDTH), w, att_transposed=False)
        xs = _peer(xn, x1, w)
        outs['lat_s'].append(ckv.reshape(dec_batch, dec_seq, KV_LORA))
        outs['kr_s'].append(kr.reshape(dec_batch, dec_seq, D_ROPE))
        outs['ssm_s'].append(_state_out(h_last))

    return (xp.reshape(batch, seq, D_MODEL), xs.reshape(dec_batch, dec_seq, D_MODEL),
            jnp.stack(outs['lat_p']), jnp.stack(outs['kr_p']), jnp.stack(outs['ssm_p']),
            jnp.stack(outs['lat_s']), jnp.stack(outs['kr_s']), jnp.stack(outs['ssm_s']))
```

```python
import functools
import math

import jax
import jax.numpy as jnp
from jax import lax
from jax.experimental import pallas as pl
from jax.experimental.pallas import tpu as pltpu

F32 = jnp.float32
BF16 = jnp.bfloat16

D_MODEL = 1024
SSM_WIDTH = 512
SSM_GROUP = 16
SSM_GROUPS = 32
SSM_STATE = 64
STATE_HALF = SSM_GROUPS * SSM_STATE
STATE_WIDTH = 2 * STATE_HALF
D_NOPE = 64
D_ROPE = 32
D_V = 64
N_HEADS = 8
ATT_WIDTH = N_HEADS * D_V
Q_LORA = 384
KV_LORA = 256
IN_WIDTH = SSM_WIDTH + Q_LORA + KV_LORA + D_ROPE
ROPE_THETA = 10000.0
ATTN_SCALE = 1.0 / math.sqrt(D_NOPE + D_ROPE)
PAGE_SIZE = 128
PEER_HEADS = 8
PEER_KEYS = 128
PEER_HALF = 128
PEER_TOPK = 16
EPS = 1e-6

LANES = 128
SUBLANES = 8
BF16_ROWS = 2 * SUBLANES
HEAD_BLOCK = LANES
ROPE_LO = D_NOPE
VMEM_LIMIT = 48 * 1024 * 1024

TOK_TILE = 256
KV_CHUNK = 2 * TOK_TILE
Q_TILE = 512
HEADS_PER_STEP = 2
S5_TIME_TILE = 128
S5_ROW_SKEW = 4
PAGES_PER_STEP = 32
PEER_TOK_TILE = 512
PEER_EXP_TILE = SUBLANES * PEER_KEYS
GATE_GROUP = 8
GATE_ROWS = 64
NEG_BIG = -1e30


def _cparams(sem):
    return pltpu.CompilerParams(dimension_semantics=sem, vmem_limit_bytes=VMEM_LIMIT)


def _dot(a, b):
    return jnp.dot(a, b, preferred_element_type=F32)


def _dot_nt(a, b):
    return lax.dot_general(a, b, (((1,), (1,)), ((), ())), preferred_element_type=F32)


def _rms(x, gain):
    return x * lax.rsqrt(jnp.mean(x * x, axis=-1, keepdims=True) + EPS) * gain


def _gelu_tanh(x):
    k0 = -2.0 * math.sqrt(2.0 / math.pi) * math.log2(math.e)
    return x / (1.0 + jnp.exp2(x * (k0 + (k0 * 0.044715) * (x * x))))


def _segment_sumsq(x, seg):
    sq = x * x
    hi = sq.astype(BF16)
    lo = (sq - hi.astype(F32)).astype(BF16)
    return _dot(hi, seg) + _dot(lo, seg)


def _rope_block(x, cos, sin, first_half):
    swapped = jnp.where(first_half, pltpu.roll(x, LANES - D_ROPE // 2, 1), pltpu.roll(x, D_ROPE // 2, 1))
    return x * cos + swapped * sin


def _proj_body(x_ref, cos_ref, sin_ref, gmix_ref, win_ref, gq_ref, wuq_ref, gkv_ref, wuk_ref, wuvt_ref,
               seg_ref, icnt_ref, qgain_ref, kgain_ref, krgain_ref,
               u_ref, q_ref, ckv_ref, kr_ref, kcat_ref, vt_ref):
    xn = _rms(x_ref[...], gmix_ref[...])
    z = _dot(xn.astype(BF16), win_ref[...])
    u_ref[...] = z[:, :SSM_WIDTH]
    cq = z[:, SSM_WIDTH:SSM_WIDTH + Q_LORA]
    ckv = z[:, SSM_WIDTH + Q_LORA:SSM_WIDTH + Q_LORA + KV_LORA]
    krc = z[:, SSM_WIDTH + Q_LORA + KV_LORA:]

    cos = cos_ref[...]
    sin = sin_ref[...]
    lane = lax.broadcasted_iota(jnp.int32, cos.shape, 1)
    first_half = lane < ROPE_LO + D_ROPE // 2
    seg = seg_ref[...]
    icnt = icnt_ref[...]

    ckvn = _rms(ckv, gkv_ref[...])
    ckv_ref[...] = ckvn
    ckvb = ckvn.astype(BF16)

    krn = krc * lax.rsqrt(jnp.sum(krc * krc, axis=-1, keepdims=True) * (1.0 / D_ROPE) + EPS) * krgain_ref[...]
    krr = _rope_block(krn, cos, sin, first_half)
    kr_ref[...] = krr[:, ROPE_LO:ROPE_LO + D_ROPE]

    qp = _dot(_rms(cq, gq_ref[...]).astype(BF16), wuq_ref[...])
    kp = _dot(ckvb, wuk_ref[...])
    qgain = qgain_ref[...]
    kgain = kgain_ref[...]
    q_blocks = []
    k_blocks = []
    for h in range(N_HEADS):
        sl = slice(h * HEAD_BLOCK, (h + 1) * HEAD_BLOCK)
        qh = qp[:, sl]
        qh = qh * lax.rsqrt(_segment_sumsq(qh, seg) * icnt + EPS) * qgain
        q_blocks.append(_rope_block(qh, cos, sin, first_half).astype(BF16))
        kh = kp[:, sl]
        kh = kh * lax.rsqrt(_segment_sumsq(kh, seg) * icnt + EPS) * kgain
        k_blocks.append((kh + krr).astype(BF16))
    q_ref[...] = jnp.concatenate(q_blocks, axis=1)
    kcat_ref[...] = jnp.concatenate(k_blocks, axis=1)

    vt = _dot_nt(wuvt_ref[...], ckvb)
    vt_ref[...] = vt.astype(BF16).reshape(N_HEADS, 1, D_V, vt.shape[1])


def _proj(x, cos, sin, w):
    n = x.shape[0]
    tm = TOK_TILE
    nt = n // tm
    cos_tiles = cos.shape[0] // tm
    full = lambda a: pl.BlockSpec(a.shape, lambda i: (0,) * a.ndim)
    tok = lambda width: pl.BlockSpec((tm, width), lambda i: (i, 0))
    tab = pl.BlockSpec((tm, LANES), lambda i: (i % cos_tiles, 0))
    consts = (w['gmix'], w['win'], w['gq'], w['wuq'], w['gkv'], w['wuk'], w['wuvt'],
              w['seg'], w['icnt'], w['qgain'], w['kgain'], w['krgain'])
    out_shape = (
        jax.ShapeDtypeStruct((n, SSM_WIDTH), F32),
        jax.ShapeDtypeStruct((n, N_HEADS * HEAD_BLOCK), BF16),
        jax.ShapeDtypeStruct((n, KV_LORA), F32),
        jax.ShapeDtypeStruct((n, D_ROPE), F32),
        jax.ShapeDtypeStruct((n, N_HEADS * HEAD_BLOCK), BF16),
        jax.ShapeDtypeStruct((N_HEADS, nt, D_V, tm), BF16),
    )
    out_specs = (tok(SSM_WIDTH), tok(N_HEADS * HEAD_BLOCK), tok(KV_LORA), tok(D_ROPE),
                 tok(N_HEADS * HEAD_BLOCK),
                 pl.BlockSpec((N_HEADS, 1, D_V, tm), lambda i: (0, i, 0, 0)))
    return pl.pallas_call(
        _proj_body,
        grid=(nt,),
        in_specs=[tok(D_MODEL), tab, tab] + [full(a) for a in consts],
        out_specs=out_specs,
        out_shape=out_shape,
        compiler_params=_cparams(("parallel",)),
        name="proj",
    )(x, cos, sin, *consts)


def _prompt_attn_body(q_ref, k_ref, vt_ref, o_ref, s_ref):
    tq = q_ref.shape[0]
    tk = KV_CHUNK
    per_chunk = tk // vt_ref.shape[3]
    heads = vt_ref.shape[0]
    qi = pl.program_id(2)

    def produce(c, hh):
        lanes = slice(hh * HEAD_BLOCK, (hh + 1) * HEAD_BLOCK)
        start = pl.multiple_of(c * tk, tk)
        s = _dot_nt(k_ref[pl.ds(start, tk), lanes], q_ref[:, lanes])
        s_ref[hh * 2 + c % 2] = s
        return jnp.max(s, axis=0, keepdims=True)

    def consume(c, hh, cmax, state, diagonal):
        m, l, acc = state
        s = s_ref[hh * 2 + c % 2]
        if diagonal:
            kpos = c * tk + lax.broadcasted_iota(jnp.int32, s.shape, 0)
            qpos = qi * tq + lax.broadcasted_iota(jnp.int32, s.shape, 1)
            s = jnp.where(kpos <= qpos, s, -jnp.inf)
            cmax = jnp.max(s, axis=0, keepdims=True)
        m_new = jnp.maximum(m, cmax)
        p = jnp.exp(s - m_new)
        alpha = jnp.exp(m - m_new)
        l = alpha * l + jnp.sum(p, axis=0, keepdims=True)
        vt = jnp.concatenate([vt_ref[hh, c * per_chunk + r] for r in range(per_chunk)], axis=1)
        acc = alpha * acc + _dot(vt, p.astype(BF16))
        return m_new, l, acc

    def body(c, carry):
        nxt = tuple(produce(c + 1, hh) for hh in range(heads))
        states = tuple(consume(c, hh, carry[0][hh], carry[1][hh], False) for hh in range(heads))
        return nxt, states

    states = tuple((jnp.full((1, tq), NEG_BIG, F32), jnp.zeros((1, tq), F32), jnp.zeros((D_V, tq), F32))
                   for _ in range(heads))
    first = tuple(produce(0, hh) for hh in range(heads))
    cmax, states = lax.fori_loop(0, qi, body, (first, states))
    for hh in range(heads):
        m, l, acc = consume(qi, hh, cmax[hh], states[hh], True)
        o_ref[hh * D_V:(hh + 1) * D_V, :] = (acc / l).astype(o_ref.dtype)


def _prompt_attention(q, kcat, vt4, batch, seq):
    tq = Q_TILE
    tile = vt4.shape[3]
    nq = seq // tq
    hs = HEADS_PER_STEP
    return pl.pallas_call(
        _prompt_attn_body,
        grid=(batch, N_HEADS // hs, nq),
        in_specs=[
            pl.BlockSpec((tq, hs * HEAD_BLOCK), lambda b, h, i: (b * nq + i, h)),
            pl.BlockSpec((seq, hs * HEAD_BLOCK), lambda b, h, i: (b, h)),
            pl.BlockSpec((hs, seq // tile, D_V, tile), lambda b, h, i: (h, b, 0, 0)),
        ],
        out_specs=pl.BlockSpec((hs * D_V, tq), lambda b, h, i: (h, b * nq + i)),
        out_shape=jax.ShapeDtypeStruct((ATT_WIDTH, batch * seq), BF16),
        scratch_shapes=[pltpu.VMEM((2 * hs, KV_CHUNK, tq), F32)],
        compiler_params=_cparams(("parallel", "parallel", "arbitrary")),
        name="prompt_attn",
    )(q, kcat, vt4)


def _sample_attn_body(pt_ref, q_ref, newlat_ref, newkrt_ref, hmask_ref, wabs_ref, rsel_ref, wukt_ref, wuv_ref,
                      *rest):
    g = PAGES_PER_STEP
    lat_refs = rest[:g]
    krt_refs = rest[g:2 * g]
    o_ref = rest[2 * g]
    qabs_sc, qr_sc, m_sc, l_sc, acc_sc = rest[2 * g + 1:]
    j = pl.program_id(1)
    t_new = q_ref.shape[1]
    rows = t_new * N_HEADS

    @pl.when(j == 0)
    def _():
        q = q_ref[0].astype(F32)
        qb = jnp.broadcast_to(q[:, None, :], (t_new, N_HEADS, q.shape[1])).reshape(rows, q.shape[1])
        qbd = (qb * hmask_ref[...]).astype(BF16)
        qabs_sc[...] = _dot(qbd, wabs_ref[...]).astype(BF16)
        qr_sc[...] = _dot(qbd, rsel_ref[...]).astype(BF16)
        m_sc[...] = jnp.full(m_sc.shape, NEG_BIG, F32)
        l_sc[...] = jnp.zeros(l_sc.shape, F32)
        acc_sc[...] = jnp.zeros(acc_sc.shape, F32)

    def attend(lat32, krt32, causal):
        lat = lat32.astype(BF16)
        tk = lat.shape[0]
        knt = _dot_nt(wukt_ref[...], lat)
        ssq = jnp.sum((knt * knt).reshape(N_HEADS, D_NOPE, tk), axis=1)
        rinv = lax.rsqrt(ssq * (1.0 / D_NOPE) + EPS)
        rinv = jnp.broadcast_to(rinv[None], (t_new, N_HEADS, tk)).reshape(rows, tk)
        s = _dot_nt(qabs_sc[...], lat) * rinv + _dot(qr_sc[...], krt32.astype(BF16))
        if causal:
            q_tok = lax.broadcasted_iota(jnp.int32, (t_new, N_HEADS, tk), 0).reshape(rows, tk)
            k_tok = lax.broadcasted_iota(jnp.int32, (rows, tk), 1)
            s = jnp.where(k_tok <= q_tok, s, -jnp.inf)
        m = m_sc[...]
        m_new = jnp.maximum(m, jnp.max(s, axis=1, keepdims=True))
        p = jnp.exp(s - m_new)
        alpha = jnp.exp(m - m_new)
        l_sc[...] = alpha * l_sc[...] + jnp.sum(p, axis=1, keepdims=True)
        acc_sc[...] = alpha * acc_sc[...] + _dot(p.astype(BF16), lat)
        m_sc[...] = m_new

    attend(jnp.concatenate([r[0, 0] for r in lat_refs], axis=0),
           jnp.concatenate([r[0, 0] for r in krt_refs], axis=1), causal=False)

    @pl.when(j == pl.num_programs(1) - 1)
    def _():
        attend(newlat_ref[0], newkrt_ref[0], causal=True)
        o_lat = acc_sc[...] / l_sc[...]
        full = _dot(o_lat.astype(BF16), wuv_ref[...])
        shape3 = (t_new, N_HEADS, ATT_WIDTH)
        col = lax.broadcasted_iota(jnp.int32, shape3, 2)
        lo = lax.broadcasted_iota(jnp.int32, shape3, 1) * D_V
        sel = jnp.where(col >= lo, jnp.where(col < lo + D_V, full.reshape(shape3), 0.0), 0.0)
        o_ref[0] = jnp.sum(sel, axis=1)


def _sample_attention(q, new_lat, new_krt, cache_lat, cache_krt, page_table, w):
    bd, t_new, _ = q.shape
    n_pages = page_table.shape[1]
    g = PAGES_PER_STEP
    steps = n_pages // g
    rows = t_new * N_HEADS
    per_b = lambda a: pl.BlockSpec((1,) + a.shape[1:], lambda b, j, pt: (b, 0, 0))
    full = lambda a: pl.BlockSpec(a.shape, lambda b, j, pt: (0,) * a.ndim)

    def page_spec(shape, k):
        return pl.BlockSpec((1, 1) + shape, lambda b, j, pt: (0, pt[b, j * g + k], 0, 0))

    consts = (w['hmask'], w['wabs'], w['rsel'], w['wukt'], w['wuv'])
    grid_spec = pltpu.PrefetchScalarGridSpec(
        num_scalar_prefetch=1,
        grid=(bd, steps),
        in_specs=[per_b(q), per_b(new_lat), per_b(new_krt)] + [full(a) for a in consts]
        + [page_spec((PAGE_SIZE, KV_LORA), k) for k in range(g)]
        + [page_spec((D_ROPE, PAGE_SIZE), k) for k in range(g)],
        out_specs=pl.BlockSpec((1, t_new, ATT_WIDTH), lambda b, j, pt: (b, 0, 0)),
        scratch_shapes=[
            pltpu.VMEM((rows, KV_LORA), BF16),
            pltpu.VMEM((rows, D_ROPE), BF16),
            pltpu.VMEM((rows, 1), F32),
            pltpu.VMEM((rows, 1), F32),
            pltpu.VMEM((rows, KV_LORA), F32),
        ],
    )
    return pl.pallas_call(
        _sample_attn_body,
        grid_spec=grid_spec,
        out_shape=jax.ShapeDtypeStruct((bd, t_new, ATT_WIDTH), F32),
        compiler_params=_cparams(("parallel", "arbitrary")),
        name="sample_attn",
    )(page_table, q, new_lat, new_krt, *consts, *([cache_lat] * g), *([cache_krt] * g))


def _s5_body(a_ref, h0_ref, u_ref, bbig_ref, cbig_ref, d_ref, y_ref, hl_ref, st_ref, h_ref,
             *, nseq, steps, pitch, group_pitch):
    groups, rows, _ = u_ref.shape
    slabs = STATE_WIDTH // LANES
    half = slabs // 2

    @pl.when(pl.program_id(0) == 0)
    def _():
        h_ref[...] = h0_ref[...]

    for g in range(groups):
        inc = _dot(u_ref[g].astype(BF16), bbig_ref[...])
        for k in range(slabs):
            st_ref[k, g * group_pitch:g * group_pitch + rows, :] = inc[:, k * LANES:(k + 1) * LANES]

    a = a_ref[...]
    ar = [a[:, k * LANES:(k + 1) * LANES] for k in range(half)]
    ai = [a[:, (half + k) * LANES:(half + k + 1) * LANES] for k in range(half)]

    def body(t, carry):
        out = []
        for k in range(half):
            hr, hi = carry[k], carry[half + k]
            sel = pl.ds(t, nseq, stride=pitch)
            nr = ar[k] * hr - ai[k] * hi + st_ref[k, sel, :]
            ni = ar[k] * hi + ai[k] * hr + st_ref[half + k, sel, :]
            st_ref[k, sel, :] = nr
            st_ref[half + k, sel, :] = ni
            out.append((nr, ni))
        return tuple(o[0] for o in out) + tuple(o[1] for o in out)

    h = h_ref[...]
    final = lax.fori_loop(0, steps, body, tuple(h[:, k * LANES:(k + 1) * LANES] for k in range(slabs)))
    h_new = jnp.concatenate(final, axis=1)
    h_ref[...] = h_new
    hl_ref[...] = h_new

    d = d_ref[...]
    for g in range(groups):
        lo = g * group_pitch
        states = jnp.concatenate([st_ref[k, lo:lo + rows, :].astype(BF16) for k in range(slabs)], axis=1)
        y_ref[g] = _dot(states, cbig_ref[...]) + d * u_ref[g]


def _s5(u3, h0, ssm, nseq, steps, pitch, group_pitch):
    a_bar, bbig, cbig, d_row = ssm
    groups, rows_total, _ = u3.shape
    rows = steps if groups == nseq else rows_total
    n_tiles = rows_total // rows
    scratch_rows = (groups - 1) * group_pitch + rows
    full = lambda arr: pl.BlockSpec(arr.shape, lambda i: (0,) * arr.ndim)
    blk = pl.BlockSpec((groups, rows, SSM_WIDTH), lambda i: (0, i, 0))
    return pl.pallas_call(
        functools.partial(_s5_body, nseq=nseq, steps=steps, pitch=pitch, group_pitch=group_pitch),
        grid=(n_tiles,),
        in_specs=[full(a_bar), full(h0), blk, full(bbig), full(cbig), full(d_row)],
        out_specs=(blk, full(h0)),
        out_shape=(jax.ShapeDtypeStruct(u3.shape, F32), jax.ShapeDtypeStruct(h0.shape, F32)),
        scratch_shapes=[pltpu.VMEM((STATE_WIDTH // LANES, scratch_rows, LANES), F32),
                        pltpu.VMEM(h0.shape, F32)],
        compiler_params=_cparams(("arbitrary",)),
        name="s5",
    )(a_bar, h0, u3, bbig, cbig, d_row)


def _s5_weights(a_re, a_im, log_dt, b, c, d_skip):
    g = SSM_GROUPS
    a_re, a_im = a_re.astype(F32), a_im.astype(F32)
    dt = jnp.exp(log_dt.astype(F32))[:, None]
    lr, li = a_re * dt, a_im * dt
    ar, ai = jnp.exp(lr) * jnp.cos(li), jnp.exp(lr) * jnp.sin(li)
    den = a_re * a_re + a_im * a_im
    fr = ((ar - 1.0) * a_re + ai * a_im) / den
    fi = (ai * a_re - (ar - 1.0) * a_im) / den
    bre, bim = b[..., 0].astype(F32), b[..., 1].astype(F32)
    bbr = fr[..., None] * bre - fi[..., None] * bim
    bbi = fr[..., None] * bim + fi[..., None] * bre
    cre, cim = c[..., 0].astype(F32), c[..., 1].astype(F32)
    eye = jnp.eye(g, dtype=F32)
    bbig = jnp.einsum('rgph,gq->ghrqp', jnp.stack([bbr, bbi]), eye).reshape(SSM_WIDTH, STATE_WIDTH)
    cbig = jnp.einsum('rgop,gq->rqpgo', jnp.stack([cre, -cim]), eye).reshape(STATE_WIDTH, SSM_WIDTH)
    a_bar = jnp.concatenate([ar.reshape(1, -1), ai.reshape(1, -1)], axis=1)
    return a_bar, bbig.astype(BF16), cbig.astype(BF16), d_skip.astype(F32).reshape(1, -1)


def _mix_body(x_ref, y_ref, o_ref, wglu_ref, wout_ref, gffn_ref, x1_ref, xn_ref, *, att_transposed):
    gl = _dot(jax.nn.gelu(y_ref[...]).astype(BF16), wglu_ref[...])
    glu = gl[:, :SSM_WIDTH] * jax.nn.sigmoid(gl[:, SSM_WIDTH:])
    o = o_ref[...]
    if att_transposed:
        o = o.astype(F32).T
    x1 = x_ref[...] + _dot(glu.astype(BF16), wout_ref[:SSM_WIDTH, :]) + _dot(o.astype(BF16), wout_ref[SSM_WIDTH:, :])
    x1_ref[...] = x1
    xn_ref[...] = _rms(x1, gffn_ref[...]).astype(BF16)


def _mix(x, y, o, w, att_transposed):
    n = x.shape[0]
    tm = TOK_TILE
    tok = lambda width: pl.BlockSpec((tm, width), lambda i: (i, 0))
    full = lambda a: pl.BlockSpec(a.shape, lambda i: (0,) * a.ndim)
    o_spec = pl.BlockSpec((ATT_WIDTH, tm), lambda i: (0, i)) if att_transposed else tok(ATT_WIDTH)
    consts = (w['wglu'], w['wout'], w['gffn'])
    return pl.pallas_call(
        functools.partial(_mix_body, att_transposed=att_transposed),
        grid=(n // tm,),
        in_specs=[tok(D_MODEL), tok(SSM_WIDTH), o_spec] + [full(a) for a in consts],
        out_specs=(tok(D_MODEL), tok(D_MODEL)),
        out_shape=(jax.ShapeDtypeStruct((n, D_MODEL), F32), jax.ShapeDtypeStruct((n, D_MODEL), BF16)),
        compiler_params=_cparams(("parallel",)),
        name="mix",
    )(x, y, o, *consts)


def _desc_top(vals_fn, count):
    tops = []
    prev = None
    for _ in range(count):
        cur = vals_fn(prev)
        tops.append(cur)
        prev = cur
    return tops


def _sort_network(n):
    pairs = []
    p = 1
    while p < n:
        k = p
        while k >= 1:
            for j in range(k % p, n - k, 2 * k):
                for i in range(min(k, n - j - k)):
                    if (i + j) // (2 * p) == (i + j + k) // (2 * p):
                        pairs.append((i + j, i + j + k))
            k //= 2
        p *= 2
    return pairs


def _top_keys(s):
    depth = PEER_KEYS // SUBLANES
    cols = [s[k * SUBLANES:(k + 1) * SUBLANES] for k in range(depth)]
    for i, j in _sort_network(depth):
        cols[i], cols[j] = jnp.maximum(cols[i], cols[j]), jnp.minimum(cols[i], cols[j])
    tops = []
    for t in range(PEER_TOPK):
        top = jnp.max(cols[0], axis=0, keepdims=True)
        tops.append(top)
        popped = cols[0] == top
        for k in range(PEER_TOPK - 1 - t):
            cols[k] = jnp.where(popped, cols[k + 1], cols[k])
    return tops


def _peer_route_body(xn_ref, wq_ref, keys_ref, rank_ref, e2_ref, lim_ref, cc_ref):
    xn = xn_ref[...]
    q = _dot(xn, wq_ref[...]).astype(BF16)
    k1 = keys_ref[0]
    k2 = keys_ref[1]
    tops1, tops2, s1_all, s2_all = [], [], [], []
    for h in range(PEER_HEADS):
        base = h * 2 * PEER_HALF
        s1 = _dot_nt(k1, q[:, base:base + PEER_HALF])
        s2 = _dot_nt(k2, q[:, base + PEER_HALF:base + 2 * PEER_HALF])
        s1_all.append(s1)
        s2_all.append(s2)
        tops1.append(_top_keys(s1))
        tops2.append(_top_keys(s2))
    a = [jnp.concatenate([tops1[h][r] for h in range(PEER_HEADS)], axis=0) for r in range(PEER_TOPK)]
    b = [jnp.concatenate([tops2[h][r] for h in range(PEER_HEADS)], axis=0) for r in range(PEER_TOPK)]
    cands = [a[i] + b[j] for i in range(PEER_TOPK) for j in range(PEER_TOPK) if (i + 1) * (j + 1) <= PEER_TOPK]

    def nxt_sum(prev):
        best = None
        for cnd in cands:
            v = cnd if prev is None else jnp.where(cnd < prev, cnd, -jnp.inf)
            best = v if best is None else jnp.maximum(best, v)
        return best

    tsum = _desc_top(nxt_sum, PEER_TOPK)
    tau = tsum[-1]
    zsum = jnp.zeros_like(tau)
    for t in tsum:
        zsum = zsum + jnp.exp(t - tsum[0])
    inv_z = 1.0 / zsum
    count = []
    for i in range(PEER_TOPK):
        cnt = jnp.zeros_like(tau)
        for j in range(PEER_TOPK):
            if (i + 1) * (j + 1) <= PEER_TOPK:
                cnt = cnt + jnp.where(a[i] + b[j] >= tau, 1.0, 0.0)
        count.append(cnt)
    for h in range(PEER_HEADS):
        row = slice(h, h + 1)
        lim = jnp.zeros(s1_all[h].shape, F32)
        rank = jnp.full(s2_all[h].shape, float(PEER_KEYS), F32)
        for r in range(PEER_TOPK):
            lim = jnp.where(s1_all[h] == a[r][row], count[r][row], lim)
            rank = jnp.where(s2_all[h] == b[r][row], float(r), rank)
        lim_ref[h] = lim
        rank_ref[h] = rank
        cc_ref[h] = jnp.exp(s1_all[h] - a[0][row]) * inv_z[row]
        e2_ref[h] = jnp.exp(s2_all[h] - b[0][row])


def _peer_route(xn, w):
    n = xn.shape[0]
    tm = TOK_TILE
    spec = pl.BlockSpec((PEER_HEADS, PEER_KEYS, tm), lambda i: (0, 0, i))
    shp = jax.ShapeDtypeStruct((PEER_HEADS, PEER_KEYS, n), F32)
    return pl.pallas_call(
        _peer_route_body,
        grid=(n // tm,),
        in_specs=[pl.BlockSpec((tm, D_MODEL), lambda i: (i, 0)),
                  pl.BlockSpec(w['peer_wq'].shape, lambda i: (0, 0)),
                  pl.BlockSpec(w['peer_keys'].shape, lambda i: (0, 0, 0))],
        out_specs=(spec, spec, spec, spec),
        out_shape=(shp, shp, shp, shp),
        compiler_params=_cparams(("parallel",)),
        name="peer_route",
    )(xn, w['peer_wq'], w['peer_keys'])


def _peer_expert_body(xn_ref, x1_ref, u_ref, vt_ref, rank_ref, e2_ref, lim_ref, cc_ref, o_ref, acc_ref, act_ref,
                      ga_ref):
    j = pl.program_id(1)
    te = u_ref.shape[0]
    tm = xn_ref.shape[0]
    nsub = te // PEER_KEYS
    tiles = GATE_ROWS // BF16_ROWS

    @pl.when(j == 0)
    def _():
        acc_ref[...] = jnp.zeros(acc_ref.shape, F32)

    act_ref[...] = _gelu_tanh(_dot_nt(u_ref[...], xn_ref[...]))
    zero = jnp.zeros((), BF16)
    first = pl.multiple_of(j * nsub, SUBLANES)
    for c in range(tm // LANES):
        ls = slice(c * LANES, (c + 1) * LANES)
        for grp in range(nsub // GATE_GROUP):
            for lo in range(0, PEER_KEYS, GATE_ROWS):
                ks = slice(lo, lo + GATE_ROWS)
                gates = [None] * GATE_GROUP
                for h in range(PEER_HEADS):
                    rank = rank_ref[h, ks, ls].astype(BF16).reshape(tiles, BF16_ROWS, LANES)
                    e2 = e2_ref[h, ks, ls].astype(BF16).reshape(tiles, BF16_ROWS, LANES)
                    lim8 = lim_ref[h, pl.ds(first, nsub), ls]
                    cc8 = cc_ref[h, pl.ds(first, nsub), ls]
                    for s in range(GATE_GROUP):
                        sub = grp * GATE_GROUP + s
                        lim = jnp.broadcast_to(lim8[sub:sub + 1], (BF16_ROWS, LANES)).astype(BF16)
                        cc = jnp.broadcast_to(cc8[sub:sub + 1], (BF16_ROWS, LANES)).astype(BF16)
                        wgt = jnp.where(rank < lim[None], e2 * cc[None], zero)
                        gates[s] = wgt if gates[s] is None else gates[s] + wgt
                for s in range(GATE_GROUP):
                    rs = slice((grp * GATE_GROUP + s) * PEER_KEYS + lo, (grp * GATE_GROUP + s) * PEER_KEYS + lo + GATE_ROWS)
                    ga_ref[rs, ls] = gates[s].reshape(GATE_ROWS, LANES) * act_ref[rs, ls].astype(BF16)
    acc_ref[...] += _dot(vt_ref[0], ga_ref[...])

    @pl.when(j == pl.num_programs(1) - 1)
    def _():
        o_ref[...] = x1_ref[...] + acc_ref[...].T


def _peer_experts(xn, x1, routes, w):
    n = xn.shape[0]
    tm = PEER_TOK_TILE
    te = PEER_EXP_TILE
    n_exp = w['peer_u'].shape[0]
    tok = pl.BlockSpec((tm, D_MODEL), lambda i, j: (i, 0))
    route = lambda a: pl.BlockSpec((PEER_HEADS, a.shape[1], tm), lambda i, j: (0, 0, i))
    return pl.pallas_call(
        _peer_expert_body,
        grid=(n // tm, n_exp // te),
        in_specs=[tok, tok,
                  pl.BlockSpec((te, D_MODEL), lambda i, j: (j, 0)),
                  pl.BlockSpec((1, D_MODEL, te), lambda i, j: (j, 0, 0)),
                  ] + [route(a) for a in routes],
        out_specs=tok,
        out_shape=jax.ShapeDtypeStruct((n, D_MODEL), F32),
        scratch_shapes=[pltpu.VMEM((D_MODEL, tm), F32), pltpu.VMEM((te, tm), F32), pltpu.VMEM((te, tm), BF16)],
        compiler_params=_cparams(("parallel", "arbitrary")),
        name="peer_experts",
    )(xn, x1, w['peer_u'], w['peer_vt'], *routes)


def _rope_tables(pos):
    inv = ROPE_THETA ** (-jnp.arange(0, D_ROPE, 2, dtype=F32) / D_ROPE)
    ang = pos.astype(F32)[:, None] * inv[None, :]
    cos, sin = jnp.cos(ang), jnp.sin(ang)
    t = pos.shape[0]
    ones = jnp.ones((t, ROPE_LO), F32)
    tail = HEAD_BLOCK - ROPE_LO - D_ROPE
    cos_t = jnp.concatenate([ones, cos, cos, jnp.ones((t, tail), F32)], axis=1)
    sin_t = jnp.concatenate([0 * ones, -sin, sin, jnp.zeros((t, tail), F32)], axis=1)
    return cos_t, sin_t


def _head_blocks(wmat, lo, width):
    pad = jnp.zeros(wmat.shape[:2] + (HEAD_BLOCK,), wmat.dtype)
    return pad.at[:, :, lo:lo + width].set(wmat).reshape(wmat.shape[0], N_HEADS * HEAD_BLOCK)


def _prep_weights(norm_mix, w_in, norm_q_lora, w_uq, norm_kv_lora, w_uk, w_uv, g_qn, g_qr, g_kn, g_kr,
                  w_glu, w_out, norm_ffn, peer_wq, peer_keys, peer_u, peer_v):
    row = lambda v: v.astype(F32).reshape(1, -1)
    kr_cols = jnp.zeros((D_MODEL, LANES), F32).at[:, ROPE_LO:ROPE_LO + D_ROPE].set(w_in[:, IN_WIDTH - D_ROPE:])
    win = jnp.concatenate([w_in[:, :IN_WIDTH - D_ROPE], kr_cols], axis=1)
    wuq = _head_blocks(w_uq.reshape(Q_LORA, N_HEADS, D_NOPE + D_ROPE), 0, D_NOPE + D_ROPE)
    wuk = _head_blocks(w_uk, 0, D_NOPE)
    wuvt = w_uv.reshape(KV_LORA, ATT_WIDTH).T
    lane = jnp.arange(HEAD_BLOCK)
    is_nope = lane < ROPE_LO
    is_rope = (lane >= ROPE_LO) & (lane < ROPE_LO + D_ROPE)
    seg = ((is_nope[:, None] & is_nope[None, :]) | (is_rope[:, None] & is_rope[None, :])).astype(BF16)
    icnt = jnp.where(is_nope, 1.0 / D_NOPE, jnp.where(is_rope, 1.0 / D_ROPE, 1.0)).astype(F32).reshape(1, -1)
    blockvec = lambda a, bvec: jnp.zeros((HEAD_BLOCK,), F32).at[:ROPE_LO].set(a).at[ROPE_LO:ROPE_LO + D_ROPE].set(bvec)
    qgain = (blockvec(g_qn, g_qr) * ATTN_SCALE).reshape(1, -1)
    kgain = blockvec(g_kn, jnp.zeros((D_ROPE,), F32)).reshape(1, -1)
    krgain = blockvec(jnp.zeros((D_NOPE,), F32), g_kr).reshape(1, -1)
    wabs = jnp.einsum('chd,d->hdc', w_uk, g_kn)
    wabs = jnp.zeros((N_HEADS, HEAD_BLOCK, KV_LORA), F32).at[:, :D_NOPE, :].set(wabs)
    wabs = wabs.reshape(N_HEADS * HEAD_BLOCK, KV_LORA)
    rsel = jnp.zeros((N_HEADS, HEAD_BLOCK, D_ROPE), F32).at[:, ROPE_LO:ROPE_LO + D_ROPE, :].set(
        jnp.broadcast_to(jnp.eye(D_ROPE, dtype=F32), (N_HEADS, D_ROPE, D_ROPE)))
    rsel = rsel.reshape(N_HEADS * HEAD_BLOCK, D_ROPE)
    head_of_lane = jnp.arange(N_HEADS * HEAD_BLOCK) // HEAD_BLOCK
    hmask8 = (head_of_lane[None, :] == jnp.arange(N_HEADS)[:, None]).astype(F32)
    return {
        'gmix': row(norm_mix), 'gq': row(norm_q_lora), 'gkv': row(norm_kv_lora),
        'seg': seg, 'icnt': icnt, 'qgain': qgain, 'kgain': kgain, 'krgain': krgain,
        'win': win.astype(BF16), 'wuq': wuq.astype(BF16), 'wuk': wuk.astype(BF16), 'wuvt': wuvt.astype(BF16),
        'hmask8': hmask8, 'wabs': wabs.astype(BF16), 'rsel': rsel.astype(BF16),
        'wukt': w_uk.reshape(KV_LORA, N_HEADS * D_NOPE).T.astype(BF16),
        'wuv': w_uv.reshape(KV_LORA, ATT_WIDTH).astype(BF16),
        'wglu': w_glu.astype(BF16), 'wout': w_out.astype(BF16), 'gffn': row(norm_ffn),
        'peer_wq': peer_wq.astype(BF16), 'peer_keys': peer_keys.astype(BF16),
        'peer_u': peer_u.astype(BF16),
        'peer_vt': peer_v.astype(BF16).reshape(-1, PEER_EXP_TILE, D_MODEL).transpose(0, 2, 1),
    }


def _peer(xn, x1, w):
    return _peer_experts(xn, x1, _peer_route(xn, w), w)


def _state_in(st):
    return jnp.moveaxis(st.astype(F32), -1, 1).reshape(st.shape[0], STATE_WIDTH)


def _state_out(h):
    return jnp.moveaxis(h.reshape(h.shape[0], 2, SSM_GROUPS, SSM_STATE), 1, -1)


def kernel(x_prompt, x_sample, cache_kv_latent, cache_k_rope, state_ssm, page_table, norm_mix, w_in, norm_q_lora, w_uq, norm_kv_lora, w_uk, w_uv, qk_gain_q_nope, qk_gain_q_rope, qk_gain_k_nope, qk_gain_k_rope, ssm_a_re, ssm_a_im, ssm_log_dt, ssm_b, ssm_c, ssm_d, w_glu, w_out, norm_ffn, peer_wq, peer_keys, peer_u, peer_v):
    depth = norm_mix.shape[0]
    batch, seq, _ = x_prompt.shape
    dec_batch, dec_seq, _ = x_sample.shape
    n_dec = dec_batch * dec_seq
    past = page_table.shape[1] * PAGE_SIZE
    assert seq % Q_TILE == 0 and Q_TILE == KV_CHUNK and seq % S5_TIME_TILE == 0 and n_dec % PEER_TOK_TILE == 0
    assert page_table.shape[1] % PAGES_PER_STEP == 0 and TOK_TILE % dec_seq == 0 and dec_seq <= PAGE_SIZE

    cos_p, sin_p = _rope_tables(jnp.arange(seq))
    cos_s, sin_s = _rope_tables(past + jnp.arange(dec_seq))
    cos_s, sin_s = jnp.tile(cos_s, (TOK_TILE // dec_seq, 1)), jnp.tile(sin_s, (TOK_TILE // dec_seq, 1))
    cache_krt = jnp.swapaxes(cache_k_rope, 2, 3)

    xp = x_prompt.reshape(batch * seq, D_MODEL)
    xs = x_sample.reshape(n_dec, D_MODEL)
    outs = {k: [] for k in ('lat_p', 'kr_p', 'ssm_p', 'lat_s', 'kr_s', 'ssm_s')}
    for l in range(depth):
        w = _prep_weights(norm_mix[l], w_in[l], norm_q_lora[l], w_uq[l], norm_kv_lora[l], w_uk[l], w_uv[l],
                          qk_gain_q_nope[l], qk_gain_q_rope[l], qk_gain_k_nope[l], qk_gain_k_rope[l],
                          w_glu[l], w_out[l], norm_ffn[l], peer_wq[l], peer_keys[l], peer_u[l], peer_v[l])
        w['hmask'] = jnp.tile(w['hmask8'], (dec_seq, 1))
        ssm = _s5_weights(ssm_a_re[l], ssm_a_im[l], ssm_log_dt[l], ssm_b[l], ssm_c[l], ssm_d[l])

        u, q, ckv, kr, kcat, vt4 = _proj(xp, cos_p, sin_p, w)
        o_t = _prompt_attention(q, kcat, vt4, batch, seq)
        pitch = S5_TIME_TILE + S5_ROW_SKEW
        y, h_last = _s5(u.reshape(batch, seq, SSM_WIDTH), jnp.zeros((batch, STATE_WIDTH), F32), ssm,
                        nseq=batch, steps=S5_TIME_TILE, pitch=pitch, group_pitch=pitch)
        x1, xn = _mix(xp, y.reshape(batch * seq, SSM_WIDTH), o_t, w, att_transposed=True)
        xp = _peer(xn, x1, w)
        outs['lat_p'].append(ckv.reshape(batch, seq, KV_LORA))
        outs['kr_p'].append(kr.reshape(batch, seq, D_ROPE))
        outs['ssm_p'].append(_state_out(h_last))

        u, q, ckv, kr, _, _ = _proj(xs, cos_s, sin_s, w)
        pad_rows = ((0, 0), (0, PAGE_SIZE - dec_seq), (0, 0))
        new_lat = jnp.pad(ckv.reshape(dec_batch, dec_seq, KV_LORA), pad_rows)
        new_krt = jnp.swapaxes(jnp.pad(kr.reshape(dec_batch, dec_seq, D_ROPE), pad_rows), 1, 2)
        o = _sample_attention(q.reshape(dec_batch, dec_seq, -1), new_lat, new_krt,
                              cache_kv_latent[l:l + 1], cache_krt[l:l + 1], page_table, w)
        y, h_last = _s5(u.reshape(1, n_dec, SSM_WIDTH), _state_in(state_ssm[l]), ssm,
                        nseq=dec_batch, steps=dec_seq, pitch=dec_seq, group_pitch=n_dec)
        x1, xn = _mix(xs, y.reshape(n_dec, SSM_WIDTH), o.reshape(n_dec, ATT_WIDTH), w, att_transposed=False)
        xs = _peer(xn, x1, w)
        outs['lat_s'].append(ckv.reshape(dec_batch, dec_seq, KV_LORA))
        outs['kr_s'].append(kr.reshape(dec_batch, dec_seq, D_ROPE))
        outs['ssm_s'].append(_state_out(h_last))

    return (xp.reshape(batch, seq, D_MODEL), xs.reshape(dec_batch, dec_seq, D_MODEL),
            jnp.stack(outs['lat_p']), jnp.stack(outs['kr_p']), jnp.stack(outs['ssm_p']),
            jnp.stack(outs['lat_s']), jnp.stack(outs['kr_s']), jnp.stack(outs['ssm_s']))
```

```python
import functools
import math

import jax
import jax.numpy as jnp
from jax import lax
from jax.experimental import pallas as pl
from jax.experimental.pallas import tpu as pltpu

F32 = jnp.float32
BF16 = jnp.bfloat16

D_MODEL = 1024
SSM_WIDTH = 512
SSM_GROUP = 16
SSM_GROUPS = 32
SSM_STATE = 64
STATE_HALF = SSM_GROUPS * SSM_STATE
STATE_WIDTH = 2 * STATE_HALF
D_NOPE = 64
D_ROPE = 32
D_V = 64
N_HEADS = 8
ATT_WIDTH = N_HEADS * D_V
Q_LORA = 384
KV_LORA = 256
IN_WIDTH = SSM_WIDTH + Q_LORA + KV_LORA + D_ROPE
ROPE_THETA = 10000.0
ATTN_SCALE = 1.0 / math.sqrt(D_NOPE + D_ROPE)
PAGE_SIZE = 128
PEER_HEADS = 8
PEER_KEYS = 128
PEER_HALF = 128
PEER_TOPK = 16
EPS = 1e-6

LANES = 128
SUBLANES = 8
BF16_ROWS = 2 * SUBLANES
HEAD_BLOCK = LANES
ROPE_LO = D_NOPE
VMEM_LIMIT = 48 * 1024 * 1024

TOK_TILE = 256
KV_CHUNK = 2 * TOK_TILE
Q_TILE = 512
HEADS_PER_STEP = 2
S5_TIME_TILE = 128
S5_ROW_SKEW = 4
PAGES_PER_STEP = 32
PEER_TOK_TILE = 512
PEER_EXP_TILE = SUBLANES * PEER_KEYS
GATE_GROUP = 8
GATE_ROWS = 64
NEG_BIG = -1e30


def _cparams(sem):
    return pltpu.CompilerParams(dimension_semantics=sem, vmem_limit_bytes=VMEM_LIMIT)


def _dot(a, b):
    return jnp.dot(a, b, preferred_element_type=F32)


def _dot_nt(a, b):
    return lax.dot_general(a, b, (((1,), (1,)), ((), ())), preferred_element_type=F32)


def _rms(x, gain):
    return x * lax.rsqrt(jnp.mean(x * x, axis=-1, keepdims=True) + EPS) * gain


def _gelu_tanh(x):
    k0 = -2.0 * math.sqrt(2.0 / math.pi) * math.log2(math.e)
    return x / (1.0 + jnp.exp2(x * (k0 + (k0 * 0.044715) * (x * x))))


def _segment_sumsq(x, seg):
    sq = x * x
    hi = sq.astype(BF16)
    lo = (sq - hi.astype(F32)).astype(BF16)
    return _dot(hi, seg) + _dot(lo, seg)


def _rope_block(x, cos, sin, first_half):
    swapped = jnp.where(first_half, pltpu.roll(x, LANES - D_ROPE // 2, 1), pltpu.roll(x, D_ROPE // 2, 1))
    return x * cos + swapped * sin


def _proj_body(x_ref, cos_ref, sin_ref, gmix_ref, win_ref, gq_ref, wuq_ref, gkv_ref, wuk_ref, wuvt_ref,
               seg_ref, icnt_ref, qgain_ref, kgain_ref, krgain_ref,
               u_ref, q_ref, ckv_ref, kr_ref, kcat_ref, vt_ref):
    xn = _rms(x_ref[...], gmix_ref[...])
    z = _dot(xn.astype(BF16), win_ref[...])
    u_ref[...] = z[:, :SSM_WIDTH]
    cq = z[:, SSM_WIDTH:SSM_WIDTH + Q_LORA]
    ckv = z[:, SSM_WIDTH + Q_LORA:SSM_WIDTH + Q_LORA + KV_LORA]
    krc = z[:, SSM_WIDTH + Q_LORA + KV_LORA:]

    cos = cos_ref[...]
    sin = sin_ref[...]
    lane = lax.broadcasted_iota(jnp.int32, cos.shape, 1)
    first_half = lane < ROPE_LO + D_ROPE // 2
    seg = seg_ref[...]
    icnt = icnt_ref[...]

    ckvn = _rms(ckv, gkv_ref[...])
    ckv_ref[...] = ckvn
    ckvb = ckvn.astype(BF16)

    krn = krc * lax.rsqrt(jnp.sum(krc * krc, axis=-1, keepdims=True) * (1.0 / D_ROPE) + EPS) * krgain_ref[...]
    krr = _rope_block(krn, cos, sin, first_half)
    kr_ref[...] = krr[:, ROPE_LO:ROPE_LO + D_ROPE]

    qp = _dot(_rms(cq, gq_ref[...]).astype(BF16), wuq_ref[...])
    kp = _dot(ckvb, wuk_ref[...])
    qgain = qgain_ref[...]
    kgain = kgain_ref[...]
    q_blocks = []
    k_blocks = []
    for h in range(N_HEADS):
        sl = slice(h * HEAD_BLOCK, (h + 1) * HEAD_BLOCK)
        qh = qp[:, sl]
        qh = qh * lax.rsqrt(_segment_sumsq(qh, seg) * icnt + EPS) * qgain
        q_blocks.append(_rope_block(qh, cos, sin, first_half).astype(BF16))
        kh = kp[:, sl]
        kh = kh * lax.rsqrt(_segment_sumsq(kh, seg) * icnt + EPS) * kgain
        k_blocks.append((kh + krr).astype(BF16))
    q_ref[...] = jnp.concatenate(q_blocks, axis=1)
    kcat_ref[...] = jnp.concatenate(k_blocks, axis=1)

    vt = _dot_nt(wuvt_ref[...], ckvb)
    vt_ref[...] = vt.astype(BF16).reshape(N_HEADS, 1, D_V, vt.shape[1])


def _proj(x, cos, sin, w):
    n = x.shape[0]
    tm = TOK_TILE
    nt = n // tm
    cos_tiles = cos.shape[0] // tm
    full = lambda a: pl.BlockSpec(a.shape, lambda i: (0,) * a.ndim)
    tok = lambda width: pl.BlockSpec((tm, width), lambda i: (i, 0))
    tab = pl.BlockSpec((tm, LANES), lambda i: (i % cos_tiles, 0))
    consts = (w['gmix'], w['win'], w['gq'], w['wuq'], w['gkv'], w['wuk'], w['wuvt'],
              w['seg'], w['icnt'], w['qgain'], w['kgain'], w['krgain'])
    out_shape = (
        jax.ShapeDtypeStruct((n, SSM_WIDTH), F32),
        jax.ShapeDtypeStruct((n, N_HEADS * HEAD_BLOCK), BF16),
        jax.ShapeDtypeStruct((n, KV_LORA), F32),
        jax.ShapeDtypeStruct((n, D_ROPE), F32),
        jax.ShapeDtypeStruct((n, N_HEADS * HEAD_BLOCK), BF16),
        jax.ShapeDtypeStruct((N_HEADS, nt, D_V, tm), BF16),
    )
    out_specs = (tok(SSM_WIDTH), tok(N_HEADS * HEAD_BLOCK), tok(KV_LORA), tok(D_ROPE),
                 tok(N_HEADS * HEAD_BLOCK),
                 pl.BlockSpec((N_HEADS, 1, D_V, tm), lambda i: (0, i, 0, 0)))
    return pl.pallas_call(
        _proj_body,
        grid=(nt,),
        in_specs=[tok(D_MODEL), tab, tab] + [full(a) for a in consts],
        out_specs=out_specs,
        out_shape=out_shape,
        compiler_params=_cparams(("parallel",)),
        name="proj",
    )(x, cos, sin, *consts)


def _prompt_attn_body(q_ref, k_ref, vt_ref, o_ref, s_ref):
    tq = q_ref.shape[0]
    tk = KV_CHUNK
    per_chunk = tk // vt_ref.shape[3]
    heads = vt_ref.shape[0]
    qi = pl.program_id(2)

    def produce(c, hh):
        lanes = slice(hh * HEAD_BLOCK, (hh + 1) * HEAD_BLOCK)
        start = pl.multiple_of(c * tk, tk)
        s = _dot_nt(k_ref[pl.ds(start, tk), lanes], q_ref[:, lanes])
        s_ref[hh * 2 + c % 2] = s
        return jnp.max(s, axis=0, keepdims=True)

    def consume(c, hh, cmax, state, diagonal):
        m, l, acc = state
        s = s_ref[hh * 2 + c % 2]
        if diagonal:
            kpos = c * tk + lax.broadcasted_iota(jnp.int32, s.shape, 0)
            qpos = qi * tq + lax.broadcasted_iota(jnp.int32, s.shape, 1)
            s = jnp.where(kpos <= qpos, s, -jnp.inf)
            cmax = jnp.max(s, axis=0, keepdims=True)
        m_new = jnp.maximum(m, cmax)
        p = jnp.exp(s - m_new)
        alpha = jnp.exp(m - m_new)
        l = alpha * l + jnp.sum(p, axis=0, keepdims=True)
        vt = jnp.concatenate([vt_ref[hh, c * per_chunk + r] for r in range(per_chunk)], axis=1)
        acc = alpha * acc + _dot(vt, p.astype(BF16))
        return m_new, l, acc

    def body(c, carry):
        nxt = tuple(produce(c + 1, hh) for hh in range(heads))
        states = tuple(consume(c, hh, carry[0][hh], carry[1][hh], False) for hh in range(heads))
        return nxt, states

    states = tuple((jnp.full((1, tq), NEG_BIG, F32), jnp.zeros((1, tq), F32), jnp.zeros((D_V, tq), F32))
                   for _ in range(heads))
    first = tuple(produce(0, hh) for hh in range(heads))
    cmax, states = lax.fori_loop(0, qi, body, (first, states))
    for hh in range(heads):
        m, l, acc = consume(qi, hh, cmax[hh], states[hh], True)
        o_ref[hh * D_V:(hh + 1) * D_V, :] = (acc / l).astype(o_ref.dtype)


def _prompt_attention(q, kcat, vt4, batch, seq):
    tq = Q_TILE
    tile = vt4.shape[3]
    nq = seq // tq
    hs = HEADS_PER_STEP
    return pl.pallas_call(
        _prompt_attn_body,
        grid=(batch, N_HEADS // hs, nq),
        in_specs=[
            pl.BlockSpec((tq, hs * HEAD_BLOCK), lambda b, h, i: (b * nq + i, h)),
            pl.BlockSpec((seq, hs * HEAD_BLOCK), lambda b, h, i: (b, h)),
            pl.BlockSpec((hs, seq // tile, D_V, tile), lambda b, h, i: (h, b, 0, 0)),
        ],
        out_specs=pl.BlockSpec((hs * D_V, tq), lambda b, h, i: (h, b * nq + i)),
        out_shape=jax.ShapeDtypeStruct((ATT_WIDTH, batch * seq), BF16),
        scratch_shapes=[pltpu.VMEM((2 * hs, KV_CHUNK, tq), F32)],
        compiler_params=_cparams(("parallel", "parallel", "arbitrary")),
        name="prompt_attn",
    )(q, kcat, vt4)


def _sample_attn_body(pt_ref, q_ref, newlat_ref, newkrt_ref, hmask_ref, wabs_ref, rsel_ref, wukt_ref, wuv_ref,
                      *rest):
    g = PAGES_PER_STEP
    lat_refs = rest[:g]
    krt_refs = rest[g:2 * g]
    o_ref = rest[2 * g]
    qabs_sc, qr_sc, m_sc, l_sc, acc_sc = rest[2 * g + 1:]
    j = pl.program_id(1)
    t_new = q_ref.shape[1]
    rows = t_new * N_HEADS

    @pl.when(j == 0)
    def _():
        q = q_ref[0].astype(F32)
        qb = jnp.broadcast_to(q[:, None, :], (t_new, N_HEADS, q.shape[1])).reshape(rows, q.shape[1])
        qbd = (qb * hmask_ref[...]).astype(BF16)
        qabs_sc[...] = _dot(qbd, wabs_ref[...]).astype(BF16)
        qr_sc[...] = _dot(qbd, rsel_ref[...]).astype(BF16)
        m_sc[...] = jnp.full(m_sc.shape, NEG_BIG, F32)
        l_sc[...] = jnp.zeros(l_sc.shape, F32)
        acc_sc[...] = jnp.zeros(acc_sc.shape, F32)

    def attend(lat32, krt32, causal):
        lat = lat32.astype(BF16)
        tk = lat.shape[0]
        knt = _dot_nt(wukt_ref[...], lat)
        ssq = jnp.sum((knt * knt).reshape(N_HEADS, D_NOPE, tk), axis=1)
        rinv = lax.rsqrt(ssq * (1.0 / D_NOPE) + EPS)
        rinv = jnp.broadcast_to(rinv[None], (t_new, N_HEADS, tk)).reshape(rows, tk)
        s = _dot_nt(qabs_sc[...], lat) * rinv + _dot(qr_sc[...], krt32.astype(BF16))
        if causal:
            q_tok = lax.broadcasted_iota(jnp.int32, (t_new, N_HEADS, tk), 0).reshape(rows, tk)
            k_tok = lax.broadcasted_iota(jnp.int32, (rows, tk), 1)
            s = jnp.where(k_tok <= q_tok, s, -jnp.inf)
        m = m_sc[...]
        m_new = jnp.maximum(m, jnp.max(s, axis=1, keepdims=True))
        p = jnp.exp(s - m_new)
        alpha = jnp.exp(m - m_new)
        l_sc[...] = alpha * l_sc[...] + jnp.sum(p, axis=1, keepdims=True)
        acc_sc[...] = alpha * acc_sc[...] + _dot(p.astype(BF16), lat)
        m_sc[...] = m_new

    attend(jnp.concatenate([r[0, 0] for r in lat_refs], axis=0),
           jnp.concatenate([r[0, 0] for r in krt_refs], axis=1), causal=False)

    @pl.when(j == pl.num_programs(1) - 1)
    def _():
        attend(newlat_ref[0], newkrt_ref[0], causal=True)
        o_lat = acc_sc[...] / l_sc[...]
        full = _dot(o_lat.astype(BF16), wuv_ref[...])
        shape3 = (t_new, N_HEADS, ATT_WIDTH)
        col = lax.broadcasted_iota(jnp.int32, shape3, 2)
        lo = lax.broadcasted_iota(jnp.int32, shape3, 1) * D_V
        sel = jnp.where(col >= lo, jnp.where(col < lo + D_V, full.reshape(shape3), 0.0), 0.0)
        o_ref[0] = jnp.sum(sel, axis=1)


def _sample_attention(q, new_lat, new_krt, cache_lat, cache_krt, page_table, w):
    bd, t_new, _ = q.shape
    n_pages = page_table.shape[1]
    g = PAGES_PER_STEP
    steps = n_pages // g
    rows = t_new * N_HEADS
    per_b = lambda a: pl.BlockSpec((1,) + a.shape[1:], lambda b, j, pt: (b, 0, 0))
    full = lambda a: pl.BlockSpec(a.shape, lambda b, j, pt: (0,) * a.ndim)

    def page_spec(shape, k):
        return pl.BlockSpec((1, 1) + shape, lambda b, j, pt: (0, pt[b, j * g + k], 0, 0))

    consts = (w['hmask'], w['wabs'], w['rsel'], w['wukt'], w['wuv'])
    grid_spec = pltpu.PrefetchScalarGridSpec(
        num_scalar_prefetch=1,
        grid=(bd, steps),
        in_specs=[per_b(q), per_b(new_lat), per_b(new_krt)] + [full(a) for a in consts]
        + [page_spec((PAGE_SIZE, KV_LORA), k) for k in range(g)]
        + [page_spec((D_ROPE, PAGE_SIZE), k) for k in range(g)],
        out_specs=pl.BlockSpec((1, t_new, ATT_WIDTH), lambda b, j, pt: (b, 0, 0)),
        scratch_shapes=[
            pltpu.VMEM((rows, KV_LORA), BF16),
            pltpu.VMEM((rows, D_ROPE), BF16),
            pltpu.VMEM((rows, 1), F32),
            pltpu.VMEM((rows, 1), F32),
            pltpu.VMEM((rows, KV_LORA), F32),
        ],
    )
    return pl.pallas_call(
        _sample_attn_body,
        grid_spec=grid_spec,
        out_shape=jax.ShapeDtypeStruct((bd, t_new, ATT_WIDTH), F32),
        compiler_params=_cparams(("parallel", "arbitrary")),
        name="sample_attn",
    )(page_table, q, new_lat, new_krt, *consts, *([cache_lat] * g), *([cache_krt] * g))


def _s5_body(a_ref, h0_ref, u_ref, bbig_ref, cbig_ref, d_ref, y_ref, hl_ref, st_ref, h_ref,
             *, nseq, steps, pitch, group_pitch):
    groups, rows, _ = u_ref.shape
    slabs = STATE_WIDTH // LANES
    half = slabs // 2

    @pl.when(pl.program_id(0) == 0)
    def _():
        h_ref[...] = h0_ref[...]

    for g in range(groups):
        inc = _dot(u_ref[g].astype(BF16), bbig_ref[...])
        for k in range(slabs):
            st_ref[k, g * group_pitch:g * group_pitch + rows, :] = inc[:, k * LANES:(k + 1) * LANES]

    a = a_ref[...]
    ar = [a[:, k * LANES:(k + 1) * LANES] for k in range(half)]
    ai = [a[:, (half + k) * LANES:(half + k + 1) * LANES] for k in range(half)]

    def body(t, carry):
        out = []
        for k in range(half):
            hr, hi = carry[k], carry[half + k]
            sel = pl.ds(t, nseq, stride=pitch)
            nr = ar[k] * hr - ai[k] * hi + st_ref[k, sel, :]
            ni = ar[k] * hi + ai[k] * hr + st_ref[half + k, sel, :]
            st_ref[k, sel, :] = nr
            st_ref[half + k, sel, :] = ni
            out.append((nr, ni))
        return tuple(o[0] for o in out) + tuple(o[1] for o in out)

    h = h_ref[...]
    final = lax.fori_loop(0, steps, body, tuple(h[:, k * LANES:(k + 1) * LANES] for k in range(slabs)))
    h_new = jnp.concatenate(final, axis=1)
    h_ref[...] = h_new
    hl_ref[...] = h_new

    d = d_ref[...]
    for g in range(groups):
        lo = g * group_pitch
        states = jnp.concatenate([st_ref[k, lo:lo + rows, :].astype(BF16) for k in range(slabs)], axis=1)
        y_ref[g] = _dot(states, cbig_ref[...]) + d * u_ref[g]


def _s5(u3, h0, ssm, nseq, steps, pitch, group_pitch):
    a_bar, bbig, cbig, d_row = ssm
    groups, rows_total, _ = u3.shape
    rows = steps if groups == nseq else rows_total
    n_tiles = rows_total // rows
    scratch_rows = (groups - 1) * group_pitch + rows
    full = lambda arr: pl.BlockSpec(arr.shape, lambda i: (0,) * arr.ndim)
    blk = pl.BlockSpec((groups, rows, SSM_WIDTH), lambda i: (0, i, 0))
    return pl.pallas_call(
        functools.partial(_s5_body, nseq=nseq, steps=steps, pitch=pitch, group_pitch=group_pitch),
        grid=(n_tiles,),
        in_specs=[full(a_bar), full(h0), blk, full(bbig), full(cbig), full(d_row)],
        out_specs=(blk, full(h0)),
        out_shape=(jax.ShapeDtypeStruct(u3.shape, F32), jax.ShapeDtypeStruct(h0.shape, F32)),
        scratch_shapes=[pltpu.VMEM((STATE_WIDTH // LANES, scratch_rows, LANES), F32),
                        pltpu.VMEM(h0.shape, F32)],
        compiler_params=_cparams(("arbitrary",)),
        name="s5",
    )(a_bar, h0, u3, bbig, cbig, d_row)


def _s5_weights(a_re, a_im, log_dt, b, c, d_skip):
    g = SSM_GROUPS
    a_re, a_im = a_re.astype(F32), a_im.astype(F32)
    dt = jnp.exp(log_dt.astype(F32))[:, None]
    lr, li = a_re * dt, a_im * dt
    ar, ai = jnp.exp(lr) * jnp.cos(li), jnp.exp(lr) * jnp.sin(li)
    den = a_re * a_re + a_im * a_im
    fr = ((ar - 1.0) * a_re + ai * a_im) / den
    fi = (ai * a_re - (ar - 1.0) * a_im) / den
    bre, bim = b[..., 0].astype(F32), b[..., 1].astype(F32)
    bbr = fr[..., None] * bre - fi[..., None] * bim
    bbi = fr[..., None] * bim + fi[..., None] * bre
    cre, cim = c[..., 0].astype(F32), c[..., 1].astype(F32)
    eye = jnp.eye(g, dtype=F32)
    bbig = jnp.einsum('rgph,gq->ghrqp', jnp.stack([bbr, bbi]), eye).reshape(SSM_WIDTH, STATE_WIDTH)
    cbig = jnp.einsum('rgop,gq->rqpgo', jnp.stack([cre, -cim]), eye).reshape(STATE_WIDTH, SSM_WIDTH)
    a_bar = jnp.concatenate([ar.reshape(1, -1), ai.reshape(1, -1)], axis=1)
    return a_bar, bbig.astype(BF16), cbig.astype(BF16), d_skip.astype(F32).reshape(1, -1)


def _mix_body(x_ref, y_ref, o_ref, wglu_ref, wout_ref, gffn_ref, x1_ref, xn_ref, *, att_transposed):
    gl = _dot(jax.nn.gelu(y_ref[...]).astype(BF16), wglu_ref[...])
    glu = gl[:, :SSM_WIDTH] * jax.nn.sigmoid(gl[:, SSM_WIDTH:])
    o = o_ref[...]
    if att_transposed:
        o = o.astype(F32).T
    x1 = x_ref[...] + _dot(glu.astype(BF16), wout_ref[:SSM_WIDTH, :]) + _dot(o.astype(BF16), wout_ref[SSM_WIDTH:, :])
    x1_ref[...] = x1
    xn_ref[...] = _rms(x1, gffn_ref[...]).astype(BF16)


def _mix(x, y, o, w, att_transposed):
    n = x.shape[0]
    tm = TOK_TILE
    tok = lambda width: pl.BlockSpec((tm, width), lambda i: (i, 0))
    full = lambda a: pl.BlockSpec(a.shape, lambda i: (0,) * a.ndim)
    o_spec = pl.BlockSpec((ATT_WIDTH, tm), lambda i: (0, i)) if att_transposed else tok(ATT_WIDTH)
    consts = (w['wglu'], w['wout'], w['gffn'])
    return pl.pallas_call(
        functools.partial(_mix_body, att_transposed=att_transposed),
        grid=(n // tm,),
        in_specs=[tok(D_MODEL), tok(SSM_WIDTH), o_spec] + [full(a) for a in consts],
        out_specs=(tok(D_MODEL), tok(D_MODEL)),
        out_shape=(jax.ShapeDtypeStruct((n, D_MODEL), F32), jax.ShapeDtypeStruct((n, D_MODEL), BF16)),
        compiler_params=_cparams(("parallel",)),
        name="mix",
    )(x, y, o, *consts)


def _desc_top(vals_fn, count):
    tops = []
    prev = None
    for _ in range(count):
        cur = vals_fn(prev)
        tops.append(cur)
        prev = cur
    return tops


def _sort_network(n):
    pairs = []
    p = 1
    while p < n:
        k = p
        while k >= 1:
            for j in range(k % p, n - k, 2 * k):
                for i in range(min(k, n - j - k)):
                    if (i + j) // (2 * p) == (i + j + k) // (2 * p):
                        pairs.append((i + j, i + j + k))
            k //= 2
        p *= 2
    return pairs


def _top_keys(s):
    depth = PEER_KEYS // SUBLANES
    cols = [s[k * SUBLANES:(k + 1) * SUBLANES] for k in range(depth)]
    for i, j in _sort_network(depth):
        cols[i], cols[j] = jnp.maximum(cols[i], cols[j]), jnp.minimum(cols[i], cols[j])
    tops = []
    for t in range(PEER_TOPK):
        top = jnp.max(cols[0], axis=0, keepdims=True)
        tops.append(top)
        popped = cols[0] == top
        for k in range(PEER_TOPK - 1 - t):
            cols[k] = jnp.where(popped, cols[k + 1], cols[k])
    return tops


def _peer_route_body(xn_ref, wq_ref, keys_ref, rank_ref, e2_ref, lim_ref, cc_ref):
    xn = xn_ref[...]
    q = _dot(xn, wq_ref[...]).astype(BF16)
    k1 = keys_ref[0]
    k2 = keys_ref[1]
    tops1, tops2, s1_all, s2_all = [], [], [], []
    for h in range(PEER_HEADS):
        base = h * 2 * PEER_HALF
        s1 = _dot_nt(k1, q[:, base:base + PEER_HALF])
        s2 = _dot_nt(k2, q[:, base + PEER_HALF:base + 2 * PEER_HALF])
        s1_all.append(s1)
        s2_all.append(s2)
        tops1.append(_top_keys(s1))
        tops2.append(_top_keys(s2))
    a = [jnp.concatenate([tops1[h][r] for h in range(PEER_HEADS)], axis=0) for r in range(PEER_TOPK)]
    b = [jnp.concatenate([tops2[h][r] for h in range(PEER_HEADS)], axis=0) for r in range(PEER_TOPK)]
    cands = [a[i] + b[j] for i in range(PEER_TOPK) for j in range(PEER_TOPK) if (i + 1) * (j + 1) <= PEER_TOPK]

    def nxt_sum(prev):
        best = None
        for cnd in cands:
            v = cnd if prev is None else jnp.where(cnd < prev, cnd, -jnp.inf)
            best = v if best is None else jnp.maximum(best, v)
        return best

    tsum = _desc_top(nxt_sum, PEER_TOPK)
    tau = tsum[-1]
    zsum = jnp.zeros_like(tau)
    for t in tsum:
        zsum = zsum + jnp.exp(t - tsum[0])
    inv_z = 1.0 / zsum
    count = []
    for i in range(PEER_TOPK):
        cnt = jnp.zeros_like(tau)
        for j in range(PEER_TOPK):
            if (i + 1) * (j + 1) <= PEER_TOPK:
                cnt = cnt + jnp.where(a[i] + b[j] >= tau, 1.0, 0.0)
        count.append(cnt)
    for h in range(PEER_HEADS):
        row = slice(h, h + 1)
        lim = jnp.zeros(s1_all[h].shape, F32)
        rank = jnp.full(s2_all[h].shape, float(PEER_KEYS), F32)
        for r in range(PEER_TOPK):
            lim = jnp.where(s1_all[h] == a[r][row], count[r][row], lim)
            rank = jnp.where(s2_all[h] == b[r][row], float(r), rank)
        lim_ref[h] = lim
        rank_ref[h] = rank
        cc_ref[h] = jnp.exp(s1_all[h] - a[0][row]) * inv_z[row]
        e2_ref[h] = jnp.exp(s2_all[h] - b[0][row])


def _peer_route(xn, w):
    n = xn.shape[0]
    tm = TOK_TILE
    spec = pl.BlockSpec((PEER_HEADS, PEER_KEYS, tm), lambda i: (0, 0, i))
    shp = jax.ShapeDtypeStruct((PEER_HEADS, PEER_KEYS, n), F32)
    return pl.pallas_call(
        _peer_route_body,
        grid=(n // tm,),
        in_specs=[pl.BlockSpec((tm, D_MODEL), lambda i: (i, 0)),
                  pl.BlockSpec(w['peer_wq'].shape, lambda i: (0, 0)),
                  pl.BlockSpec(w['peer_keys'].shape, lambda i: (0, 0, 0))],
        out_specs=(spec, spec, spec, spec),
        out_shape=(shp, shp, shp, shp),
        compiler_params=_cparams(("parallel",)),
        name="peer_route",
    )(xn, w['peer_wq'], w['peer_keys'])


def _peer_expert_body(xn_ref, x1_ref, u_ref, vt_ref, rank_ref, e2_ref, lim_ref, cc_ref, o_ref, acc_ref, act_ref,
                      ga_ref):
    j = pl.program_id(1)
    te = u_ref.shape[0]
    tm = xn_ref.shape[0]
    nsub = te // PEER_KEYS
    tiles = GATE_ROWS // BF16_ROWS

    @pl.when(j == 0)
    def _():
        acc_ref[...] = jnp.zeros(acc_ref.shape, F32)

    act_ref[...] = _gelu_tanh(_dot_nt(u_ref[...], xn_ref[...]))
    zero = jnp.zeros((), BF16)
    first = pl.multiple_of(j * nsub, SUBLANES)
    for c in range(tm // LANES):
        ls = slice(c * LANES, (c + 1) * LANES)
        for grp in range(nsub // GATE_GROUP):
            for lo in range(0, PEER_KEYS, GATE_ROWS):
                ks = slice(lo, lo + GATE_ROWS)
                def add_head(h, gates, ks=ks, ls=ls, grp=grp):
                    rank = rank_ref[h, ks, ls].astype(BF16).reshape(tiles, BF16_ROWS, LANES)
                    e2 = e2_ref[h, ks, ls].astype(BF16).reshape(tiles, BF16_ROWS, LANES)
                    lim8 = lim_ref[h, pl.ds(first, nsub), ls]
                    cc8 = cc_ref[h, pl.ds(first, nsub), ls]
                    out = []
                    for s in range(GATE_GROUP):
                        sub = grp * GATE_GROUP + s
                        lim = jnp.broadcast_to(lim8[sub:sub + 1], (BF16_ROWS, LANES)).astype(BF16)
                        cc = jnp.broadcast_to(cc8[sub:sub + 1], (BF16_ROWS, LANES)).astype(BF16)
                        out.append(gates[s] + jnp.where(rank < lim[None], e2 * cc[None], zero))
                    return tuple(out)

                init = tuple(jnp.zeros((tiles, BF16_ROWS, LANES), BF16) for _ in range(GATE_GROUP))
                gates = lax.fori_loop(0, PEER_HEADS, add_head, init)
                for s in range(GATE_GROUP):
                    rs = slice((grp * GATE_GROUP + s) * PEER_KEYS + lo, (grp * GATE_GROUP + s) * PEER_KEYS + lo + GATE_ROWS)
                    ga_ref[rs, ls] = gates[s].reshape(GATE_ROWS, LANES) * act_ref[rs, ls].astype(BF16)
    acc_ref[...] += _dot(vt_ref[0], ga_ref[...])

    @pl.when(j == pl.num_programs(1) - 1)
    def _():
        o_ref[...] = x1_ref[...] + acc_ref[...].T


def _peer_experts(xn, x1, routes, w):
    n = xn.shape[0]
    tm = PEER_TOK_TILE
    te = PEER_EXP_TILE
    n_exp = w['peer_u'].shape[0]
    tok = pl.BlockSpec((tm, D_MODEL), lambda i, j: (i, 0))
    route = lambda a: pl.BlockSpec((PEER_HEADS, a.shape[1], tm), lambda i, j: (0, 0, i))
    return pl.pallas_call(
        _peer_expert_body,
        grid=(n // tm, n_exp // te),
        in_specs=[tok, tok,
                  pl.BlockSpec((te, D_MODEL), lambda i, j: (j, 0)),
                  pl.BlockSpec((1, D_MODEL, te), lambda i, j: (j, 0, 0)),
                  ] + [route(a) for a in routes],
        out_specs=tok,
        out_shape=jax.ShapeDtypeStruct((n, D_MODEL), F32),
        scratch_shapes=[pltpu.VMEM((D_MODEL, tm), F32), pltpu.VMEM((te, tm), F32), pltpu.VMEM((te, tm), BF16)],
        compiler_params=_cparams(("parallel", "arbitrary")),
        name="peer_experts",
    )(xn, x1, w['peer_u'], w['peer_vt'], *routes)


def _rope_tables(pos):
    inv = ROPE_THETA ** (-jnp.arange(0, D_ROPE, 2, dtype=F32) / D_ROPE)
    ang = pos.astype(F32)[:, None] * inv[None, :]
    cos, sin = jnp.cos(ang), jnp.sin(ang)
    t = pos.shape[0]
    ones = jnp.ones((t, ROPE_LO), F32)
    tail = HEAD_BLOCK - ROPE_LO - D_ROPE
    cos_t = jnp.concatenate([ones, cos, cos, jnp.ones((t, tail), F32)], axis=1)
    sin_t = jnp.concatenate([0 * ones, -sin, sin, jnp.zeros((t, tail), F32)], axis=1)
    return cos_t, sin_t


def _head_blocks(wmat, lo, width):
    pad = jnp.zeros(wmat.shape[:2] + (HEAD_BLOCK,), wmat.dtype)
    return pad.at[:, :, lo:lo + width].set(wmat).reshape(wmat.shape[0], N_HEADS * HEAD_BLOCK)


def _prep_weights(norm_mix, w_in, norm_q_lora, w_uq, norm_kv_lora, w_uk, w_uv, g_qn, g_qr, g_kn, g_kr,
                  w_glu, w_out, norm_ffn, peer_wq, peer_keys, peer_u, peer_v):
    row = lambda v: v.astype(F32).reshape(1, -1)
    kr_cols = jnp.zeros((D_MODEL, LANES), F32).at[:, ROPE_LO:ROPE_LO + D_ROPE].set(w_in[:, IN_WIDTH - D_ROPE:])
    win = jnp.concatenate([w_in[:, :IN_WIDTH - D_ROPE], kr_cols], axis=1)
    wuq = _head_blocks(w_uq.reshape(Q_LORA, N_HEADS, D_NOPE + D_ROPE), 0, D_NOPE + D_ROPE)
    wuk = _head_blocks(w_uk, 0, D_NOPE)
    wuvt = w_uv.reshape(KV_LORA, ATT_WIDTH).T
    lane = jnp.arange(HEAD_BLOCK)
    is_nope = lane < ROPE_LO
    is_rope = (lane >= ROPE_LO) & (lane < ROPE_LO + D_ROPE)
    seg = ((is_nope[:, None] & is_nope[None, :]) | (is_rope[:, None] & is_rope[None, :])).astype(BF16)
    icnt = jnp.where(is_nope, 1.0 / D_NOPE, jnp.where(is_rope, 1.0 / D_ROPE, 1.0)).astype(F32).reshape(1, -1)
    blockvec = lambda a, bvec: jnp.zeros((HEAD_BLOCK,), F32).at[:ROPE_LO].set(a).at[ROPE_LO:ROPE_LO + D_ROPE].set(bvec)
    qgain = (blockvec(g_qn, g_qr) * ATTN_SCALE).reshape(1, -1)
    kgain = blockvec(g_kn, jnp.zeros((D_ROPE,), F32)).reshape(1, -1)
    krgain = blockvec(jnp.zeros((D_NOPE,), F32), g_kr).reshape(1, -1)
    wabs = jnp.einsum('chd,d->hdc', w_uk, g_kn)
    wabs = jnp.zeros((N_HEADS, HEAD_BLOCK, KV_LORA), F32).at[:, :D_NOPE, :].set(wabs)
    wabs = wabs.reshape(N_HEADS * HEAD_BLOCK, KV_LORA)
    rsel = jnp.zeros((N_HEADS, HEAD_BLOCK, D_ROPE), F32).at[:, ROPE_LO:ROPE_LO + D_ROPE, :].set(
        jnp.broadcast_to(jnp.eye(D_ROPE, dtype=F32), (N_HEADS, D_ROPE, D_ROPE)))
    rsel = rsel.reshape(N_HEADS * HEAD_BLOCK, D_ROPE)
    head_of_lane = jnp.arange(N_HEADS * HEAD_BLOCK) // HEAD_BLOCK
    hmask8 = (head_of_lane[None, :] == jnp.arange(N_HEADS)[:, None]).astype(F32)
    return {
        'gmix': row(norm_mix), 'gq': row(norm_q_lora), 'gkv': row(norm_kv_lora),
        'seg': seg, 'icnt': icnt, 'qgain': qgain, 'kgain': kgain, 'krgain': krgain,
        'win': win.astype(BF16), 'wuq': wuq.astype(BF16), 'wuk': wuk.astype(BF16), 'wuvt': wuvt.astype(BF16),
        'hmask8': hmask8, 'wabs': wabs.astype(BF16), 'rsel': rsel.astype(BF16),
        'wukt': w_uk.reshape(KV_LORA, N_HEADS * D_NOPE).T.astype(BF16),
        'wuv': w_uv.reshape(KV_LORA, ATT_WIDTH).astype(BF16),
        'wglu': w_glu.astype(BF16), 'wout': w_out.astype(BF16), 'gffn': row(norm_ffn),
        'peer_wq': peer_wq.astype(BF16), 'peer_keys': peer_keys.astype(BF16),
        'peer_u': peer_u.astype(BF16),
        'peer_vt': peer_v.astype(BF16).reshape(-1, PEER_EXP_TILE, D_MODEL).transpose(0, 2, 1),
    }


def _peer(xn, x1, w):
    return _peer_experts(xn, x1, _peer_route(xn, w), w)


def _state_in(st):
    return jnp.moveaxis(st.astype(F32), -1, 1).reshape(st.shape[0], STATE_WIDTH)


def _state_out(h):
    return jnp.moveaxis(h.reshape(h.shape[0], 2, SSM_GROUPS, SSM_STATE), 1, -1)


def kernel(x_prompt, x_sample, cache_kv_latent, cache_k_rope, state_ssm, page_table, norm_mix, w_in, norm_q_lora, w_uq, norm_kv_lora, w_uk, w_uv, qk_gain_q_nope, qk_gain_q_rope, qk_gain_k_nope, qk_gain_k_rope, ssm_a_re, ssm_a_im, ssm_log_dt, ssm_b, ssm_c, ssm_d, w_glu, w_out, norm_ffn, peer_wq, peer_keys, peer_u, peer_v):
    depth = norm_mix.shape[0]
    batch, seq, _ = x_prompt.shape
    dec_batch, dec_seq, _ = x_sample.shape
    n_dec = dec_batch * dec_seq
    past = page_table.shape[1] * PAGE_SIZE
    assert seq % Q_TILE == 0 and Q_TILE == KV_CHUNK and seq % S5_TIME_TILE == 0 and n_dec % PEER_TOK_TILE == 0
    assert page_table.shape[1] % PAGES_PER_STEP == 0 and TOK_TILE % dec_seq == 0 and dec_seq <= PAGE_SIZE

    cos_p, sin_p = _rope_tables(jnp.arange(seq))
    cos_s, sin_s = _rope_tables(past + jnp.arange(dec_seq))
    cos_s, sin_s = jnp.tile(cos_s, (TOK_TILE // dec_seq, 1)), jnp.tile(sin_s, (TOK_TILE // dec_seq, 1))
    cache_krt = jnp.swapaxes(cache_k_rope, 2, 3)

    xp = x_prompt.reshape(batch * seq, D_MODEL)
    xs = x_sample.reshape(n_dec, D_MODEL)
    outs = {k: [] for k in ('lat_p', 'kr_p', 'ssm_p', 'lat_s', 'kr_s', 'ssm_s')}
    for l in range(depth):
        w = _prep_weights(norm_mix[l], w_in[l], norm_q_lora[l], w_uq[l], norm_kv_lora[l], w_uk[l], w_uv[l],
                          qk_gain_q_nope[l], qk_gain_q_rope[l], qk_gain_k_nope[l], qk_gain_k_rope[l],
                          w_glu[l], w_out[l], norm_ffn[l], peer_wq[l], peer_keys[l], peer_u[l], peer_v[l])
        w['hmask'] = jnp.tile(w['hmask8'], (dec_seq, 1))
        ssm = _s5_weights(ssm_a_re[l], ssm_a_im[l], ssm_log_dt[l], ssm_b[l], ssm_c[l], ssm_d[l])

        u, q, ckv, kr, kcat, vt4 = _proj(xp, cos_p, sin_p, w)
        o_t = _prompt_attention(q, kcat, vt4, batch, seq)
        pitch = S5_TIME_TILE + S5_ROW_SKEW
        y, h_last = _s5(u.reshape(batch, seq, SSM_WIDTH), jnp.zeros((batch, STATE_WIDTH), F32), ssm,
                        nseq=batch, steps=S5_TIME_TILE, pitch=pitch, group_pitch=pitch)
        x1, xn = _mix(xp, y.reshape(batch * seq, SSM_WIDTH), o_t, w, att_transposed=True)
        xp = _peer(xn, x1, w)
        outs['lat_p'].append(ckv.reshape(batch, seq, KV_LORA))
        outs['kr_p'].append(kr.reshape(batch, seq, D_ROPE))
        outs['ssm_p'].append(_state_out(h_last))

        u, q, ckv, kr, _, _ = _proj(xs, cos_s, sin_s, w)
        pad_rows = ((0, 0), (0, PAGE_SIZE - dec_seq), (0, 0))
        new_lat = jnp.pad(ckv.reshape(dec_batch, dec_seq, KV_LORA), pad_rows)
        new_krt = jnp.swapaxes(jnp.pad(kr.reshape(dec_batch, dec_seq, D_ROPE), pad_rows), 1, 2)
        o = _sample_attention(q.reshape(dec_batch, dec_seq, -1), new_lat, new_krt,
                              cache_kv_latent[l:l + 1], cache_krt[l:l + 1], page_table, w)
        y, h_last = _s5(u.reshape(1, n_dec, SSM_WIDTH), _state_in(state_ssm[l]), ssm,
                        nseq=dec_batch, steps=dec_seq, pitch=dec_seq, group_pitch=n_dec)
        x1, xn = _mix(xs, y.reshape(n_dec, SSM_WIDTH), o.reshape(n_dec, ATT_WIDTH), w, att_transposed=False)
        xs = _peer(xn, x1, w)
        outs['lat_s'].append(ckv.reshape(dec_batch, dec_seq, KV_LORA))
        outs['kr_s'].append(kr.reshape(dec_batch, dec_seq, D_ROPE))
        outs['ssm_s'].append(_state_out(h_last))

    return (xp.reshape(batch, seq, D_MODEL), xs.reshape(dec_batch, dec_seq, D_MODEL),
            jnp.stack(outs['lat_p']), jnp.stack(outs['kr_p']), jnp.stack(outs['ssm_p']),
            jnp.stack(outs['lat_s']), jnp.stack(outs['kr_s']), jnp.stack(outs['ssm_s']))
```

```python
import functools
import math

import jax
import jax.numpy as jnp
from jax import lax
from jax.experimental import pallas as pl
from jax.experimental.pallas import tpu as pltpu

F32 = jnp.float32
BF16 = jnp.bfloat16

D_MODEL = 1024
SSM_WIDTH = 512
SSM_GROUP = 16
SSM_GROUPS = 32
SSM_STATE = 64
STATE_HALF = SSM_GROUPS * SSM_STATE
STATE_WIDTH = 2 * STATE_HALF
D_NOPE = 64
D_ROPE = 32
D_V = 64
N_HEADS = 8
ATT_WIDTH = N_HEADS * D_V
Q_LORA = 384
KV_LORA = 256
IN_WIDTH = SSM_WIDTH + Q_LORA + KV_LORA + D_ROPE
ROPE_THETA = 10000.0
ATTN_SCALE = 1.0 / math.sqrt(D_NOPE + D_ROPE)
PAGE_SIZE = 128
PEER_HEADS = 8
PEER_KEYS = 128
PEER_HALF = 128
PEER_TOPK = 16
EPS = 1e-6

LANES = 128
SUBLANES = 8
BF16_ROWS = 2 * SUBLANES
HEAD_BLOCK = LANES
ROPE_LO = D_NOPE
VMEM_LIMIT = 48 * 1024 * 1024

TOK_TILE = 256
KV_CHUNK = 2 * TOK_TILE
Q_TILE = 512
HEADS_PER_STEP = 2
S5_TIME_TILE = 128
S5_ROW_SKEW = 4
PAGES_PER_STEP = 64
PEER_TOK_TILE = 512
PEER_EXP_TILE = SUBLANES * PEER_KEYS
GATE_GROUP = 8
GATE_ROWS = 64
NEG_BIG = -1e30


def _cparams(sem):
    return pltpu.CompilerParams(dimension_semantics=sem, vmem_limit_bytes=VMEM_LIMIT)


def _dot(a, b):
    return jnp.dot(a, b, preferred_element_type=F32)


def _dot_nt(a, b):
    return lax.dot_general(a, b, (((1,), (1,)), ((), ())), preferred_element_type=F32)


def _rms(x, gain):
    return x * lax.rsqrt(jnp.mean(x * x, axis=-1, keepdims=True) + EPS) * gain


def _gelu_tanh(x):
    k0 = -2.0 * math.sqrt(2.0 / math.pi) * math.log2(math.e)
    return x / (1.0 + jnp.exp2(x * (k0 + (k0 * 0.044715) * (x * x))))


def _segment_sumsq(x, seg):
    sq = x * x
    hi = sq.astype(BF16)
    lo = (sq - hi.astype(F32)).astype(BF16)
    return _dot(hi, seg) + _dot(lo, seg)


def _rope_block(x, cos, sin, first_half):
    swapped = jnp.where(first_half, pltpu.roll(x, LANES - D_ROPE // 2, 1), pltpu.roll(x, D_ROPE // 2, 1))
    return x * cos + swapped * sin


def _proj_body(x_ref, cos_ref, sin_ref, gmix_ref, win_ref, gq_ref, wuq_ref, gkv_ref, wuk_ref, wuvt_ref,
               seg_ref, icnt_ref, qgain_ref, kgain_ref, krgain_ref,
               u_ref, q_ref, ckv_ref, kr_ref, kcat_ref, vt_ref):
    xn = _rms(x_ref[...], gmix_ref[...])
    z = _dot(xn.astype(BF16), win_ref[...])
    u_ref[...] = z[:, :SSM_WIDTH]
    cq = z[:, SSM_WIDTH:SSM_WIDTH + Q_LORA]
    ckv = z[:, SSM_WIDTH + Q_LORA:SSM_WIDTH + Q_LORA + KV_LORA]
    krc = z[:, SSM_WIDTH + Q_LORA + KV_LORA:]

    cos = cos_ref[...]
    sin = sin_ref[...]
    lane = lax.broadcasted_iota(jnp.int32, cos.shape, 1)
    first_half = lane < ROPE_LO + D_ROPE // 2
    seg = seg_ref[...]
    icnt = icnt_ref[...]

    ckvn = _rms(ckv, gkv_ref[...])
    ckv_ref[...] = ckvn
    ckvb = ckvn.astype(BF16)

    krn = krc * lax.rsqrt(jnp.sum(krc * krc, axis=-1, keepdims=True) * (1.0 / D_ROPE) + EPS) * krgain_ref[...]
    krr = _rope_block(krn, cos, sin, first_half)
    kr_ref[...] = krr[:, ROPE_LO:ROPE_LO + D_ROPE]

    qp = _dot(_rms(cq, gq_ref[...]).astype(BF16), wuq_ref[...])
    kp = _dot(ckvb, wuk_ref[...])
    qgain = qgain_ref[...]
    kgain = kgain_ref[...]
    q_blocks = []
    k_blocks = []
    for h in range(N_HEADS):
        sl = slice(h * HEAD_BLOCK, (h + 1) * HEAD_BLOCK)
        qh = qp[:, sl]
        qh = qh * lax.rsqrt(_segment_sumsq(qh, seg) * icnt + EPS) * qgain
        q_blocks.append(_rope_block(qh, cos, sin, first_half).astype(BF16))
        kh = kp[:, sl]
        kh = kh * lax.rsqrt(_segment_sumsq(kh, seg) * icnt + EPS) * kgain
        k_blocks.append((kh + krr).astype(BF16))
    q_ref[...] = jnp.concatenate(q_blocks, axis=1)
    kcat_ref[...] = jnp.concatenate(k_blocks, axis=1)

    vt = _dot_nt(wuvt_ref[...], ckvb)
    vt_ref[...] = vt.astype(BF16).reshape(N_HEADS, 1, D_V, vt.shape[1])


def _proj(x, cos, sin, w):
    n = x.shape[0]
    tm = TOK_TILE
    nt = n // tm
    cos_tiles = cos.shape[0] // tm
    full = lambda a: pl.BlockSpec(a.shape, lambda i: (0,) * a.ndim)
    tok = lambda width: pl.BlockSpec((tm, width), lambda i: (i, 0))
    tab = pl.BlockSpec((tm, LANES), lambda i: (i % cos_tiles, 0))
    consts = (w['gmix'], w['win'], w['gq'], w['wuq'], w['gkv'], w['wuk'], w['wuvt'],
              w['seg'], w['icnt'], w['qgain'], w['kgain'], w['krgain'])
    out_shape = (
        jax.ShapeDtypeStruct((n, SSM_WIDTH), F32),
        jax.ShapeDtypeStruct((n, N_HEADS * HEAD_BLOCK), BF16),
        jax.ShapeDtypeStruct((n, KV_LORA), F32),
        jax.ShapeDtypeStruct((n, D_ROPE), F32),
        jax.ShapeDtypeStruct((n, N_HEADS * HEAD_BLOCK), BF16),
        jax.ShapeDtypeStruct((N_HEADS, nt, D_V, tm), BF16),
    )
    out_specs = (tok(SSM_WIDTH), tok(N_HEADS * HEAD_BLOCK), tok(KV_LORA), tok(D_ROPE),
                 tok(N_HEADS * HEAD_BLOCK),
                 pl.BlockSpec((N_HEADS, 1, D_V, tm), lambda i: (0, i, 0, 0)))
    return pl.pallas_call(
        _proj_body,
        grid=(nt,),
        in_specs=[tok(D_MODEL), tab, tab] + [full(a) for a in consts],
        out_specs=out_specs,
        out_shape=out_shape,
        compiler_params=_cparams(("parallel",)),
        name="proj",
    )(x, cos, sin, *consts)


def _prompt_attn_body(q_ref, k_ref, vt_ref, o_ref, s_ref):
    tq = q_ref.shape[0]
    tk = KV_CHUNK
    per_chunk = tk // vt_ref.shape[3]
    heads = vt_ref.shape[0]
    qi = pl.program_id(2)

    def produce(c, hh):
        lanes = slice(hh * HEAD_BLOCK, (hh + 1) * HEAD_BLOCK)
        start = pl.multiple_of(c * tk, tk)
        s = _dot_nt(k_ref[pl.ds(start, tk), lanes], q_ref[:, lanes])
        s_ref[hh * 2 + c % 2] = s
        return jnp.max(s, axis=0, keepdims=True)

    def consume(c, hh, cmax, state, diagonal):
        m, l, acc = state
        s = s_ref[hh * 2 + c % 2]
        if diagonal:
            kpos = c * tk + lax.broadcasted_iota(jnp.int32, s.shape, 0)
            qpos = qi * tq + lax.broadcasted_iota(jnp.int32, s.shape, 1)
            s = jnp.where(kpos <= qpos, s, -jnp.inf)
            cmax = jnp.max(s, axis=0, keepdims=True)
        m_new = jnp.maximum(m, cmax)
        p = jnp.exp(s - m_new)
        alpha = jnp.exp(m - m_new)
        l = alpha * l + jnp.sum(p, axis=0, keepdims=True)
        vt = jnp.concatenate([vt_ref[hh, c * per_chunk + r] for r in range(per_chunk)], axis=1)
        acc = alpha * acc + _dot(vt, p.astype(BF16))
        return m_new, l, acc

    def body(c, carry):
        nxt = tuple(produce(c + 1, hh) for hh in range(heads))
        states = tuple(consume(c, hh, carry[0][hh], carry[1][hh], False) for hh in range(heads))
        return nxt, states

    states = tuple((jnp.full((1, tq), NEG_BIG, F32), jnp.zeros((1, tq), F32), jnp.zeros((D_V, tq), F32))
                   for _ in range(heads))
    first = tuple(produce(0, hh) for hh in range(heads))
    cmax, states = lax.fori_loop(0, qi, body, (first, states))
    for hh in range(heads):
        m, l, acc = consume(qi, hh, cmax[hh], states[hh], True)
        o_ref[hh * D_V:(hh + 1) * D_V, :] = (acc / l).astype(o_ref.dtype)


def _prompt_attention(q, kcat, vt4, batch, seq):
    tq = Q_TILE
    tile = vt4.shape[3]
    nq = seq // tq
    hs = HEADS_PER_STEP
    return pl.pallas_call(
        _prompt_attn_body,
        grid=(batch, N_HEADS // hs, nq),
        in_specs=[
            pl.BlockSpec((tq, hs * HEAD_BLOCK), lambda b, h, i: (b * nq + i, h)),
            pl.BlockSpec((seq, hs * HEAD_BLOCK), lambda b, h, i: (b, h)),
            pl.BlockSpec((hs, seq // tile, D_V, tile), lambda b, h, i: (h, b, 0, 0)),
        ],
        out_specs=pl.BlockSpec((hs * D_V, tq), lambda b, h, i: (h, b * nq + i)),
        out_shape=jax.ShapeDtypeStruct((ATT_WIDTH, batch * seq), BF16),
        scratch_shapes=[pltpu.VMEM((2 * hs, KV_CHUNK, tq), F32)],
        compiler_params=_cparams(("parallel", "parallel", "arbitrary")),
        name="prompt_attn",
    )(q, kcat, vt4)


def _sample_attn_body(pt_ref, q_ref, newlat_ref, newkrt_ref, hmask_ref, wabs_ref, rsel_ref, wukt_ref, wuv_ref,
                      *rest):
    g = PAGES_PER_STEP
    lat_refs = rest[:g]
    krt_refs = rest[g:2 * g]
    o_ref = rest[2 * g]
    qabs_sc, qr_sc, m_sc, l_sc, acc_sc = rest[2 * g + 1:]
    j = pl.program_id(1)
    t_new = q_ref.shape[1]
    rows = t_new * N_HEADS

    @pl.when(j == 0)
    def _():
        q = q_ref[0].astype(F32)
        qb = jnp.broadcast_to(q[:, None, :], (t_new, N_HEADS, q.shape[1])).reshape(rows, q.shape[1])
        qbd = (qb * hmask_ref[...]).astype(BF16)
        qabs_sc[...] = _dot(qbd, wabs_ref[...]).astype(BF16)
        qr_sc[...] = _dot(qbd, rsel_ref[...]).astype(BF16)
        m_sc[...] = jnp.full(m_sc.shape, NEG_BIG, F32)
        l_sc[...] = jnp.zeros(l_sc.shape, F32)
        acc_sc[...] = jnp.zeros(acc_sc.shape, F32)

    def attend(lat32, krt32, causal):
        lat = lat32.astype(BF16)
        tk = lat.shape[0]
        knt = _dot_nt(wukt_ref[...], lat)
        ssq = jnp.sum((knt * knt).reshape(N_HEADS, D_NOPE, tk), axis=1)
        rinv = lax.rsqrt(ssq * (1.0 / D_NOPE) + EPS)
        rinv = jnp.broadcast_to(rinv[None], (t_new, N_HEADS, tk)).reshape(rows, tk)
        s = _dot_nt(qabs_sc[...], lat) * rinv + _dot(qr_sc[...], krt32.astype(BF16))
        if causal:
            q_tok = lax.broadcasted_iota(jnp.int32, (t_new, N_HEADS, tk), 0).reshape(rows, tk)
            k_tok = lax.broadcasted_iota(jnp.int32, (rows, tk), 1)
            s = jnp.where(k_tok <= q_tok, s, -jnp.inf)
        m = m_sc[...]
        m_new = jnp.maximum(m, jnp.max(s, axis=1, keepdims=True))
        p = jnp.exp(s - m_new)
        alpha = jnp.exp(m - m_new)
        l_sc[...] = alpha * l_sc[...] + jnp.sum(p, axis=1, keepdims=True)
        acc_sc[...] = alpha * acc_sc[...] + _dot(p.astype(BF16), lat)
        m_sc[...] = m_new

    attend(jnp.concatenate([r[0, 0] for r in lat_refs], axis=0),
           jnp.concatenate([r[0, 0] for r in krt_refs], axis=1), causal=False)

    @pl.when(j == pl.num_programs(1) - 1)
    def _():
        attend(newlat_ref[0], newkrt_ref[0], causal=True)
        o_lat = acc_sc[...] / l_sc[...]
        full = _dot(o_lat.astype(BF16), wuv_ref[...])
        shape3 = (t_new, N_HEADS, ATT_WIDTH)
        col = lax.broadcasted_iota(jnp.int32, shape3, 2)
        lo = lax.broadcasted_iota(jnp.int32, shape3, 1) * D_V
        sel = jnp.where(col >= lo, jnp.where(col < lo + D_V, full.reshape(shape3), 0.0), 0.0)
        o_ref[0] = jnp.sum(sel, axis=1)


def _sample_attention(q, new_lat, new_krt, cache_lat, cache_krt, page_table, w):
    bd, t_new, _ = q.shape
    n_pages = page_table.shape[1]
    g = PAGES_PER_STEP
    steps = n_pages // g
    rows = t_new * N_HEADS
    per_b = lambda a: pl.BlockSpec((1,) + a.shape[1:], lambda b, j, pt: (b, 0, 0))
    full = lambda a: pl.BlockSpec(a.shape, lambda b, j, pt: (0,) * a.ndim)

    def page_spec(shape, k):
        return pl.BlockSpec((1, 1) + shape, lambda b, j, pt: (0, pt[b, j * g + k], 0, 0))

    consts = (w['hmask'], w['wabs'], w['rsel'], w['wukt'], w['wuv'])
    grid_spec = pltpu.PrefetchScalarGridSpec(
        num_scalar_prefetch=1,
        grid=(bd, steps),
        in_specs=[per_b(q), per_b(new_lat), per_b(new_krt)] + [full(a) for a in consts]
        + [page_spec((PAGE_SIZE, KV_LORA), k) for k in range(g)]
        + [page_spec((D_ROPE, PAGE_SIZE), k) for k in range(g)],
        out_specs=pl.BlockSpec((1, t_new, ATT_WIDTH), lambda b, j, pt: (b, 0, 0)),
        scratch_shapes=[
            pltpu.VMEM((rows, KV_LORA), BF16),
            pltpu.VMEM((rows, D_ROPE), BF16),
            pltpu.VMEM((rows, 1), F32),
            pltpu.VMEM((rows, 1), F32),
            pltpu.VMEM((rows, KV_LORA), F32),
        ],
    )
    return pl.pallas_call(
        _sample_attn_body,
        grid_spec=grid_spec,
        out_shape=jax.ShapeDtypeStruct((bd, t_new, ATT_WIDTH), F32),
        compiler_params=_cparams(("parallel", "arbitrary")),
        name="sample_attn",
    )(page_table, q, new_lat, new_krt, *consts, *([cache_lat] * g), *([cache_krt] * g))


def _s5_body(a_ref, h0_ref, u_ref, bbig_ref, cbig_ref, d_ref, y_ref, hl_ref, st_ref, h_ref,
             *, nseq, steps, pitch, group_pitch):
    groups, rows, _ = u_ref.shape
    slabs = STATE_WIDTH // LANES
    half = slabs // 2

    @pl.when(pl.program_id(0) == 0)
    def _():
        h_ref[...] = h0_ref[...]

    for g in range(groups):
        inc = _dot(u_ref[g].astype(BF16), bbig_ref[...])
        for k in range(slabs):
            st_ref[k, g * group_pitch:g * group_pitch + rows, :] = inc[:, k * LANES:(k + 1) * LANES]

    a = a_ref[...]
    ar = [a[:, k * LANES:(k + 1) * LANES] for k in range(half)]
    ai = [a[:, (half + k) * LANES:(half + k + 1) * LANES] for k in range(half)]

    def body(t, carry):
        out = []
        for k in range(half):
            hr, hi = carry[k], carry[half + k]
            sel = pl.ds(t, nseq, stride=pitch)
            nr = ar[k] * hr - ai[k] * hi + st_ref[k, sel, :]
            ni = ar[k] * hi + ai[k] * hr + st_ref[half + k, sel, :]
            st_ref[k, sel, :] = nr
            st_ref[half + k, sel, :] = ni
            out.append((nr, ni))
        return tuple(o[0] for o in out) + tuple(o[1] for o in out)

    h = h_ref[...]
    final = lax.fori_loop(0, steps, body, tuple(h[:, k * LANES:(k + 1) * LANES] for k in range(slabs)))
    h_new = jnp.concatenate(final, axis=1)
    h_ref[...] = h_new
    hl_ref[...] = h_new

    d = d_ref[...]
    for g in range(groups):
        lo = g * group_pitch
        states = jnp.concatenate([st_ref[k, lo:lo + rows, :].astype(BF16) for k in range(slabs)], axis=1)
        y_ref[g] = _dot(states, cbig_ref[...]) + d * u_ref[g]


def _s5(u3, h0, ssm, nseq, steps, pitch, group_pitch):
    a_bar, bbig, cbig, d_row = ssm
    groups, rows_total, _ = u3.shape
    rows = steps if groups == nseq else rows_total
    n_tiles = rows_total // rows
    scratch_rows = (groups - 1) * group_pitch + rows
    full = lambda arr: pl.BlockSpec(arr.shape, lambda i: (0,) * arr.ndim)
    blk = pl.BlockSpec((groups, rows, SSM_WIDTH), lambda i: (0, i, 0))
    return pl.pallas_call(
        functools.partial(_s5_body, nseq=nseq, steps=steps, pitch=pitch, group_pitch=group_pitch),
        grid=(n_tiles,),
        in_specs=[full(a_bar), full(h0), blk, full(bbig), full(cbig), full(d_row)],
        out_specs=(blk, full(h0)),
        out_shape=(jax.ShapeDtypeStruct(u3.shape, F32), jax.ShapeDtypeStruct(h0.shape, F32)),
        scratch_shapes=[pltpu.VMEM((STATE_WIDTH // LANES, scratch_rows, LANES), F32),
                        pltpu.VMEM(h0.shape, F32)],
        compiler_params=_cparams(("arbitrary",)),
        name="s5",
    )(a_bar, h0, u3, bbig, cbig, d_row)


def _s5_weights(a_re, a_im, log_dt, b, c, d_skip):
    g = SSM_GROUPS
    a_re, a_im = a_re.astype(F32), a_im.astype(F32)
    dt = jnp.exp(log_dt.astype(F32))[:, None]
    lr, li = a_re * dt, a_im * dt
    ar, ai = jnp.exp(lr) * jnp.cos(li), jnp.exp(lr) * jnp.sin(li)
    den = a_re * a_re + a_im * a_im
    fr = ((ar - 1.0) * a_re + ai * a_im) / den
    fi = (ai * a_re - (ar - 1.0) * a_im) / den
    bre, bim = b[..., 0].astype(F32), b[..., 1].astype(F32)
    bbr = fr[..., None] * bre - fi[..., None] * bim
    bbi = fr[..., None] * bim + fi[..., None] * bre
    cre, cim = c[..., 0].astype(F32), c[..., 1].astype(F32)
    eye = jnp.eye(g, dtype=F32)
    bbig = jnp.einsum('rgph,gq->ghrqp', jnp.stack([bbr, bbi]), eye).reshape(SSM_WIDTH, STATE_WIDTH)
    cbig = jnp.einsum('rgop,gq->rqpgo', jnp.stack([cre, -cim]), eye).reshape(STATE_WIDTH, SSM_WIDTH)
    a_bar = jnp.concatenate([ar.reshape(1, -1), ai.reshape(1, -1)], axis=1)
    return a_bar, bbig.astype(BF16), cbig.astype(BF16), d_skip.astype(F32).reshape(1, -1)


def _mix_body(x_ref, y_ref, o_ref, wglu_ref, wout_ref, gffn_ref, x1_ref, xn_ref, *, att_transposed):
    gl = _dot(jax.nn.gelu(y_ref[...]).astype(BF16), wglu_ref[...])
    glu = gl[:, :SSM_WIDTH] * jax.nn.sigmoid(gl[:, SSM_WIDTH:])
    o = o_ref[...]
    if att_transposed:
        o = o.astype(F32).T
    x1 = x_ref[...] + _dot(glu.astype(BF16), wout_ref[:SSM_WIDTH, :]) + _dot(o.astype(BF16), wout_ref[SSM_WIDTH:, :])
    x1_ref[...] = x1
    xn_ref[...] = _rms(x1, gffn_ref[...]).astype(BF16)


def _mix(x, y, o, w, att_transposed):
    n = x.shape[0]
    tm = TOK_TILE
    tok = lambda width: pl.BlockSpec((tm, width), lambda i: (i, 0))
    full = lambda a: pl.BlockSpec(a.shape, lambda i: (0,) * a.ndim)
    o_spec = pl.BlockSpec((ATT_WIDTH, tm), lambda i: (0, i)) if att_transposed else tok(ATT_WIDTH)
    consts = (w['wglu'], w['wout'], w['gffn'])
    return pl.pallas_call(
        functools.partial(_mix_body, att_transposed=att_transposed),
        grid=(n // tm,),
        in_specs=[tok(D_MODEL), tok(SSM_WIDTH), o_spec] + [full(a) for a in consts],
        out_specs=(tok(D_MODEL), tok(D_MODEL)),
        out_shape=(jax.ShapeDtypeStruct((n, D_MODEL), F32), jax.ShapeDtypeStruct((n, D_MODEL), BF16)),
        compiler_params=_cparams(("parallel",)),
        name="mix",
    )(x, y, o, *consts)


def _desc_top(vals_fn, count):
    tops = []
    prev = None
    for _ in range(count):
        cur = vals_fn(prev)
        tops.append(cur)
        prev = cur
    return tops


def _sort_network(n):
    pairs = []
    p = 1
    while p < n:
        k = p
        while k >= 1:
            for j in range(k % p, n - k, 2 * k):
                for i in range(min(k, n - j - k)):
                    if (i + j) // (2 * p) == (i + j + k) // (2 * p):
                        pairs.append((i + j, i + j + k))
            k //= 2
        p *= 2
    return pairs


def _top_keys(s):
    depth = PEER_KEYS // SUBLANES
    cols = [s[k * SUBLANES:(k + 1) * SUBLANES] for k in range(depth)]
    for i, j in _sort_network(depth):
        cols[i], cols[j] = jnp.maximum(cols[i], cols[j]), jnp.minimum(cols[i], cols[j])
    tops = []
    for t in range(PEER_TOPK):
        top = jnp.max(cols[0], axis=0, keepdims=True)
        tops.append(top)
        popped = cols[0] == top
        for k in range(PEER_TOPK - 1 - t):
            cols[k] = jnp.where(popped, cols[k + 1], cols[k])
    return tops


def _peer_route_body(xn_ref, wq_ref, keys_ref, rank_ref, e2_ref, lim_ref, cc_ref):
    xn = xn_ref[...]
    q = _dot(xn, wq_ref[...]).astype(BF16)
    k1 = keys_ref[0]
    k2 = keys_ref[1]
    tops1, tops2, s1_all, s2_all = [], [], [], []
    for h in range(PEER_HEADS):
        base = h * 2 * PEER_HALF
        s1 = _dot_nt(k1, q[:, base:base + PEER_HALF])
        s2 = _dot_nt(k2, q[:, base + PEER_HALF:base + 2 * PEER_HALF])
        s1_all.append(s1)
        s2_all.append(s2)
        tops1.append(_top_keys(s1))
        tops2.append(_top_keys(s2))
    a = [jnp.concatenate([tops1[h][r] for h in range(PEER_HEADS)], axis=0) for r in range(PEER_TOPK)]
    b = [jnp.concatenate([tops2[h][r] for h in range(PEER_HEADS)], axis=0) for r in range(PEER_TOPK)]
    cands = [a[i] + b[j] for i in range(PEER_TOPK) for j in range(PEER_TOPK) if (i + 1) * (j + 1) <= PEER_TOPK]

    def nxt_sum(prev):
        best = None
        for cnd in cands:
            v = cnd if prev is None else jnp.where(cnd < prev, cnd, -jnp.inf)
            best = v if best is None else jnp.maximum(best, v)
        return best

    tsum = _desc_top(nxt_sum, PEER_TOPK)
    tau = tsum[-1]
    zsum = jnp.zeros_like(tau)
    for t in tsum:
        zsum = zsum + jnp.exp(t - tsum[0])
    inv_z = 1.0 / zsum
    count = []
    for i in range(PEER_TOPK):
        cnt = jnp.zeros_like(tau)
        for j in range(PEER_TOPK):
            if (i + 1) * (j + 1) <= PEER_TOPK:
                cnt = cnt + jnp.where(a[i] + b[j] >= tau, 1.0, 0.0)
        count.append(cnt)
    for h in range(PEER_HEADS):
        row = slice(h, h + 1)
        lim = jnp.zeros(s1_all[h].shape, F32)
        rank = jnp.full(s2_all[h].shape, float(PEER_KEYS), F32)
        for r in range(PEER_TOPK):
            lim = jnp.where(s1_all[h] == a[r][row], count[r][row], lim)
            rank = jnp.where(s2_all[h] == b[r][row], float(r), rank)
        lim_ref[h] = lim
        rank_ref[h] = rank
        cc_ref[h] = jnp.exp(s1_all[h] - a[0][row]) * inv_z[row]
        e2_ref[h] = jnp.exp(s2_all[h] - b[0][row])


def _peer_route(xn, w):
    n = xn.shape[0]
    tm = TOK_TILE
    spec = pl.BlockSpec((PEER_HEADS, PEER_KEYS, tm), lambda i: (0, 0, i))
    shp = jax.ShapeDtypeStruct((PEER_HEADS, PEER_KEYS, n), F32)
    return pl.pallas_call(
        _peer_route_body,
        grid=(n // tm,),
        in_specs=[pl.BlockSpec((tm, D_MODEL), lambda i: (i, 0)),
                  pl.BlockSpec(w['peer_wq'].shape, lambda i: (0, 0)),
                  pl.BlockSpec(w['peer_keys'].shape, lambda i: (0, 0, 0))],
        out_specs=(spec, spec, spec, spec),
        out_shape=(shp, shp, shp, shp),
        compiler_params=_cparams(("parallel",)),
        name="peer_route",
    )(xn, w['peer_wq'], w['peer_keys'])


def _peer_expert_body(xn_ref, x1_ref, u_ref, vt_ref, rank_ref, e2_ref, lim_ref, cc_ref, o_ref, acc_ref, act_ref,
                      ga_ref):
    j = pl.program_id(1)
    te = u_ref.shape[0]
    tm = xn_ref.shape[0]
    nsub = te // PEER_KEYS
    tiles = GATE_ROWS // BF16_ROWS

    @pl.when(j == 0)
    def _():
        acc_ref[...] = jnp.zeros(acc_ref.shape, F32)

    act_ref[...] = _gelu_tanh(_dot_nt(u_ref[...], xn_ref[...]))
    zero = jnp.zeros((), BF16)
    first = pl.multiple_of(j * nsub, SUBLANES)
    for c in range(tm // LANES):
        ls = slice(c * LANES, (c + 1) * LANES)
        for grp in range(nsub // GATE_GROUP):
            for lo in range(0, PEER_KEYS, GATE_ROWS):
                ks = slice(lo, lo + GATE_ROWS)
                gates = [None] * GATE_GROUP
                for h in range(PEER_HEADS):
                    rank = rank_ref[h, ks, ls].astype(BF16).reshape(tiles, BF16_ROWS, LANES)
                    e2 = e2_ref[h, ks, ls].astype(BF16).reshape(tiles, BF16_ROWS, LANES)
                    lim8 = lim_ref[h, pl.ds(first, nsub), ls]
                    cc8 = cc_ref[h, pl.ds(first, nsub), ls]
                    for s in range(GATE_GROUP):
                        sub = grp * GATE_GROUP + s
                        lim = jnp.broadcast_to(lim8[sub:sub + 1], (BF16_ROWS, LANES)).astype(BF16)
                        cc = jnp.broadcast_to(cc8[sub:sub + 1], (BF16_ROWS, LANES)).astype(BF16)
                        wgt = jnp.where(rank < lim[None], e2 * cc[None], zero)
                        gates[s] = wgt if gates[s] is None else gates[s] + wgt
                for s in range(GATE_GROUP):
                    rs = slice((grp * GATE_GROUP + s) * PEER_KEYS + lo, (grp * GATE_GROUP + s) * PEER_KEYS + lo + GATE_ROWS)
                    ga_ref[rs, ls] = gates[s].reshape(GATE_ROWS, LANES) * act_ref[rs, ls].astype(BF16)
    acc_ref[...] += _dot(vt_ref[0], ga_ref[...])

    @pl.when(j == pl.num_programs(1) - 1)
    def _():
        o_ref[...] = x1_ref[...] + acc_ref[...].T


def _peer_experts(xn, x1, routes, w):
    n = xn.shape[0]
    tm = PEER_TOK_TILE
    te = PEER_EXP_TILE
    n_exp = w['peer_u'].shape[0]
    tok = pl.BlockSpec((tm, D_MODEL), lambda i, j: (i, 0))
    route = lambda a: pl.BlockSpec((PEER_HEADS, a.shape[1], tm), lambda i, j: (0, 0, i))
    return pl.pallas_call(
        _peer_expert_body,
        grid=(n // tm, n_exp // te),
        in_specs=[tok, tok,
                  pl.BlockSpec((te, D_MODEL), lambda i, j: (j, 0)),
                  pl.BlockSpec((1, D_MODEL, te), lambda i, j: (j, 0, 0)),
                  ] + [route(a) for a in routes],
        out_specs=tok,
        out_shape=jax.ShapeDtypeStruct((n, D_MODEL), F32),
        scratch_shapes=[pltpu.VMEM((D_MODEL, tm), F32), pltpu.VMEM((te, tm), F32), pltpu.VMEM((te, tm), BF16)],
        compiler_params=_cparams(("parallel", "arbitrary")),
        name="peer_experts",
    )(xn, x1, w['peer_u'], w['peer_vt'], *routes)


def _rope_tables(pos):
    inv = ROPE_THETA ** (-jnp.arange(0, D_ROPE, 2, dtype=F32) / D_ROPE)
    ang = pos.astype(F32)[:, None] * inv[None, :]
    cos, sin = jnp.cos(ang), jnp.sin(ang)
    t = pos.shape[0]
    ones = jnp.ones((t, ROPE_LO), F32)
    tail = HEAD_BLOCK - ROPE_LO - D_ROPE
    cos_t = jnp.concatenate([ones, cos, cos, jnp.ones((t, tail), F32)], axis=1)
    sin_t = jnp.concatenate([0 * ones, -sin, sin, jnp.zeros((t, tail), F32)], axis=1)
    return cos_t, sin_t


def _head_blocks(wmat, lo, width):
    pad = jnp.zeros(wmat.shape[:2] + (HEAD_BLOCK,), wmat.dtype)
    return pad.at[:, :, lo:lo + width].set(wmat).reshape(wmat.shape[0], N_HEADS * HEAD_BLOCK)


def _prep_weights(norm_mix, w_in, norm_q_lora, w_uq, norm_kv_lora, w_uk, w_uv, g_qn, g_qr, g_kn, g_kr,
                  w_glu, w_out, norm_ffn, peer_wq, peer_keys, peer_u, peer_v):
    row = lambda v: v.astype(F32).reshape(1, -1)
    kr_cols = jnp.zeros((D_MODEL, LANES), F32).at[:, ROPE_LO:ROPE_LO + D_ROPE].set(w_in[:, IN_WIDTH - D_ROPE:])
    win = jnp.concatenate([w_in[:, :IN_WIDTH - D_ROPE], kr_cols], axis=1)
    wuq = _head_blocks(w_uq.reshape(Q_LORA, N_HEADS, D_NOPE + D_ROPE), 0, D_NOPE + D_ROPE)
    wuk = _head_blocks(w_uk, 0, D_NOPE)
    wuvt = w_uv.reshape(KV_LORA, ATT_WIDTH).T
    lane = jnp.arange(HEAD_BLOCK)
    is_nope = lane < ROPE_LO
    is_rope = (lane >= ROPE_LO) & (lane < ROPE_LO + D_ROPE)
    seg = ((is_nope[:, None] & is_nope[None, :]) | (is_rope[:, None] & is_rope[None, :])).astype(BF16)
    icnt = jnp.where(is_nope, 1.0 / D_NOPE, jnp.where(is_rope, 1.0 / D_ROPE, 1.0)).astype(F32).reshape(1, -1)
    blockvec = lambda a, bvec: jnp.zeros((HEAD_BLOCK,), F32).at[:ROPE_LO].set(a).at[ROPE_LO:ROPE_LO + D_ROPE].set(bvec)
    qgain = (blockvec(g_qn, g_qr) * ATTN_SCALE).reshape(1, -1)
    kgain = blockvec(g_kn, jnp.zeros((D_ROPE,), F32)).reshape(1, -1)
    krgain = blockvec(jnp.zeros((D_NOPE,), F32), g_kr).reshape(1, -1)
    wabs = jnp.einsum('chd,d->hdc', w_uk, g_kn)
    wabs = jnp.zeros((N_HEADS, HEAD_BLOCK, KV_LORA), F32).at[:, :D_NOPE, :].set(wabs)
    wabs = wabs.reshape(N_HEADS * HEAD_BLOCK, KV_LORA)
    rsel = jnp.zeros((N_HEADS, HEAD_BLOCK, D_ROPE), F32).at[:, ROPE_LO:ROPE_LO + D_ROPE, :].set(
        jnp.broadcast_to(jnp.eye(D_ROPE, dtype=F32), (N_HEADS, D_ROPE, D_ROPE)))
    rsel = rsel.reshape(N_HEADS * HEAD_BLOCK, D_ROPE)
    head_of_lane = jnp.arange(N_HEADS * HEAD_BLOCK) // HEAD_BLOCK
    hmask8 = (head_of_lane[None, :] == jnp.arange(N_HEADS)[:, None]).astype(F32)
    return {
        'gmix': row(norm_mix), 'gq': row(norm_q_lora), 'gkv': row(norm_kv_lora),
        'seg': seg, 'icnt': icnt, 'qgain': qgain, 'kgain': kgain, 'krgain': krgain,
        'win': win.astype(BF16), 'wuq': wuq.astype(BF16), 'wuk': wuk.astype(BF16), 'wuvt': wuvt.astype(BF16),
        'hmask8': hmask8, 'wabs': wabs.astype(BF16), 'rsel': rsel.astype(BF16),
        'wukt': w_uk.reshape(KV_LORA, N_HEADS * D_NOPE).T.astype(BF16),
        'wuv': w_uv.reshape(KV_LORA, ATT_WIDTH).astype(BF16),
        'wglu': w_glu.astype(BF16), 'wout': w_out.astype(BF16), 'gffn': row(norm_ffn),
        'peer_wq': peer_wq.astype(BF16), 'peer_keys': peer_keys.astype(BF16),
        'peer_u': peer_u.astype(BF16),
        'peer_vt': peer_v.astype(BF16).reshape(-1, PEER_EXP_TILE, D_MODEL).transpose(0, 2, 1),
    }


def _peer(xn, x1, w):
    return _peer_experts(xn, x1, _peer_route(xn, w), w)


def _state_in(st):
    return jnp.moveaxis(st.astype(F32), -1, 1).reshape(st.shape[0], STATE_WIDTH)


def _state_out(h):
    return jnp.moveaxis(h.reshape(h.shape[0], 2, SSM_GROUPS, SSM_STATE), 1, -1)


def kernel(x_prompt, x_sample, cache_kv_latent, cache_k_rope, state_ssm, page_table, norm_mix, w_in, norm_q_lora, w_uq, norm_kv_lora, w_uk, w_uv, qk_gain_q_nope, qk_gain_q_rope, qk_gain_k_nope, qk_gain_k_rope, ssm_a_re, ssm_a_im, ssm_log_dt, ssm_b, ssm_c, ssm_d, w_glu, w_out, norm_ffn, peer_wq, peer_keys, peer_u, peer_v):
    depth = norm_mix.shape[0]
    batch, seq, _ = x_prompt.shape
    dec_batch, dec_seq, _ = x_sample.shape
    n_dec = dec_batch * dec_seq
    past = page_table.shape[1] * PAGE_SIZE
    assert seq % Q_TILE == 0 and Q_TILE == KV_CHUNK and seq % S5_TIME_TILE == 0 and n_dec % PEER_TOK_TILE == 0
    assert page_table.shape[1] % PAGES_PER_STEP == 0 and TOK_TILE % dec_seq == 0 and dec_seq <= PAGE_SIZE

    cos_p, sin_p = _rope_tables(jnp.arange(seq))
    cos_s, sin_s = _rope_tables(past + jnp.arange(dec_seq))
    cos_s, sin_s = jnp.tile(cos_s, (TOK_TILE // dec_seq, 1)), jnp.tile(sin_s, (TOK_TILE // dec_seq, 1))
    cache_krt = jnp.swapaxes(cache_k_rope, 2, 3)

    xp = x_prompt.reshape(batch * seq, D_MODEL)
    xs = x_sample.reshape(n_dec, D_MODEL)
    outs = {k: [] for k in ('lat_p', 'kr_p', 'ssm_p', 'lat_s', 'kr_s', 'ssm_s')}
    for l in range(depth):
        w = _prep_weights(norm_mix[l], w_in[l], norm_q_lora[l], w_uq[l], norm_kv_lora[l], w_uk[l], w_uv[l],
                          qk_gain_q_nope[l], qk_gain_q_rope[l], qk_gain_k_nope[l], qk_gain_k_rope[l],
                          w_glu[l], w_out[l], norm_ffn[l], peer_wq[l], peer_keys[l], peer_u[l], peer_v[l])
        w['hmask'] = jnp.tile(w['hmask8'], (dec_seq, 1))
        ssm = _s5_weights(ssm_a_re[l], ssm_a_im[l], ssm_log_dt[l], ssm_b[l], ssm_c[l], ssm_d[l])

        u, q, ckv, kr, kcat, vt4 = _proj(xp, cos_p, sin_p, w)
        o_t = _prompt_attention(q, kcat, vt4, batch, seq)
        pitch = S5_TIME_TILE + S5_ROW_SKEW
        y, h_last = _s5(u.reshape(batch, seq, SSM_WIDTH), jnp.zeros((batch, STATE_WIDTH), F32), ssm,
                        nseq=batch, steps=S5_TIME_TILE, pitch=pitch, group_pitch=pitch)
        x1, xn = _mix(xp, y.reshape(batch * seq, SSM_WIDTH), o_t, w, att_transposed=True)
        xp = _peer(xn, x1, w)
        outs['lat_p'].append(ckv.reshape(batch, seq, KV_LORA))
        outs['kr_p'].append(kr.reshape(batch, seq, D_ROPE))
        outs['ssm_p'].append(_state_out(h_last))

        u, q, ckv, kr, _, _ = _proj(xs, cos_s, sin_s, w)
        pad_rows = ((0, 0), (0, PAGE_SIZE - dec_seq), (0, 0))
        new_lat = jnp.pad(ckv.reshape(dec_batch, dec_seq, KV_LORA), pad_rows)
        new_krt = jnp.swapaxes(jnp.pad(kr.reshape(dec_batch, dec_seq, D_ROPE), pad_rows), 1, 2)
        o = _sample_attention(q.reshape(dec_batch, dec_seq, -1), new_lat, new_krt,
                              cache_kv_latent[l:l + 1], cache_krt[l:l + 1], page_table, w)
        y, h_last = _s5(u.reshape(1, n_dec, SSM_WIDTH), _state_in(state_ssm[l]), ssm,
                        nseq=dec_batch, steps=dec_seq, pitch=dec_seq, group_pitch=n_dec)
        x1, xn = _mix(xs, y.reshape(n_dec, SSM_WIDTH), o.reshape(n_dec, ATT_WIDTH), w, att_transposed=False)
        xs = _peer(xn, x1, w)
        outs['lat_s'].append(ckv.reshape(dec_batch, dec_seq, KV_LORA))
        outs['kr_s'].append(kr.reshape(dec_batch, dec_seq, D_ROPE))
        outs['ssm_s'].append(_state_out(h_last))

    return (xp.reshape(batch, seq, D_MODEL), xs.reshape(dec_batch, dec_seq, D_MODEL),
            jnp.stack(outs['lat_p']), jnp.stack(outs['kr_p']), jnp.stack(outs['ssm_p']),
            jnp.stack(outs['lat_s']), jnp.stack(outs['kr_s']), jnp.stack(outs['ssm_s']))
```

```python
import functools
import math

import jax
import jax.numpy as jnp
from jax import lax
from jax.experimental import pallas as pl
from jax.experimental.pallas import tpu as pltpu

F32 = jnp.float32
BF16 = jnp.bfloat16

D_MODEL = 1024
SSM_WIDTH = 512
SSM_GROUP = 16
SSM_GROUPS = 32
SSM_STATE = 64
STATE_HALF = SSM_GROUPS * SSM_STATE
STATE_WIDTH = 2 * STATE_HALF
D_NOPE = 64
D_ROPE = 32
D_V = 64
N_HEADS = 8
ATT_WIDTH = N_HEADS * D_V
Q_LORA = 384
KV_LORA = 256
IN_WIDTH = SSM_WIDTH + Q_LORA + KV_LORA + D_ROPE
ROPE_THETA = 10000.0
ATTN_SCALE = 1.0 / math.sqrt(D_NOPE + D_ROPE)
PAGE_SIZE = 128
PEER_HEADS = 8
PEER_KEYS = 128
PEER_HALF = 128
PEER_TOPK = 16
EPS = 1e-6

LANES = 128
SUBLANES = 8
BF16_ROWS = 2 * SUBLANES
HEAD_BLOCK = LANES
ROPE_LO = D_NOPE
VMEM_LIMIT = 48 * 1024 * 1024

TOK_TILE = 256
KV_CHUNK = 2 * TOK_TILE
Q_TILE = 512
HEADS_PER_STEP = 2
S5_TIME_TILE = 128
S5_ROW_SKEW = 4
PAGES_PER_STEP = 64
PEER_TOK_TILE = 512
PEER_EXP_TILE = SUBLANES * PEER_KEYS
GATE_GROUP = 8
GATE_ROWS = 64
NEG_BIG = -1e30


def _cparams(sem):
    return pltpu.CompilerParams(dimension_semantics=sem, vmem_limit_bytes=VMEM_LIMIT)


def _dot(a, b):
    return jnp.dot(a, b, preferred_element_type=F32)


def _dot_nt(a, b):
    return lax.dot_general(a, b, (((1,), (1,)), ((), ())), preferred_element_type=F32)


def _rms(x, gain):
    return x * lax.rsqrt(jnp.mean(x * x, axis=-1, keepdims=True) + EPS) * gain


def _gelu_tanh(x):
    k0 = -2.0 * math.sqrt(2.0 / math.pi) * math.log2(math.e)
    return x / (1.0 + jnp.exp2(x * (k0 + (k0 * 0.044715) * (x * x))))


def _segment_sumsq(x, seg):
    sq = x * x
    hi = sq.astype(BF16)
    lo = (sq - hi.astype(F32)).astype(BF16)
    return _dot(hi, seg) + _dot(lo, seg)


def _rope_block(x, cos, sin, first_half):
    swapped = jnp.where(first_half, pltpu.roll(x, LANES - D_ROPE // 2, 1), pltpu.roll(x, D_ROPE // 2, 1))
    return x * cos + swapped * sin


def _proj_body(x_ref, cos_ref, sin_ref, gmix_ref, win_ref, gq_ref, wuq_ref, gkv_ref, wuk_ref, wuvt_ref,
               seg_ref, icnt_ref, qgain_ref, kgain_ref, krgain_ref,
               u_ref, q_ref, ckv_ref, kr_ref, kcat_ref, vt_ref):
    xn = _rms(x_ref[...], gmix_ref[...])
    z = _dot(xn.astype(BF16), win_ref[...])
    u_ref[...] = z[:, :SSM_WIDTH]
    cq = z[:, SSM_WIDTH:SSM_WIDTH + Q_LORA]
    ckv = z[:, SSM_WIDTH + Q_LORA:SSM_WIDTH + Q_LORA + KV_LORA]
    krc = z[:, SSM_WIDTH + Q_LORA + KV_LORA:]

    cos = cos_ref[...]
    sin = sin_ref[...]
    lane = lax.broadcasted_iota(jnp.int32, cos.shape, 1)
    first_half = lane < ROPE_LO + D_ROPE // 2
    seg = seg_ref[...]
    icnt = icnt_ref[...]

    ckvn = _rms(ckv, gkv_ref[...])
    ckv_ref[...] = ckvn
    ckvb = ckvn.astype(BF16)

    krn = krc * lax.rsqrt(jnp.sum(krc * krc, axis=-1, keepdims=True) * (1.0 / D_ROPE) + EPS) * krgain_ref[...]
    krr = _rope_block(krn, cos, sin, first_half)
    kr_ref[...] = krr[:, ROPE_LO:ROPE_LO + D_ROPE]

    qp = _dot(_rms(cq, gq_ref[...]).astype(BF16), wuq_ref[...])
    kp = _dot(ckvb, wuk_ref[...])
    qgain = qgain_ref[...]
    kgain = kgain_ref[...]
    q_blocks = []
    k_blocks = []
    for h in range(N_HEADS):
        sl = slice(h * HEAD_BLOCK, (h + 1) * HEAD_BLOCK)
        qh = qp[:, sl]
        qh = qh * lax.rsqrt(_segment_sumsq(qh, seg) * icnt + EPS) * qgain
        q_blocks.append(_rope_block(qh, cos, sin, first_half).astype(BF16))
        kh = kp[:, sl]
        kh = kh * lax.rsqrt(_segment_sumsq(kh, seg) * icnt + EPS) * kgain
        k_blocks.append((kh + krr).astype(BF16))
    q_ref[...] = jnp.concatenate(q_blocks, axis=1)
    kcat_ref[...] = jnp.concatenate(k_blocks, axis=1)

    vt = _dot_nt(wuvt_ref[...], ckvb)
    vt_ref[...] = vt.astype(BF16).reshape(N_HEADS, 1, D_V, vt.shape[1])


def _proj(x, cos, sin, w):
    n = x.shape[0]
    tm = TOK_TILE
    nt = n // tm
    cos_tiles = cos.shape[0] // tm
    full = lambda a: pl.BlockSpec(a.shape, lambda i: (0,) * a.ndim)
    tok = lambda width: pl.BlockSpec((tm, width), lambda i: (i, 0))
    tab = pl.BlockSpec((tm, LANES), lambda i: (i % cos_tiles, 0))
    consts = (w['gmix'], w['win'], w['gq'], w['wuq'], w['gkv'], w['wuk'], w['wuvt'],
              w['seg'], w['icnt'], w['qgain'], w['kgain'], w['krgain'])
    out_shape = (
        jax.ShapeDtypeStruct((n, SSM_WIDTH), F32),
        jax.ShapeDtypeStruct((n, N_HEADS * HEAD_BLOCK), BF16),
        jax.ShapeDtypeStruct((n, KV_LORA), F32),
        jax.ShapeDtypeStruct((n, D_ROPE), F32),
        jax.ShapeDtypeStruct((n, N_HEADS * HEAD_BLOCK), BF16),
        jax.ShapeDtypeStruct((N_HEADS, nt, D_V, tm), BF16),
    )
    out_specs = (tok(SSM_WIDTH), tok(N_HEADS * HEAD_BLOCK), tok(KV_LORA), tok(D_ROPE),
                 tok(N_HEADS * HEAD_BLOCK),
                 pl.BlockSpec((N_HEADS, 1, D_V, tm), lambda i: (0, i, 0, 0)))
    return pl.pallas_call(
        _proj_body,
        grid=(nt,),
        in_specs=[tok(D_MODEL), tab, tab] + [full(a) for a in consts],
        out_specs=out_specs,
        out_shape=out_shape,
        compiler_params=_cparams(("parallel",)),
        name="proj",
    )(x, cos, sin, *consts)


def _prompt_attn_body(q_ref, k_ref, vt_ref, o_ref, s_ref):
    tq = q_ref.shape[0]
    tk = KV_CHUNK
    per_chunk = tk // vt_ref.shape[3]
    heads = vt_ref.shape[0]
    qi = pl.program_id(2)

    def produce(c, hh):
        lanes = slice(hh * HEAD_BLOCK, (hh + 1) * HEAD_BLOCK)
        start = pl.multiple_of(c * tk, tk)
        s = _dot_nt(k_ref[pl.ds(start, tk), lanes], q_ref[:, lanes])
        s_ref[hh * 2 + c % 2] = s
        return jnp.max(s, axis=0, keepdims=True)

    def consume(c, hh, cmax, state, diagonal):
        m, l, acc = state
        s = s_ref[hh * 2 + c % 2]
        if diagonal:
            kpos = c * tk + lax.broadcasted_iota(jnp.int32, s.shape, 0)
            qpos = qi * tq + lax.broadcasted_iota(jnp.int32, s.shape, 1)
            s = jnp.where(kpos <= qpos, s, -jnp.inf)
            cmax = jnp.max(s, axis=0, keepdims=True)
        m_new = jnp.maximum(m, cmax)
        p = jnp.exp(s - m_new)
        alpha = jnp.exp(m - m_new)
        l = alpha * l + jnp.sum(p, axis=0, keepdims=True)
        vt = jnp.concatenate([vt_ref[hh, c * per_chunk + r] for r in range(per_chunk)], axis=1)
        acc = alpha * acc + _dot(vt, p.astype(BF16))
        return m_new, l, acc

    def body(c, carry):
        nxt = tuple(produce(c + 1, hh) for hh in range(heads))
        states = tuple(consume(c, hh, carry[0][hh], carry[1][hh], False) for hh in range(heads))
        return nxt, states

    states = tuple((jnp.full((1, tq), NEG_BIG, F32), jnp.zeros((1, tq), F32), jnp.zeros((D_V, tq), F32))
                   for _ in range(heads))
    first = tuple(produce(0, hh) for hh in range(heads))
    cmax, states = lax.fori_loop(0, qi, body, (first, states))
    for hh in range(heads):
        m, l, acc = consume(qi, hh, cmax[hh], states[hh], True)
        o_ref[hh * D_V:(hh + 1) * D_V, :] = (acc / l).astype(o_ref.dtype)


def _prompt_attention(q, kcat, vt4, batch, seq):
    tq = Q_TILE
    tile = vt4.shape[3]
    nq = seq // tq
    hs = HEADS_PER_STEP
    return pl.pallas_call(
        _prompt_attn_body,
        grid=(batch, N_HEADS // hs, nq),
        in_specs=[
            pl.BlockSpec((tq, hs * HEAD_BLOCK), lambda b, h, i: (b * nq + i, h)),
            pl.BlockSpec((seq, hs * HEAD_BLOCK), lambda b, h, i: (b, h)),
            pl.BlockSpec((hs, seq // tile, D_V, tile), lambda b, h, i: (h, b, 0, 0)),
        ],
        out_specs=pl.BlockSpec((hs * D_V, tq), lambda b, h, i: (h, b * nq + i)),
        out_shape=jax.ShapeDtypeStruct((ATT_WIDTH, batch * seq), BF16),
        scratch_shapes=[pltpu.VMEM((2 * hs, KV_CHUNK, tq), F32)],
        compiler_params=_cparams(("parallel", "parallel", "arbitrary")),
        name="prompt_attn",
    )(q, kcat, vt4)


def _sample_attn_body(pt_ref, q_ref, newlat_ref, newkrt_ref, hmask_ref, wabs_ref, rsel_ref, wukt_ref, wuv_ref,
                      *rest):
    g = PAGES_PER_STEP
    lat_refs = rest[:g]
    krt_refs = rest[g:2 * g]
    o_ref = rest[2 * g]
    qabs_sc, qr_sc, m_sc, l_sc, acc_sc = rest[2 * g + 1:]
    j = pl.program_id(1)
    t_new = q_ref.shape[1]
    rows = t_new * N_HEADS

    @pl.when(j == 0)
    def _():
        q = q_ref[0].astype(F32)
        qb = jnp.broadcast_to(q[:, None, :], (t_new, N_HEADS, q.shape[1])).reshape(rows, q.shape[1])
        qbd = (qb * hmask_ref[...]).astype(BF16)
        qabs_sc[...] = _dot(qbd, wabs_ref[...]).astype(BF16)
        qr_sc[...] = _dot(qbd, rsel_ref[...]).astype(BF16)
        m_sc[...] = jnp.full(m_sc.shape, NEG_BIG, F32)
        l_sc[...] = jnp.zeros(l_sc.shape, F32)
        acc_sc[...] = jnp.zeros(acc_sc.shape, F32)

    def attend(lat32, krt32, causal):
        lat = lat32.astype(BF16)
        tk = lat.shape[0]
        both = _dot_nt(jnp.concatenate([wukt_ref[...], qabs_sc[...]], axis=0), lat)
        knt = both[:N_HEADS * D_NOPE]
        ssq = jnp.sum((knt * knt).reshape(N_HEADS, D_NOPE, tk), axis=1)
        rinv = lax.rsqrt(ssq * (1.0 / D_NOPE) + EPS)
        rinv = jnp.broadcast_to(rinv[None], (t_new, N_HEADS, tk)).reshape(rows, tk)
        s = both[N_HEADS * D_NOPE:] * rinv + _dot(qr_sc[...], krt32.astype(BF16))
        if causal:
            q_tok = lax.broadcasted_iota(jnp.int32, (t_new, N_HEADS, tk), 0).reshape(rows, tk)
            k_tok = lax.broadcasted_iota(jnp.int32, (rows, tk), 1)
            s = jnp.where(k_tok <= q_tok, s, -jnp.inf)
        m = m_sc[...]
        m_new = jnp.maximum(m, jnp.max(s, axis=1, keepdims=True))
        p = jnp.exp(s - m_new)
        alpha = jnp.exp(m - m_new)
        l_sc[...] = alpha * l_sc[...] + jnp.sum(p, axis=1, keepdims=True)
        acc_sc[...] = alpha * acc_sc[...] + _dot(p.astype(BF16), lat)
        m_sc[...] = m_new

    attend(jnp.concatenate([r[0, 0] for r in lat_refs], axis=0),
           jnp.concatenate([r[0, 0] for r in krt_refs], axis=1), causal=False)

    @pl.when(j == pl.num_programs(1) - 1)
    def _():
        attend(newlat_ref[0], newkrt_ref[0], causal=True)
        o_lat = acc_sc[...] / l_sc[...]
        full = _dot(o_lat.astype(BF16), wuv_ref[...])
        shape3 = (t_new, N_HEADS, ATT_WIDTH)
        col = lax.broadcasted_iota(jnp.int32, shape3, 2)
        lo = lax.broadcasted_iota(jnp.int32, shape3, 1) * D_V
        sel = jnp.where(col >= lo, jnp.where(col < lo + D_V, full.reshape(shape3), 0.0), 0.0)
        o_ref[0] = jnp.sum(sel, axis=1)


def _sample_attention(q, new_lat, new_krt, cache_lat, cache_krt, page_table, w):
    bd, t_new, _ = q.shape
    n_pages = page_table.shape[1]
    g = PAGES_PER_STEP
    steps = n_pages // g
    rows = t_new * N_HEADS
    per_b = lambda a: pl.BlockSpec((1,) + a.shape[1:], lambda b, j, pt: (b, 0, 0))
    full = lambda a: pl.BlockSpec(a.shape, lambda b, j, pt: (0,) * a.ndim)

    def page_spec(shape, k):
        return pl.BlockSpec((1, 1) + shape, lambda b, j, pt: (0, pt[b, j * g + k], 0, 0))

    consts = (w['hmask'], w['wabs'], w['rsel'], w['wukt'], w['wuv'])
    grid_spec = pltpu.PrefetchScalarGridSpec(
        num_scalar_prefetch=1,
        grid=(bd, steps),
        in_specs=[per_b(q), per_b(new_lat), per_b(new_krt)] + [full(a) for a in consts]
        + [page_spec((PAGE_SIZE, KV_LORA), k) for k in range(g)]
        + [page_spec((D_ROPE, PAGE_SIZE), k) for k in range(g)],
        out_specs=pl.BlockSpec((1, t_new, ATT_WIDTH), lambda b, j, pt: (b, 0, 0)),
        scratch_shapes=[
            pltpu.VMEM((rows, KV_LORA), BF16),
            pltpu.VMEM((rows, D_ROPE), BF16),
            pltpu.VMEM((rows, 1), F32),
            pltpu.VMEM((rows, 1), F32),
            pltpu.VMEM((rows, KV_LORA), F32),
        ],
    )
    return pl.pallas_call(
        _sample_attn_body,
        grid_spec=grid_spec,
        out_shape=jax.ShapeDtypeStruct((bd, t_new, ATT_WIDTH), F32),
        compiler_params=_cparams(("parallel", "arbitrary")),
        name="sample_attn",
    )(page_table, q, new_lat, new_krt, *consts, *([cache_lat] * g), *([cache_krt] * g))


def _s5_body(a_ref, h0_ref, u_ref, bbig_ref, cbig_ref, d_ref, y_ref, hl_ref, st_ref, h_ref,
             *, nseq, steps, pitch, group_pitch):
    groups, rows, _ = u_ref.shape
    slabs = STATE_WIDTH // LANES
    half = slabs // 2

    @pl.when(pl.program_id(0) == 0)
    def _():
        h_ref[...] = h0_ref[...]

    for g in range(groups):
        inc = _dot(u_ref[g].astype(BF16), bbig_ref[...])
        for k in range(slabs):
            st_ref[k, g * group_pitch:g * group_pitch + rows, :] = inc[:, k * LANES:(k + 1) * LANES]

    a = a_ref[...]
    ar = [a[:, k * LANES:(k + 1) * LANES] for k in range(half)]
    ai = [a[:, (half + k) * LANES:(half + k + 1) * LANES] for k in range(half)]

    def body(t, carry):
        out = []
        for k in range(half):
            hr, hi = carry[k], carry[half + k]
            sel = pl.ds(t, nseq, stride=pitch)
            nr = ar[k] * hr - ai[k] * hi + st_ref[k, sel, :]
            ni = ar[k] * hi + ai[k] * hr + st_ref[half + k, sel, :]
            st_ref[k, sel, :] = nr
            st_ref[half + k, sel, :] = ni
            out.append((nr, ni))
        return tuple(o[0] for o in out) + tuple(o[1] for o in out)

    h = h_ref[...]
    final = lax.fori_loop(0, steps, body, tuple(h[:, k * LANES:(k + 1) * LANES] for k in range(slabs)))
    h_new = jnp.concatenate(final, axis=1)
    h_ref[...] = h_new
    hl_ref[...] = h_new

    d = d_ref[...]
    for g in range(groups):
        lo = g * group_pitch
        states = jnp.concatenate([st_ref[k, lo:lo + rows, :].astype(BF16) for k in range(slabs)], axis=1)
        y_ref[g] = _dot(states, cbig_ref[...]) + d * u_ref[g]


def _s5(u3, h0, ssm, nseq, steps, pitch, group_pitch):
    a_bar, bbig, cbig, d_row = ssm
    groups, rows_total, _ = u3.shape
    rows = steps if groups == nseq else rows_total
    n_tiles = rows_total // rows
    scratch_rows = (groups - 1) * group_pitch + rows
    full = lambda arr: pl.BlockSpec(arr.shape, lambda i: (0,) * arr.ndim)
    blk = pl.BlockSpec((groups, rows, SSM_WIDTH), lambda i: (0, i, 0))
    return pl.pallas_call(
        functools.partial(_s5_body, nseq=nseq, steps=steps, pitch=pitch, group_pitch=group_pitch),
        grid=(n_tiles,),
        in_specs=[full(a_bar), full(h0), blk, full(bbig), full(cbig), full(d_row)],
        out_specs=(blk, full(h0)),
        out_shape=(jax.ShapeDtypeStruct(u3.shape, F32), jax.ShapeDtypeStruct(h0.shape, F32)),
        scratch_shapes=[pltpu.VMEM((STATE_WIDTH // LANES, scratch_rows, LANES), F32),
                        pltpu.VMEM(h0.shape, F32)],
        compiler_params=_cparams(("arbitrary",)),
        name="s5",
    )(a_bar, h0, u3, bbig, cbig, d_row)


def _s5_weights(a_re, a_im, log_dt, b, c, d_skip):
    g = SSM_GROUPS
    a_re, a_im = a_re.astype(F32), a_im.astype(F32)
    dt = jnp.exp(log_dt.astype(F32))[:, None]
    lr, li = a_re * dt, a_im * dt
    ar, ai = jnp.exp(lr) * jnp.cos(li), jnp.exp(lr) * jnp.sin(li)
    den = a_re * a_re + a_im * a_im
    fr = ((ar - 1.0) * a_re + ai * a_im) / den
    fi = (ai * a_re - (ar - 1.0) * a_im) / den
    bre, bim = b[..., 0].astype(F32), b[..., 1].astype(F32)
    bbr = fr[..., None] * bre - fi[..., None] * bim
    bbi = fr[..., None] * bim + fi[..., None] * bre
    cre, cim = c[..., 0].astype(F32), c[..., 1].astype(F32)
    eye = jnp.eye(g, dtype=F32)
    bbig = jnp.einsum('rgph,gq->ghrqp', jnp.stack([bbr, bbi]), eye).reshape(SSM_WIDTH, STATE_WIDTH)
    cbig = jnp.einsum('rgop,gq->rqpgo', jnp.stack([cre, -cim]), eye).reshape(STATE_WIDTH, SSM_WIDTH)
    a_bar = jnp.concatenate([ar.reshape(1, -1), ai.reshape(1, -1)], axis=1)
    return a_bar, bbig.astype(BF16), cbig.astype(BF16), d_skip.astype(F32).reshape(1, -1)


def _mix_body(x_ref, y_ref, o_ref, wglu_ref, wout_ref, gffn_ref, x1_ref, xn_ref, *, att_transposed):
    gl = _dot(jax.nn.gelu(y_ref[...]).astype(BF16), wglu_ref[...])
    glu = gl[:, :SSM_WIDTH] * jax.nn.sigmoid(gl[:, SSM_WIDTH:])
    o = o_ref[...]
    if att_transposed:
        o = o.astype(F32).T
    x1 = x_ref[...] + _dot(glu.astype(BF16), wout_ref[:SSM_WIDTH, :]) + _dot(o.astype(BF16), wout_ref[SSM_WIDTH:, :])
    x1_ref[...] = x1
    xn_ref[...] = _rms(x1, gffn_ref[...]).astype(BF16)


def _mix(x, y, o, w, att_transposed):
    n = x.shape[0]
    tm = TOK_TILE
    tok = lambda width: pl.BlockSpec((tm, width), lambda i: (i, 0))
    full = lambda a: pl.BlockSpec(a.shape, lambda i: (0,) * a.ndim)
    o_spec = pl.BlockSpec((ATT_WIDTH, tm), lambda i: (0, i)) if att_transposed else tok(ATT_WIDTH)
    consts = (w['wglu'], w['wout'], w['gffn'])
    return pl.pallas_call(
        functools.partial(_mix_body, att_transposed=att_transposed),
        grid=(n // tm,),
        in_specs=[tok(D_MODEL), tok(SSM_WIDTH), o_spec] + [full(a) for a in consts],
        out_specs=(tok(D_MODEL), tok(D_MODEL)),
        out_shape=(jax.ShapeDtypeStruct((n, D_MODEL), F32), jax.ShapeDtypeStruct((n, D_MODEL), BF16)),
        compiler_params=_cparams(("parallel",)),
        name="mix",
    )(x, y, o, *consts)


def _desc_top(vals_fn, count):
    tops = []
    prev = None
    for _ in range(count):
        cur = vals_fn(prev)
        tops.append(cur)
        prev = cur
    return tops


def _sort_network(n):
    pairs = []
    p = 1
    while p < n:
        k = p
        while k >= 1:
            for j in range(k % p, n - k, 2 * k):
                for i in range(min(k, n - j - k)):
                    if (i + j) // (2 * p) == (i + j + k) // (2 * p):
                        pairs.append((i + j, i + j + k))
            k //= 2
        p *= 2
    return pairs


def _top_keys(s):
    depth = PEER_KEYS // SUBLANES
    cols = [s[k * SUBLANES:(k + 1) * SUBLANES] for k in range(depth)]
    for i, j in _sort_network(depth):
        cols[i], cols[j] = jnp.maximum(cols[i], cols[j]), jnp.minimum(cols[i], cols[j])
    tops = []
    for t in range(PEER_TOPK):
        top = jnp.max(cols[0], axis=0, keepdims=True)
        tops.append(top)
        popped = cols[0] == top
        for k in range(PEER_TOPK - 1 - t):
            cols[k] = jnp.where(popped, cols[k + 1], cols[k])
    return tops


def _peer_route_body(xn_ref, wq_ref, keys_ref, rank_ref, e2_ref, lim_ref, cc_ref):
    xn = xn_ref[...]
    q = _dot(xn, wq_ref[...]).astype(BF16)
    k1 = keys_ref[0]
    k2 = keys_ref[1]
    tops1, tops2, s1_all, s2_all = [], [], [], []
    for h in range(PEER_HEADS):
        base = h * 2 * PEER_HALF
        s1 = _dot_nt(k1, q[:, base:base + PEER_HALF])
        s2 = _dot_nt(k2, q[:, base + PEER_HALF:base + 2 * PEER_HALF])
        s1_all.append(s1)
        s2_all.append(s2)
        tops1.append(_top_keys(s1))
        tops2.append(_top_keys(s2))
    a = [jnp.concatenate([tops1[h][r] for h in range(PEER_HEADS)], axis=0) for r in range(PEER_TOPK)]
    b = [jnp.concatenate([tops2[h][r] for h in range(PEER_HEADS)], axis=0) for r in range(PEER_TOPK)]
    cands = [a[i] + b[j] for i in range(PEER_TOPK) for j in range(PEER_TOPK) if (i + 1) * (j + 1) <= PEER_TOPK]

    def nxt_sum(prev):
        best = None
        for cnd in cands:
            v = cnd if prev is None else jnp.where(cnd < prev, cnd, -jnp.inf)
            best = v if best is None else jnp.maximum(best, v)
        return best

    tsum = _desc_top(nxt_sum, PEER_TOPK)
    tau = tsum[-1]
    zsum = jnp.zeros_like(tau)
    for t in tsum:
        zsum = zsum + jnp.exp(t - tsum[0])
    inv_z = 1.0 / zsum
    count = []
    for i in range(PEER_TOPK):
        cnt = jnp.zeros_like(tau)
        for j in range(PEER_TOPK):
            if (i + 1) * (j + 1) <= PEER_TOPK:
                cnt = cnt + jnp.where(a[i] + b[j] >= tau, 1.0, 0.0)
        count.append(cnt)
    for h in range(PEER_HEADS):
        row = slice(h, h + 1)
        lim = jnp.zeros(s1_all[h].shape, F32)
        rank = jnp.full(s2_all[h].shape, float(PEER_KEYS), F32)
        for r in range(PEER_TOPK):
            lim = jnp.where(s1_all[h] == a[r][row], count[r][row], lim)
            rank = jnp.where(s2_all[h] == b[r][row], float(r), rank)
        lim_ref[h] = lim
        rank_ref[h] = rank
        cc_ref[h] = jnp.exp(s1_all[h] - a[0][row]) * inv_z[row]
        e2_ref[h] = jnp.exp(s2_all[h] - b[0][row])


def _peer_route(xn, w):
    n = xn.shape[0]
    tm = TOK_TILE
    spec = pl.BlockSpec((PEER_HEADS, PEER_KEYS, tm), lambda i: (0, 0, i))
    shp = jax.ShapeDtypeStruct((PEER_HEADS, PEER_KEYS, n), F32)
    return pl.pallas_call(
        _peer_route_body,
        grid=(n // tm,),
        in_specs=[pl.BlockSpec((tm, D_MODEL), lambda i: (i, 0)),
                  pl.BlockSpec(w['peer_wq'].shape, lambda i: (0, 0)),
                  pl.BlockSpec(w['peer_keys'].shape, lambda i: (0, 0, 0))],
        out_specs=(spec, spec, spec, spec),
        out_shape=(shp, shp, shp, shp),
        compiler_params=_cparams(("parallel",)),
        name="peer_route",
    )(xn, w['peer_wq'], w['peer_keys'])


def _peer_expert_body(xn_ref, x1_ref, u_ref, vt_ref, rank_ref, e2_ref, lim_ref, cc_ref, o_ref, acc_ref, act_ref,
                      ga_ref):
    j = pl.program_id(1)
    te = u_ref.shape[0]
    tm = xn_ref.shape[0]
    nsub = te // PEER_KEYS
    tiles = GATE_ROWS // BF16_ROWS

    @pl.when(j == 0)
    def _():
        acc_ref[...] = jnp.zeros(acc_ref.shape, F32)

    act_ref[...] = _gelu_tanh(_dot_nt(u_ref[...], xn_ref[...]))
    zero = jnp.zeros((), BF16)
    first = pl.multiple_of(j * nsub, SUBLANES)
    for c in range(tm // LANES):
        ls = slice(c * LANES, (c + 1) * LANES)
        for grp in range(nsub // GATE_GROUP):
            for lo in range(0, PEER_KEYS, GATE_ROWS):
                ks = slice(lo, lo + GATE_ROWS)
                gates = [None] * GATE_GROUP
                for h in range(PEER_HEADS):
                    rank = rank_ref[h, ks, ls].astype(BF16).reshape(tiles, BF16_ROWS, LANES)
                    e2 = e2_ref[h, ks, ls].astype(BF16).reshape(tiles, BF16_ROWS, LANES)
                    lim8 = lim_ref[h, pl.ds(first, nsub), ls]
                    cc8 = cc_ref[h, pl.ds(first, nsub), ls]
                    for s in range(GATE_GROUP):
                        sub = grp * GATE_GROUP + s
                        lim = jnp.broadcast_to(lim8[sub:sub + 1], (BF16_ROWS, LANES)).astype(BF16)
                        cc = jnp.broadcast_to(cc8[sub:sub + 1], (BF16_ROWS, LANES)).astype(BF16)
                        wgt = jnp.where(rank < lim[None], e2 * cc[None], zero)
                        gates[s] = wgt if gates[s] is None else gates[s] + wgt
                for s in range(GATE_GROUP):
                    rs = slice((grp * GATE_GROUP + s) * PEER_KEYS + lo, (grp * GATE_GROUP + s) * PEER_KEYS + lo + GATE_ROWS)
                    ga_ref[rs, ls] = gates[s].reshape(GATE_ROWS, LANES) * act_ref[rs, ls].astype(BF16)
    acc_ref[...] += _dot(vt_ref[0], ga_ref[...])

    @pl.when(j == pl.num_programs(1) - 1)
    def _():
        o_ref[...] = x1_ref[...] + acc_ref[...].T


def _peer_experts(xn, x1, routes, w):
    n = xn.shape[0]
    tm = PEER_TOK_TILE
    te = PEER_EXP_TILE
    n_exp = w['peer_u'].shape[0]
    tok = pl.BlockSpec((tm, D_MODEL), lambda i, j: (i, 0))
    route = lambda a: pl.BlockSpec((PEER_HEADS, a.shape[1], tm), lambda i, j: (0, 0, i))
    return pl.pallas_call(
        _peer_expert_body,
        grid=(n // tm, n_exp // te),
        in_specs=[tok, tok,
                  pl.BlockSpec((te, D_MODEL), lambda i, j: (j, 0)),
                  pl.BlockSpec((1, D_MODEL, te), lambda i, j: (j, 0, 0)),
                  ] + [route(a) for a in routes],
        out_specs=tok,
        out_shape=jax.ShapeDtypeStruct((n, D_MODEL), F32),
        scratch_shapes=[pltpu.VMEM((D_MODEL, tm), F32), pltpu.VMEM((te, tm), F32), pltpu.VMEM((te, tm), BF16)],
        compiler_params=_cparams(("parallel", "arbitrary")),
        name="peer_experts",
    )(xn, x1, w['peer_u'], w['peer_vt'], *routes)


def _rope_tables(pos):
    inv = ROPE_THETA ** (-jnp.arange(0, D_ROPE, 2, dtype=F32) / D_ROPE)
    ang = pos.astype(F32)[:, None] * inv[None, :]
    cos, sin = jnp.cos(ang), jnp.sin(ang)
    t = pos.shape[0]
    ones = jnp.ones((t, ROPE_LO), F32)
    tail = HEAD_BLOCK - ROPE_LO - D_ROPE
    cos_t = jnp.concatenate([ones, cos, cos, jnp.ones((t, tail), F32)], axis=1)
    sin_t = jnp.concatenate([0 * ones, -sin, sin, jnp.zeros((t, tail), F32)], axis=1)
    return cos_t, sin_t


def _head_blocks(wmat, lo, width):
    pad = jnp.zeros(wmat.shape[:2] + (HEAD_BLOCK,), wmat.dtype)
    return pad.at[:, :, lo:lo + width].set(wmat).reshape(wmat.shape[0], N_HEADS * HEAD_BLOCK)


def _prep_weights(norm_mix, w_in, norm_q_lora, w_uq, norm_kv_lora, w_uk, w_uv, g_qn, g_qr, g_kn, g_kr,
                  w_glu, w_out, norm_ffn, peer_wq, peer_keys, peer_u, peer_v):
    row = lambda v: v.astype(F32).reshape(1, -1)
    kr_cols = jnp.zeros((D_MODEL, LANES), F32).at[:, ROPE_LO:ROPE_LO + D_ROPE].set(w_in[:, IN_WIDTH - D_ROPE:])
    win = jnp.concatenate([w_in[:, :IN_WIDTH - D_ROPE], kr_cols], axis=1)
    wuq = _head_blocks(w_uq.reshape(Q_LORA, N_HEADS, D_NOPE + D_ROPE), 0, D_NOPE + D_ROPE)
    wuk = _head_blocks(w_uk, 0, D_NOPE)
    wuvt = w_uv.reshape(KV_LORA, ATT_WIDTH).T
    lane = jnp.arange(HEAD_BLOCK)
    is_nope = lane < ROPE_LO
    is_rope = (lane >= ROPE_LO) & (lane < ROPE_LO + D_ROPE)
    seg = ((is_nope[:, None] & is_nope[None, :]) | (is_rope[:, None] & is_rope[None, :])).astype(BF16)
    icnt = jnp.where(is_nope, 1.0 / D_NOPE, jnp.where(is_rope, 1.0 / D_ROPE, 1.0)).astype(F32).reshape(1, -1)
    blockvec = lambda a, bvec: jnp.zeros((HEAD_BLOCK,), F32).at[:ROPE_LO].set(a).at[ROPE_LO:ROPE_LO + D_ROPE].set(bvec)
    qgain = (blockvec(g_qn, g_qr) * ATTN_SCALE).reshape(1, -1)
    kgain = blockvec(g_kn, jnp.zeros((D_ROPE,), F32)).reshape(1, -1)
    krgain = blockvec(jnp.zeros((D_NOPE,), F32), g_kr).reshape(1, -1)
    wabs = jnp.einsum('chd,d->hdc', w_uk, g_kn)
    wabs = jnp.zeros((N_HEADS, HEAD_BLOCK, KV_LORA), F32).at[:, :D_NOPE, :].set(wabs)
    wabs = wabs.reshape(N_HEADS * HEAD_BLOCK, KV_LORA)
    rsel = jnp.zeros((N_HEADS, HEAD_BLOCK, D_ROPE), F32).at[:, ROPE_LO:ROPE_LO + D_ROPE, :].set(
        jnp.broadcast_to(jnp.eye(D_ROPE, dtype=F32), (N_HEADS, D_ROPE, D_ROPE)))
    rsel = rsel.reshape(N_HEADS * HEAD_BLOCK, D_ROPE)
    head_of_lane = jnp.arange(N_HEADS * HEAD_BLOCK) // HEAD_BLOCK
    hmask8 = (head_of_lane[None, :] == jnp.arange(N_HEADS)[:, None]).astype(F32)
    return {
        'gmix': row(norm_mix), 'gq': row(norm_q_lora), 'gkv': row(norm_kv_lora),
        'seg': seg, 'icnt': icnt, 'qgain': qgain, 'kgain': kgain, 'krgain': krgain,
        'win': win.astype(BF16), 'wuq': wuq.astype(BF16), 'wuk': wuk.astype(BF16), 'wuvt': wuvt.astype(BF16),
        'hmask8': hmask8, 'wabs': wabs.astype(BF16), 'rsel': rsel.astype(BF16),
        'wukt': w_uk.reshape(KV_LORA, N_HEADS * D_NOPE).T.astype(BF16),
        'wuv': w_uv.reshape(KV_LORA, ATT_WIDTH).astype(BF16),
        'wglu': w_glu.astype(BF16), 'wout': w_out.astype(BF16), 'gffn': row(norm_ffn),
        'peer_wq': peer_wq.astype(BF16), 'peer_keys': peer_keys.astype(BF16),
        'peer_u': peer_u.astype(BF16),
        'peer_vt': peer_v.astype(BF16).reshape(-1, PEER_EXP_TILE, D_MODEL).transpose(0, 2, 1),
    }


def _peer(xn, x1, w):
    return _peer_experts(xn, x1, _peer_route(xn, w), w)


def _state_in(st):
    return jnp.moveaxis(st.astype(F32), -1, 1).reshape(st.shape[0], STATE_WIDTH)


def _state_out(h):
    return jnp.moveaxis(h.reshape(h.shape[0], 2, SSM_GROUPS, SSM_STATE), 1, -1)


def kernel(x_prompt, x_sample, cache_kv_latent, cache_k_rope, state_ssm, page_table, norm_mix, w_in, norm_q_lora, w_uq, norm_kv_lora, w_uk, w_uv, qk_gain_q_nope, qk_gain_q_rope, qk_gain_k_nope, qk_gain_k_rope, ssm_a_re, ssm_a_im, ssm_log_dt, ssm_b, ssm_c, ssm_d, w_glu, w_out, norm_ffn, peer_wq, peer_keys, peer_u, peer_v):
    depth = norm_mix.shape[0]
    batch, seq, _ = x_prompt.shape
    dec_batch, dec_seq, _ = x_sample.shape
    n_dec = dec_batch * dec_seq
    past = page_table.shape[1] * PAGE_SIZE
    assert seq % Q_TILE == 0 and Q_TILE == KV_CHUNK and seq % S5_TIME_TILE == 0 and n_dec % PEER_TOK_TILE == 0
    assert page_table.shape[1] % PAGES_PER_STEP == 0 and TOK_TILE % dec_seq == 0 and dec_seq <= PAGE_SIZE

    cos_p, sin_p = _rope_tables(jnp.arange(seq))
    cos_s, sin_s = _rope_tables(past + jnp.arange(dec_seq))
    cos_s, sin_s = jnp.tile(cos_s, (TOK_TILE // dec_seq, 1)), jnp.tile(sin_s, (TOK_TILE // dec_seq, 1))
    cache_krt = jnp.swapaxes(cache_k_rope, 2, 3)

    xp = x_prompt.reshape(batch * seq, D_MODEL)
    xs = x_sample.reshape(n_dec, D_MODEL)
    outs = {k: [] for k in ('lat_p', 'kr_p', 'ssm_p', 'lat_s', 'kr_s', 'ssm_s')}
    for l in range(depth):
        w = _prep_weights(norm_mix[l], w_in[l], norm_q_lora[l], w_uq[l], norm_kv_lora[l], w_uk[l], w_uv[l],
                          qk_gain_q_nope[l], qk_gain_q_rope[l], qk_gain_k_nope[l], qk_gain_k_rope[l],
                          w_glu[l], w_out[l], norm_ffn[l], peer_wq[l], peer_keys[l], peer_u[l], peer_v[l])
        w['hmask'] = jnp.tile(w['hmask8'], (dec_seq, 1))
        ssm = _s5_weights(ssm_a_re[l], ssm_a_im[l], ssm_log_dt[l], ssm_b[l], ssm_c[l], ssm_d[l])

        u, q, ckv, kr, kcat, vt4 = _proj(xp, cos_p, sin_p, w)
        o_t = _prompt_attention(q, kcat, vt4, batch, seq)
        pitch = S5_TIME_TILE + S5_ROW_SKEW
        y, h_last = _s5(u.reshape(batch, seq, SSM_WIDTH), jnp.zeros((batch, STATE_WIDTH), F32), ssm,
                        nseq=batch, steps=S5_TIME_TILE, pitch=pitch, group_pitch=pitch)
        x1, xn = _mix(xp, y.reshape(batch * seq, SSM_WIDTH), o_t, w, att_transposed=True)
        xp = _peer(xn, x1, w)
        outs['lat_p'].append(ckv.reshape(batch, seq, KV_LORA))
        outs['kr_p'].append(kr.reshape(batch, seq, D_ROPE))
        outs['ssm_p'].append(_state_out(h_last))

        u, q, ckv, kr, _, _ = _proj(xs, cos_s, sin_s, w)
        pad_rows = ((0, 0), (0, PAGE_SIZE - dec_seq), (0, 0))
        new_lat = jnp.pad(ckv.reshape(dec_batch, dec_seq, KV_LORA), pad_rows)
        new_krt = jnp.swapaxes(jnp.pad(kr.reshape(dec_batch, dec_seq, D_ROPE), pad_rows), 1, 2)
        o = _sample_attention(q.reshape(dec_batch, dec_seq, -1), new_lat, new_krt,
                              cache_kv_latent[l:l + 1], cache_krt[l:l + 1], page_table, w)
        y, h_last = _s5(u.reshape(1, n_dec, SSM_WIDTH), _state_in(state_ssm[l]), ssm,
                        nseq=dec_batch, steps=dec_seq, pitch=dec_seq, group_pitch=n_dec)
        x1, xn = _mix(xs, y.reshape(n_dec, SSM_WIDTH), o.reshape(n_dec, ATT_WIDTH), w, att_transposed=False)
        xs = _peer(xn, x1, w)
        outs['lat_s'].append(ckv.reshape(dec_batch, dec_seq, KV_LORA))
        outs['kr_s'].append(kr.reshape(dec_batch, dec_seq, D_ROPE))
        outs['ssm_s'].append(_state_out(h_last))

    return (xp.reshape(batch, seq, D_MODEL), xs.reshape(dec_batch, dec_seq, D_MODEL),
            jnp.stack(outs['lat_p']), jnp.stack(outs['kr_p']), jnp.stack(outs['ssm_p']),
            jnp.stack(outs['lat_s']), jnp.stack(outs['kr_s']), jnp.stack(outs['ssm_s']))
```

```python
import functools
import math

import jax
import jax.numpy as jnp
from jax import lax
from jax.experimental import pallas as pl
from jax.experimental.pallas import tpu as pltpu

F32 = jnp.float32
BF16 = jnp.bfloat16

D_MODEL = 1024
SSM_WIDTH = 512
SSM_GROUP = 16
SSM_GROUPS = 32
SSM_STATE = 64
STATE_HALF = SSM_GROUPS * SSM_STATE
STATE_WIDTH = 2 * STATE_HALF
D_NOPE = 64
D_ROPE = 32
D_V = 64
N_HEADS = 8
ATT_WIDTH = N_HEADS * D_V
Q_LORA = 384
KV_LORA = 256
IN_WIDTH = SSM_WIDTH + Q_LORA + KV_LORA + D_ROPE
ROPE_THETA = 10000.0
ATTN_SCALE = 1.0 / math.sqrt(D_NOPE + D_ROPE)
PAGE_SIZE = 128
PEER_HEADS = 8
PEER_KEYS = 128
PEER_HALF = 128
PEER_TOPK = 16
EPS = 1e-6

LANES = 128
SUBLANES = 8
BF16_ROWS = 2 * SUBLANES
HEAD_BLOCK = LANES
ROPE_LO = D_NOPE
VMEM_LIMIT = 48 * 1024 * 1024

TOK_TILE = 256
KV_CHUNK = 2 * TOK_TILE
Q_TILE = 512
HEADS_PER_STEP = 4
S5_TIME_TILE = 128
S5_ROW_SKEW = 4
PAGES_PER_STEP = 64
PEER_TOK_TILE = 512
PEER_EXP_TILE = SUBLANES * PEER_KEYS
GATE_GROUP = 8
GATE_ROWS = 64
NEG_BIG = -1e30


def _cparams(sem):
    return pltpu.CompilerParams(dimension_semantics=sem, vmem_limit_bytes=VMEM_LIMIT)


def _dot(a, b):
    return jnp.dot(a, b, preferred_element_type=F32)


def _dot_nt(a, b):
    return lax.dot_general(a, b, (((1,), (1,)), ((), ())), preferred_element_type=F32)


def _rms(x, gain):
    return x * lax.rsqrt(jnp.mean(x * x, axis=-1, keepdims=True) + EPS) * gain


def _gelu_tanh(x):
    k0 = -2.0 * math.sqrt(2.0 / math.pi) * math.log2(math.e)
    return x / (1.0 + jnp.exp2(x * (k0 + (k0 * 0.044715) * (x * x))))


def _segment_sumsq(x, seg):
    sq = x * x
    hi = sq.astype(BF16)
    lo = (sq - hi.astype(F32)).astype(BF16)
    return _dot(hi, seg) + _dot(lo, seg)


def _rope_block(x, cos, sin, first_half):
    swapped = jnp.where(first_half, pltpu.roll(x, LANES - D_ROPE // 2, 1), pltpu.roll(x, D_ROPE // 2, 1))
    return x * cos + swapped * sin


def _proj_body(x_ref, cos_ref, sin_ref, gmix_ref, win_ref, gq_ref, wuq_ref, gkv_ref, wuk_ref, wuvt_ref,
               seg_ref, icnt_ref, qgain_ref, kgain_ref, krgain_ref,
               u_ref, q_ref, ckv_ref, kr_ref, kcat_ref, vt_ref):
    xn = _rms(x_ref[...], gmix_ref[...])
    z = _dot(xn.astype(BF16), win_ref[...])
    u_ref[...] = z[:, :SSM_WIDTH]
    cq = z[:, SSM_WIDTH:SSM_WIDTH + Q_LORA]
    ckv = z[:, SSM_WIDTH + Q_LORA:SSM_WIDTH + Q_LORA + KV_LORA]
    krc = z[:, SSM_WIDTH + Q_LORA + KV_LORA:]

    cos = cos_ref[...]
    sin = sin_ref[...]
    lane = lax.broadcasted_iota(jnp.int32, cos.shape, 1)
    first_half = lane < ROPE_LO + D_ROPE // 2
    seg = seg_ref[...]
    icnt = icnt_ref[...]

    ckvn = _rms(ckv, gkv_ref[...])
    ckv_ref[...] = ckvn
    ckvb = ckvn.astype(BF16)

    krn = krc * lax.rsqrt(jnp.sum(krc * krc, axis=-1, keepdims=True) * (1.0 / D_ROPE) + EPS) * krgain_ref[...]
    krr = _rope_block(krn, cos, sin, first_half)
    kr_ref[...] = krr[:, ROPE_LO:ROPE_LO + D_ROPE]

    qp = _dot(_rms(cq, gq_ref[...]).astype(BF16), wuq_ref[...])
    kp = _dot(ckvb, wuk_ref[...])
    qgain = qgain_ref[...]
    kgain = kgain_ref[...]
    q_blocks = []
    k_blocks = []
    for h in range(N_HEADS):
        sl = slice(h * HEAD_BLOCK, (h + 1) * HEAD_BLOCK)
        qh = qp[:, sl]
        qh = qh * lax.rsqrt(_segment_sumsq(qh, seg) * icnt + EPS) * qgain
        q_blocks.append(_rope_block(qh, cos, sin, first_half).astype(BF16))
        kh = kp[:, sl]
        kh = kh * lax.rsqrt(_segment_sumsq(kh, seg) * icnt + EPS) * kgain
        k_blocks.append((kh + krr).astype(BF16))
    q_ref[...] = jnp.concatenate(q_blocks, axis=1)
    kcat_ref[...] = jnp.concatenate(k_blocks, axis=1)

    vt = _dot_nt(wuvt_ref[...], ckvb)
    vt_ref[...] = vt.astype(BF16).reshape(N_HEADS, 1, D_V, vt.shape[1])


def _proj(x, cos, sin, w):
    n = x.shape[0]
    tm = TOK_TILE
    nt = n // tm
    cos_tiles = cos.shape[0] // tm
    full = lambda a: pl.BlockSpec(a.shape, lambda i: (0,) * a.ndim)
    tok = lambda width: pl.BlockSpec((tm, width), lambda i: (i, 0))
    tab = pl.BlockSpec((tm, LANES), lambda i: (i % cos_tiles, 0))
    consts = (w['gmix'], w['win'], w['gq'], w['wuq'], w['gkv'], w['wuk'], w['wuvt'],
              w['seg'], w['icnt'], w['qgain'], w['kgain'], w['krgain'])
    out_shape = (
        jax.ShapeDtypeStruct((n, SSM_WIDTH), F32),
        jax.ShapeDtypeStruct((n, N_HEADS * HEAD_BLOCK), BF16),
        jax.ShapeDtypeStruct((n, KV_LORA), F32),
        jax.ShapeDtypeStruct((n, D_ROPE), F32),
        jax.ShapeDtypeStruct((n, N_HEADS * HEAD_BLOCK), BF16),
        jax.ShapeDtypeStruct((N_HEADS, nt, D_V, tm), BF16),
    )
    out_specs = (tok(SSM_WIDTH), tok(N_HEADS * HEAD_BLOCK), tok(KV_LORA), tok(D_ROPE),
                 tok(N_HEADS * HEAD_BLOCK),
                 pl.BlockSpec((N_HEADS, 1, D_V, tm), lambda i: (0, i, 0, 0)))
    return pl.pallas_call(
        _proj_body,
        grid=(nt,),
        in_specs=[tok(D_MODEL), tab, tab] + [full(a) for a in consts],
        out_specs=out_specs,
        out_shape=out_shape,
        compiler_params=_cparams(("parallel",)),
        name="proj",
    )(x, cos, sin, *consts)


def _prompt_attn_body(q_ref, k_ref, vt_ref, o_ref, s_ref):
    tq = q_ref.shape[0]
    tk = KV_CHUNK
    per_chunk = tk // vt_ref.shape[3]
    heads = vt_ref.shape[0]
    qi = pl.program_id(2)

    def produce(c, hh):
        lanes = slice(hh * HEAD_BLOCK, (hh + 1) * HEAD_BLOCK)
        start = pl.multiple_of(c * tk, tk)
        s = _dot_nt(k_ref[pl.ds(start, tk), lanes], q_ref[:, lanes])
        s_ref[hh * 2 + c % 2] = s
        return jnp.max(s, axis=0, keepdims=True)

    def consume(c, hh, cmax, state, diagonal):
        m, l, acc = state
        s = s_ref[hh * 2 + c % 2]
        if diagonal:
            kpos = c * tk + lax.broadcasted_iota(jnp.int32, s.shape, 0)
            qpos = qi * tq + lax.broadcasted_iota(jnp.int32, s.shape, 1)
            s = jnp.where(kpos <= qpos, s, -jnp.inf)
            cmax = jnp.max(s, axis=0, keepdims=True)
        m_new = jnp.maximum(m, cmax)
        p = jnp.exp(s - m_new)
        alpha = jnp.exp(m - m_new)
        l = alpha * l + jnp.sum(p, axis=0, keepdims=True)
        vt = jnp.concatenate([vt_ref[hh, c * per_chunk + r] for r in range(per_chunk)], axis=1)
        acc = alpha * acc + _dot(vt, p.astype(BF16))
        return m_new, l, acc

    def body(c, carry):
        nxt = tuple(produce(c + 1, hh) for hh in range(heads))
        states = tuple(consume(c, hh, carry[0][hh], carry[1][hh], False) for hh in range(heads))
        return nxt, states

    states = tuple((jnp.full((1, tq), NEG_BIG, F32), jnp.zeros((1, tq), F32), jnp.zeros((D_V, tq), F32))
                   for _ in range(heads))
    first = tuple(produce(0, hh) for hh in range(heads))
    cmax, states = lax.fori_loop(0, qi, body, (first, states))
    for hh in range(heads):
        m, l, acc = consume(qi, hh, cmax[hh], states[hh], True)
        o_ref[hh * D_V:(hh + 1) * D_V, :] = (acc / l).astype(o_ref.dtype)


def _prompt_attention(q, kcat, vt4, batch, seq):
    tq = Q_TILE
    tile = vt4.shape[3]
    nq = seq // tq
    hs = HEADS_PER_STEP
    return pl.pallas_call(
        _prompt_attn_body,
        grid=(batch, N_HEADS // hs, nq),
        in_specs=[
            pl.BlockSpec((tq, hs * HEAD_BLOCK), lambda b, h, i: (b * nq + i, h)),
            pl.BlockSpec((seq, hs * HEAD_BLOCK), lambda b, h, i: (b, h)),
            pl.BlockSpec((hs, seq // tile, D_V, tile), lambda b, h, i: (h, b, 0, 0)),
        ],
        out_specs=pl.BlockSpec((hs * D_V, tq), lambda b, h, i: (h, b * nq + i)),
        out_shape=jax.ShapeDtypeStruct((ATT_WIDTH, batch * seq), BF16),
        scratch_shapes=[pltpu.VMEM((2 * hs, KV_CHUNK, tq), F32)],
        compiler_params=_cparams(("parallel", "parallel", "arbitrary")),
        name="prompt_attn",
    )(q, kcat, vt4)


def _sample_attn_body(pt_ref, q_ref, newlat_ref, newkrt_ref, hmask_ref, wabs_ref, rsel_ref, wukt_ref, wuv_ref,
                      *rest):
    g = PAGES_PER_STEP
    lat_refs = rest[:g]
    krt_refs = rest[g:2 * g]
    o_ref = rest[2 * g]
    qabs_sc, qr_sc, m_sc, l_sc, acc_sc = rest[2 * g + 1:]
    j = pl.program_id(1)
    t_new = q_ref.shape[1]
    rows = t_new * N_HEADS

    @pl.when(j == 0)
    def _():
        q = q_ref[0].astype(F32)
        qb = jnp.broadcast_to(q[:, None, :], (t_new, N_HEADS, q.shape[1])).reshape(rows, q.shape[1])
        qbd = (qb * hmask_ref[...]).astype(BF16)
        qabs_sc[...] = _dot(qbd, wabs_ref[...]).astype(BF16)
        qr_sc[...] = _dot(qbd, rsel_ref[...]).astype(BF16)
        m_sc[...] = jnp.full(m_sc.shape, NEG_BIG, F32)
        l_sc[...] = jnp.zeros(l_sc.shape, F32)
        acc_sc[...] = jnp.zeros(acc_sc.shape, F32)

    def attend(lat32, krt32, causal):
        lat = lat32.astype(BF16)
        tk = lat.shape[0]
        both = _dot_nt(jnp.concatenate([wukt_ref[...], qabs_sc[...]], axis=0), lat)
        knt = both[:N_HEADS * D_NOPE]
        ssq = jnp.sum((knt * knt).reshape(N_HEADS, D_NOPE, tk), axis=1)
        rinv = lax.rsqrt(ssq * (1.0 / D_NOPE) + EPS)
        rinv = jnp.broadcast_to(rinv[None], (t_new, N_HEADS, tk)).reshape(rows, tk)
        s = both[N_HEADS * D_NOPE:] * rinv + _dot(qr_sc[...], krt32.astype(BF16))
        if causal:
            q_tok = lax.broadcasted_iota(jnp.int32, (t_new, N_HEADS, tk), 0).reshape(rows, tk)
            k_tok = lax.broadcasted_iota(jnp.int32, (rows, tk), 1)
            s = jnp.where(k_tok <= q_tok, s, -jnp.inf)
        m = m_sc[...]
        m_new = jnp.maximum(m, jnp.max(s, axis=1, keepdims=True))
        p = jnp.exp(s - m_new)
        alpha = jnp.exp(m - m_new)
        l_sc[...] = alpha * l_sc[...] + jnp.sum(p, axis=1, keepdims=True)
        acc_sc[...] = alpha * acc_sc[...] + _dot(p.astype(BF16), lat)
        m_sc[...] = m_new

    attend(jnp.concatenate([r[0, 0] for r in lat_refs], axis=0),
           jnp.concatenate([r[0, 0] for r in krt_refs], axis=1), causal=False)

    @pl.when(j == pl.num_programs(1) - 1)
    def _():
        attend(newlat_ref[0], newkrt_ref[0], causal=True)
        o_lat = acc_sc[...] / l_sc[...]
        full = _dot(o_lat.astype(BF16), wuv_ref[...])
        shape3 = (t_new, N_HEADS, ATT_WIDTH)
        col = lax.broadcasted_iota(jnp.int32, shape3, 2)
        lo = lax.broadcasted_iota(jnp.int32, shape3, 1) * D_V
        sel = jnp.where(col >= lo, jnp.where(col < lo + D_V, full.reshape(shape3), 0.0), 0.0)
        o_ref[0] = jnp.sum(sel, axis=1)


def _sample_attention(q, new_lat, new_krt, cache_lat, cache_krt, page_table, w):
    bd, t_new, _ = q.shape
    n_pages = page_table.shape[1]
    g = PAGES_PER_STEP
    steps = n_pages // g
    rows = t_new * N_HEADS
    per_b = lambda a: pl.BlockSpec((1,) + a.shape[1:], lambda b, j, pt: (b, 0, 0))
    full = lambda a: pl.BlockSpec(a.shape, lambda b, j, pt: (0,) * a.ndim)

    def page_spec(shape, k):
        return pl.BlockSpec((1, 1) + shape, lambda b, j, pt: (0, pt[b, j * g + k], 0, 0))

    consts = (w['hmask'], w['wabs'], w['rsel'], w['wukt'], w['wuv'])
    grid_spec = pltpu.PrefetchScalarGridSpec(
        num_scalar_prefetch=1,
        grid=(bd, steps),
        in_specs=[per_b(q), per_b(new_lat), per_b(new_krt)] + [full(a) for a in consts]
        + [page_spec((PAGE_SIZE, KV_LORA), k) for k in range(g)]
        + [page_spec((D_ROPE, PAGE_SIZE), k) for k in range(g)],
        out_specs=pl.BlockSpec((1, t_new, ATT_WIDTH), lambda b, j, pt: (b, 0, 0)),
        scratch_shapes=[
            pltpu.VMEM((rows, KV_LORA), BF16),
            pltpu.VMEM((rows, D_ROPE), BF16),
            pltpu.VMEM((rows, 1), F32),
            pltpu.VMEM((rows, 1), F32),
            pltpu.VMEM((rows, KV_LORA), F32),
        ],
    )
    return pl.pallas_call(
        _sample_attn_body,
        grid_spec=grid_spec,
        out_shape=jax.ShapeDtypeStruct((bd, t_new, ATT_WIDTH), F32),
        compiler_params=_cparams(("parallel", "arbitrary")),
        name="sample_attn",
    )(page_table, q, new_lat, new_krt, *consts, *([cache_lat] * g), *([cache_krt] * g))


def _s5_body(a_ref, h0_ref, u_ref, bbig_ref, cbig_ref, d_ref, y_ref, hl_ref, st_ref, h_ref,
             *, nseq, steps, pitch, group_pitch):
    groups, rows, _ = u_ref.shape
    slabs = STATE_WIDTH // LANES
    half = slabs // 2

    @pl.when(pl.program_id(0) == 0)
    def _():
        h_ref[...] = h0_ref[...]

    for g in range(groups):
        inc = _dot(u_ref[g].astype(BF16), bbig_ref[...])
        for k in range(slabs):
            st_ref[k, g * group_pitch:g * group_pitch + rows, :] = inc[:, k * LANES:(k + 1) * LANES]

    a = a_ref[...]
    ar = [a[:, k * LANES:(k + 1) * LANES] for k in range(half)]
    ai = [a[:, (half + k) * LANES:(half + k + 1) * LANES] for k in range(half)]

    def body(t, carry):
        out = []
        for k in range(half):
            hr, hi = carry[k], carry[half + k]
            sel = pl.ds(t, nseq, stride=pitch)
            nr = ar[k] * hr - ai[k] * hi + st_ref[k, sel, :]
            ni = ar[k] * hi + ai[k] * hr + st_ref[half + k, sel, :]
            st_ref[k, sel, :] = nr
            st_ref[half + k, sel, :] = ni
            out.append((nr, ni))
        return tuple(o[0] for o in out) + tuple(o[1] for o in out)

    h = h_ref[...]
    final = lax.fori_loop(0, steps, body, tuple(h[:, k * LANES:(k + 1) * LANES] for k in range(slabs)))
    h_new = jnp.concatenate(final, axis=1)
    h_ref[...] = h_new
    hl_ref[...] = h_new

    d = d_ref[...]
    for g in range(groups):
        lo = g * group_pitch
        states = jnp.concatenate([st_ref[k, lo:lo + rows, :].astype(BF16) for k in range(slabs)], axis=1)
        y_ref[g] = _dot(states, cbig_ref[...]) + d * u_ref[g]


def _s5(u3, h0, ssm, nseq, steps, pitch, group_pitch):
    a_bar, bbig, cbig, d_row = ssm
    groups, rows_total, _ = u3.shape
    rows = steps if groups == nseq else rows_total
    n_tiles = rows_total // rows
    scratch_rows = (groups - 1) * group_pitch + rows
    full = lambda arr: pl.BlockSpec(arr.shape, lambda i: (0,) * arr.ndim)
    blk = pl.BlockSpec((groups, rows, SSM_WIDTH), lambda i: (0, i, 0))
    return pl.pallas_call(
        functools.partial(_s5_body, nseq=nseq, steps=steps, pitch=pitch, group_pitch=group_pitch),
        grid=(n_tiles,),
        in_specs=[full(a_bar), full(h0), blk, full(bbig), full(cbig), full(d_row)],
        out_specs=(blk, full(h0)),
        out_shape=(jax.ShapeDtypeStruct(u3.shape, F32), jax.ShapeDtypeStruct(h0.shape, F32)),
        scratch_shapes=[pltpu.VMEM((STATE_WIDTH // LANES, scratch_rows, LANES), F32),
                        pltpu.VMEM(h0.shape, F32)],
        compiler_params=_cparams(("arbitrary",)),
        name="s5",
    )(a_bar, h0, u3, bbig, cbig, d_row)


def _s5_weights(a_re, a_im, log_dt, b, c, d_skip):
    g = SSM_GROUPS
    a_re, a_im = a_re.astype(F32), a_im.astype(F32)
    dt = jnp.exp(log_dt.astype(F32))[:, None]
    lr, li = a_re * dt, a_im * dt
    ar, ai = jnp.exp(lr) * jnp.cos(li), jnp.exp(lr) * jnp.sin(li)
    den = a_re * a_re + a_im * a_im
    fr = ((ar - 1.0) * a_re + ai * a_im) / den
    fi = (ai * a_re - (ar - 1.0) * a_im) / den
    bre, bim = b[..., 0].astype(F32), b[..., 1].astype(F32)
    bbr = fr[..., None] * bre - fi[..., None] * bim
    bbi = fr[..., None] * bim + fi[..., None] * bre
    cre, cim = c[..., 0].astype(F32), c[..., 1].astype(F32)
    eye = jnp.eye(g, dtype=F32)
    bbig = jnp.einsum('rgph,gq->ghrqp', jnp.stack([bbr, bbi]), eye).reshape(SSM_WIDTH, STATE_WIDTH)
    cbig = jnp.einsum('rgop,gq->rqpgo', jnp.stack([cre, -cim]), eye).reshape(STATE_WIDTH, SSM_WIDTH)
    a_bar = jnp.concatenate([ar.reshape(1, -1), ai.reshape(1, -1)], axis=1)
    return a_bar, bbig.astype(BF16), cbig.astype(BF16), d_skip.astype(F32).reshape(1, -1)


def _mix_body(x_ref, y_ref, o_ref, wglu_ref, wout_ref, gffn_ref, x1_ref, xn_ref, *, att_transposed):
    gl = _dot(jax.nn.gelu(y_ref[...]).astype(BF16), wglu_ref[...])
    glu = gl[:, :SSM_WIDTH] * jax.nn.sigmoid(gl[:, SSM_WIDTH:])
    o = o_ref[...]
    if att_transposed:
        o = o.astype(F32).T
    x1 = x_ref[...] + _dot(glu.astype(BF16), wout_ref[:SSM_WIDTH, :]) + _dot(o.astype(BF16), wout_ref[SSM_WIDTH:, :])
    x1_ref[...] = x1
    xn_ref[...] = _rms(x1, gffn_ref[...]).astype(BF16)


def _mix(x, y, o, w, att_transposed):
    n = x.shape[0]
    tm = TOK_TILE
    tok = lambda width: pl.BlockSpec((tm, width), lambda i: (i, 0))
    full = lambda a: pl.BlockSpec(a.shape, lambda i: (0,) * a.ndim)
    o_spec = pl.BlockSpec((ATT_WIDTH, tm), lambda i: (0, i)) if att_transposed else tok(ATT_WIDTH)
    consts = (w['wglu'], w['wout'], w['gffn'])
    return pl.pallas_call(
        functools.partial(_mix_body, att_transposed=att_transposed),
        grid=(n // tm,),
        in_specs=[tok(D_MODEL), tok(SSM_WIDTH), o_spec] + [full(a) for a in consts],
        out_specs=(tok(D_MODEL), tok(D_MODEL)),
        out_shape=(jax.ShapeDtypeStruct((n, D_MODEL), F32), jax.ShapeDtypeStruct((n, D_MODEL), BF16)),
        compiler_params=_cparams(("parallel",)),
        name="mix",
    )(x, y, o, *consts)


def _desc_top(vals_fn, count):
    tops = []
    prev = None
    for _ in range(count):
        cur = vals_fn(prev)
        tops.append(cur)
        prev = cur
    return tops


def _sort_network(n):
    pairs = []
    p = 1
    while p < n:
        k = p
        while k >= 1:
            for j in range(k % p, n - k, 2 * k):
                for i in range(min(k, n - j - k)):
                    if (i + j) // (2 * p) == (i + j + k) // (2 * p):
                        pairs.append((i + j, i + j + k))
            k //= 2
        p *= 2
    return pairs


def _top_keys(s):
    depth = PEER_KEYS // SUBLANES
    cols = [s[k * SUBLANES:(k + 1) * SUBLANES] for k in range(depth)]
    for i, j in _sort_network(depth):
        cols[i], cols[j] = jnp.maximum(cols[i], cols[j]), jnp.minimum(cols[i], cols[j])
    tops = []
    for t in range(PEER_TOPK):
        top = jnp.max(cols[0], axis=0, keepdims=True)
        tops.append(top)
        popped = cols[0] == top
        for k in range(PEER_TOPK - 1 - t):
            cols[k] = jnp.where(popped, cols[k + 1], cols[k])
    return tops


def _peer_route_body(xn_ref, wq_ref, keys_ref, rank_ref, e2_ref, lim_ref, cc_ref):
    xn = xn_ref[...]
    q = _dot(xn, wq_ref[...]).astype(BF16)
    k1 = keys_ref[0]
    k2 = keys_ref[1]
    tops1, tops2, s1_all, s2_all = [], [], [], []
    for h in range(PEER_HEADS):
        base = h * 2 * PEER_HALF
        s1 = _dot_nt(k1, q[:, base:base + PEER_HALF])
        s2 = _dot_nt(k2, q[:, base + PEER_HALF:base + 2 * PEER_HALF])
        s1_all.append(s1)
        s2_all.append(s2)
        tops1.append(_top_keys(s1))
        tops2.append(_top_keys(s2))
    a = [jnp.concatenate([tops1[h][r] for h in range(PEER_HEADS)], axis=0) for r in range(PEER_TOPK)]
    b = [jnp.concatenate([tops2[h][r] for h in range(PEER_HEADS)], axis=0) for r in range(PEER_TOPK)]
    cands = [a[i] + b[j] for i in range(PEER_TOPK) for j in range(PEER_TOPK) if (i + 1) * (j + 1) <= PEER_TOPK]

    def nxt_sum(prev):
        best = None
        for cnd in cands:
            v = cnd if prev is None else jnp.where(cnd < prev, cnd, -jnp.inf)
            best = v if best is None else jnp.maximum(best, v)
        return best

    tsum = _desc_top(nxt_sum, PEER_TOPK)
    tau = tsum[-1]
    zsum = jnp.zeros_like(tau)
    for t in tsum:
        zsum = zsum + jnp.exp(t - tsum[0])
    inv_z = 1.0 / zsum
    count = []
    for i in range(PEER_TOPK):
        cnt = jnp.zeros_like(tau)
        for j in range(PEER_TOPK):
            if (i + 1) * (j + 1) <= PEER_TOPK:
                cnt = cnt + jnp.where(a[i] + b[j] >= tau, 1.0, 0.0)
        count.append(cnt)
    for h in range(PEER_HEADS):
        row = slice(h, h + 1)
        lim = jnp.zeros(s1_all[h].shape, F32)
        rank = jnp.full(s2_all[h].shape, float(PEER_KEYS), F32)
        for r in range(PEER_TOPK):
            lim = jnp.where(s1_all[h] == a[r][row], count[r][row], lim)
            rank = jnp.where(s2_all[h] == b[r][row], float(r), rank)
        lim_ref[h] = lim
        rank_ref[h] = rank
        cc_ref[h] = jnp.exp(s1_all[h] - a[0][row]) * inv_z[row]
        e2_ref[h] = jnp.exp(s2_all[h] - b[0][row])


def _peer_route(xn, w):
    n = xn.shape[0]
    tm = TOK_TILE
    spec = pl.BlockSpec((PEER_HEADS, PEER_KEYS, tm), lambda i: (0, 0, i))
    shp = jax.ShapeDtypeStruct((PEER_HEADS, PEER_KEYS, n), F32)
    return pl.pallas_call(
        _peer_route_body,
        grid=(n // tm,),
        in_specs=[pl.BlockSpec((tm, D_MODEL), lambda i: (i, 0)),
                  pl.BlockSpec(w['peer_wq'].shape, lambda i: (0, 0)),
                  pl.BlockSpec(w['peer_keys'].shape, lambda i: (0, 0, 0))],
        out_specs=(spec, spec, spec, spec),
        out_shape=(shp, shp, shp, shp),
        compiler_params=_cparams(("parallel",)),
        name="peer_route",
    )(xn, w['peer_wq'], w['peer_keys'])


def _peer_expert_body(xn_ref, x1_ref, u_ref, vt_ref, rank_ref, e2_ref, lim_ref, cc_ref, o_ref, acc_ref, act_ref,
                      ga_ref):
    j = pl.program_id(1)
    te = u_ref.shape[0]
    tm = xn_ref.shape[0]
    nsub = te // PEER_KEYS
    tiles = GATE_ROWS // BF16_ROWS

    @pl.when(j == 0)
    def _():
        acc_ref[...] = jnp.zeros(acc_ref.shape, F32)

    act_ref[...] = _gelu_tanh(_dot_nt(u_ref[...], xn_ref[...]))
    zero = jnp.zeros((), BF16)
    first = pl.multiple_of(j * nsub, SUBLANES)
    for c in range(tm // LANES):
        ls = slice(c * LANES, (c + 1) * LANES)
        for grp in range(nsub // GATE_GROUP):
            for lo in range(0, PEER_KEYS, GATE_ROWS):
                ks = slice(lo, lo + GATE_ROWS)
                gates = [None] * GATE_GROUP
                for h in range(PEER_HEADS):
                    rank = rank_ref[h, ks, ls].astype(BF16).reshape(tiles, BF16_ROWS, LANES)
                    e2 = e2_ref[h, ks, ls].astype(BF16).reshape(tiles, BF16_ROWS, LANES)
                    lim8 = lim_ref[h, pl.ds(first, nsub), ls]
                    cc8 = cc_ref[h, pl.ds(first, nsub), ls]
                    for s in range(GATE_GROUP):
                        sub = grp * GATE_GROUP + s
                        lim = jnp.broadcast_to(lim8[sub:sub + 1], (BF16_ROWS, LANES)).astype(BF16)
                        cc = jnp.broadcast_to(cc8[sub:sub + 1], (BF16_ROWS, LANES)).astype(BF16)
                        wgt = jnp.where(rank < lim[None], e2 * cc[None], zero)
                        gates[s] = wgt if gates[s] is None else gates[s] + wgt
                for s in range(GATE_GROUP):
                    rs = slice((grp * GATE_GROUP + s) * PEER_KEYS + lo, (grp * GATE_GROUP + s) * PEER_KEYS + lo + GATE_ROWS)
                    ga_ref[rs, ls] = gates[s].reshape(GATE_ROWS, LANES) * act_ref[rs, ls].astype(BF16)
    acc_ref[...] += _dot(vt_ref[0], ga_ref[...])

    @pl.when(j == pl.num_programs(1) - 1)
    def _():
        o_ref[...] = x1_ref[...] + acc_ref[...].T


def _peer_experts(xn, x1, routes, w):
    n = xn.shape[0]
    tm = PEER_TOK_TILE
    te = PEER_EXP_TILE
    n_exp = w['peer_u'].shape[0]
    tok = pl.BlockSpec((tm, D_MODEL), lambda i, j: (i, 0))
    route = lambda a: pl.BlockSpec((PEER_HEADS, a.shape[1], tm), lambda i, j: (0, 0, i))
    return pl.pallas_call(
        _peer_expert_body,
        grid=(n // tm, n_exp // te),
        in_specs=[tok, tok,
                  pl.BlockSpec((te, D_MODEL), lambda i, j: (j, 0)),
                  pl.BlockSpec((1, D_MODEL, te), lambda i, j: (j, 0, 0)),
                  ] + [route(a) for a in routes],
        out_specs=tok,
        out_shape=jax.ShapeDtypeStruct((n, D_MODEL), F32),
        scratch_shapes=[pltpu.VMEM((D_MODEL, tm), F32), pltpu.VMEM((te, tm), F32), pltpu.VMEM((te, tm), BF16)],
        compiler_params=_cparams(("parallel", "arbitrary")),
        name="peer_experts",
    )(xn, x1, w['peer_u'], w['peer_vt'], *routes)


def _rope_tables(pos):
    inv = ROPE_THETA ** (-jnp.arange(0, D_ROPE, 2, dtype=F32) / D_ROPE)
    ang = pos.astype(F32)[:, None] * inv[None, :]
    cos, sin = jnp.cos(ang), jnp.sin(ang)
    t = pos.shape[0]
    ones = jnp.ones((t, ROPE_LO), F32)
    tail = HEAD_BLOCK - ROPE_LO - D_ROPE
    cos_t = jnp.concatenate([ones, cos, cos, jnp.ones((t, tail), F32)], axis=1)
    sin_t = jnp.concatenate([0 * ones, -sin, sin, jnp.zeros((t, tail), F32)], axis=1)
    return cos_t, sin_t


def _head_blocks(wmat, lo, width):
    pad = jnp.zeros(wmat.shape[:2] + (HEAD_BLOCK,), wmat.dtype)
    return pad.at[:, :, lo:lo + width].set(wmat).reshape(wmat.shape[0], N_HEADS * HEAD_BLOCK)


def _prep_weights(norm_mix, w_in, norm_q_lora, w_uq, norm_kv_lora, w_uk, w_uv, g_qn, g_qr, g_kn, g_kr,
                  w_glu, w_out, norm_ffn, peer_wq, peer_keys, peer_u, peer_v):
    row = lambda v: v.astype(F32).reshape(1, -1)
    kr_cols = jnp.zeros((D_MODEL, LANES), F32).at[:, ROPE_LO:ROPE_LO + D_ROPE].set(w_in[:, IN_WIDTH - D_ROPE:])
    win = jnp.concatenate([w_in[:, :IN_WIDTH - D_ROPE], kr_cols], axis=1)
    wuq = _head_blocks(w_uq.reshape(Q_LORA, N_HEADS, D_NOPE + D_ROPE), 0, D_NOPE + D_ROPE)
    wuk = _head_blocks(w_uk, 0, D_NOPE)
    wuvt = w_uv.reshape(KV_LORA, ATT_WIDTH).T
    lane = jnp.arange(HEAD_BLOCK)
    is_nope = lane < ROPE_LO
    is_rope = (lane >= ROPE_LO) & (lane < ROPE_LO + D_ROPE)
    seg = ((is_nope[:, None] & is_nope[None, :]) | (is_rope[:, None] & is_rope[None, :])).astype(BF16)
    icnt = jnp.where(is_nope, 1.0 / D_NOPE, jnp.where(is_rope, 1.0 / D_ROPE, 1.0)).astype(F32).reshape(1, -1)
    blockvec = lambda a, bvec: jnp.zeros((HEAD_BLOCK,), F32).at[:ROPE_LO].set(a).at[ROPE_LO:ROPE_LO + D_ROPE].set(bvec)
    qgain = (blockvec(g_qn, g_qr) * ATTN_SCALE).reshape(1, -1)
    kgain = blockvec(g_kn, jnp.zeros((D_ROPE,), F32)).reshape(1, -1)
    krgain = blockvec(jnp.zeros((D_NOPE,), F32), g_kr).reshape(1, -1)
    wabs = jnp.einsum('chd,d->hdc', w_uk, g_kn)
    wabs = jnp.zeros((N_HEADS, HEAD_BLOCK, KV_LORA), F32).at[:, :D_NOPE, :].set(wabs)
    wabs = wabs.reshape(N_HEADS * HEAD_BLOCK, KV_LORA)
    rsel = jnp.zeros((N_HEADS, HEAD_BLOCK, D_ROPE), F32).at[:, ROPE_LO:ROPE_LO + D_ROPE, :].set(
        jnp.broadcast_to(jnp.eye(D_ROPE, dtype=F32), (N_HEADS, D_ROPE, D_ROPE)))
    rsel = rsel.reshape(N_HEADS * HEAD_BLOCK, D_ROPE)
    head_of_lane = jnp.arange(N_HEADS * HEAD_BLOCK) // HEAD_BLOCK
    hmask8 = (head_of_lane[None, :] == jnp.arange(N_HEADS)[:, None]).astype(F32)
    return {
        'gmix': row(norm_mix), 'gq': row(norm_q_lora), 'gkv': row(norm_kv_lora),
        'seg': seg, 'icnt': icnt, 'qgain': qgain, 'kgain': kgain, 'krgain': krgain,
        'win': win.astype(BF16), 'wuq': wuq.astype(BF16), 'wuk': wuk.astype(BF16), 'wuvt': wuvt.astype(BF16),
        'hmask8': hmask8, 'wabs': wabs.astype(BF16), 'rsel': rsel.astype(BF16),
        'wukt': w_uk.reshape(KV_LORA, N_HEADS * D_NOPE).T.astype(BF16),
        'wuv': w_uv.reshape(KV_LORA, ATT_WIDTH).astype(BF16),
        'wglu': w_glu.astype(BF16), 'wout': w_out.astype(BF16), 'gffn': row(norm_ffn),
        'peer_wq': peer_wq.astype(BF16), 'peer_keys': peer_keys.astype(BF16),
        'peer_u': peer_u.astype(BF16),
        'peer_vt': peer_v.astype(BF16).reshape(-1, PEER_EXP_TILE, D_MODEL).transpose(0, 2, 1),
    }


def _peer(xn, x1, w):
    return _peer_experts(xn, x1, _peer_route(xn, w), w)


def _state_in(st):
    return jnp.moveaxis(st.astype(F32), -1, 1).reshape(st.shape[0], STATE_WIDTH)


def _state_out(h):
    return jnp.moveaxis(h.reshape(h.shape[0], 2, SSM_GROUPS, SSM_STATE), 1, -1)


def kernel(x_prompt, x_sample, cache_kv_latent, cache_k_rope, state_ssm, page_table, norm_mix, w_in, norm_q_lora, w_uq, norm_kv_lora, w_uk, w_uv, qk_gain_q_nope, qk_gain_q_rope, qk_gain_k_nope, qk_gain_k_rope, ssm_a_re, ssm_a_im, ssm_log_dt, ssm_b, ssm_c, ssm_d, w_glu, w_out, norm_ffn, peer_wq, peer_keys, peer_u, peer_v):
    depth = norm_mix.shape[0]
    batch, seq, _ = x_prompt.shape
    dec_batch, dec_seq, _ = x_sample.shape
    n_dec = dec_batch * dec_seq
    past = page_table.shape[1] * PAGE_SIZE
    assert seq % Q_TILE == 0 and Q_TILE == KV_CHUNK and seq % S5_TIME_TILE == 0 and n_dec % PEER_TOK_TILE == 0
    assert page_table.shape[1] % PAGES_PER_STEP == 0 and TOK_TILE % dec_seq == 0 and dec_seq <= PAGE_SIZE

    cos_p, sin_p = _rope_tables(jnp.arange(seq))
    cos_s, sin_s = _rope_tables(past + jnp.arange(dec_seq))
    cos_s, sin_s = jnp.tile(cos_s, (TOK_TILE // dec_seq, 1)), jnp.tile(sin_s, (TOK_TILE // dec_seq, 1))
    cache_krt = jnp.swapaxes(cache_k_rope, 2, 3)

    xp = x_prompt.reshape(batch * seq, D_MODEL)
    xs = x_sample.reshape(n_dec, D_MODEL)
    outs = {k: [] for k in ('lat_p', 'kr_p', 'ssm_p', 'lat_s', 'kr_s', 'ssm_s')}
    for l in range(depth):
        w = _prep_weights(norm_mix[l], w_in[l], norm_q_lora[l], w_uq[l], norm_kv_lora[l], w_uk[l], w_uv[l],
                          qk_gain_q_nope[l], qk_gain_q_rope[l], qk_gain_k_nope[l], qk_gain_k_rope[l],
                          w_glu[l], w_out[l], norm_ffn[l], peer_wq[l], peer_keys[l], peer_u[l], peer_v[l])
        w['hmask'] = jnp.tile(w['hmask8'], (dec_seq, 1))
        ssm = _s5_weights(ssm_a_re[l], ssm_a_im[l], ssm_log_dt[l], ssm_b[l], ssm_c[l], ssm_d[l])

        u, q, ckv, kr, kcat, vt4 = _proj(xp, cos_p, sin_p, w)
        o_t = _prompt_attention(q, kcat, vt4, batch, seq)
        pitch = S5_TIME_TILE + S5_ROW_SKEW
        y, h_last = _s5(u.reshape(batch, seq, SSM_WIDTH), jnp.zeros((batch, STATE_WIDTH), F32), ssm,
                        nseq=batch, steps=S5_TIME_TILE, pitch=pitch, group_pitch=pitch)
        x1, xn = _mix(xp, y.reshape(batch * seq, SSM_WIDTH), o_t, w, att_transposed=True)
        xp = _peer(xn, x1, w)
        outs['lat_p'].append(ckv.reshape(batch, seq, KV_LORA))
        outs['kr_p'].append(kr.reshape(batch, seq, D_ROPE))
        outs['ssm_p'].append(_state_out(h_last))

        u, q, ckv, kr, _, _ = _proj(xs, cos_s, sin_s, w)
        pad_rows = ((0, 0), (0, PAGE_SIZE - dec_seq), (0, 0))
        new_lat = jnp.pad(ckv.reshape(dec_batch, dec_seq, KV_LORA), pad_rows)
        new_krt = jnp.swapaxes(jnp.pad(kr.reshape(dec_batch, dec_seq, D_ROPE), pad_rows), 1, 2)
        o = _sample_attention(q.reshape(dec_batch, dec_seq, -1), new_lat, new_krt,
                              cache_kv_latent[l:l + 1], cache_krt[l:l + 1], page_table, w)
        y, h_last = _s5(u.reshape(1, n_dec, SSM_WIDTH), _state_in(state_ssm[l]), ssm,
                        nseq=dec_batch, steps=dec_seq, pitch=dec_seq, group_pitch=n_dec)
        x1, xn = _mix(xs, y.reshape(n_dec, SSM_WIDTH), o.reshape(n_dec, ATT_WIDTH), w, att_transposed=False)
        xs = _peer(xn, x1, w)
        outs['lat_s'].append(ckv.reshape(dec_batch, dec_seq, KV_LORA))
        outs['kr_s'].append(kr.reshape(dec_batch, dec_seq, D_ROPE))
        outs['ssm_s'].append(_state_out(h_last))

    return (xp.reshape(batch, seq, D_MODEL), xs.reshape(dec_batch, dec_seq, D_MODEL),
            jnp.stack(outs['lat_p']), jnp.stack(outs['kr_p']), jnp.stack(outs['ssm_p']),
            jnp.stack(outs['lat_s']), jnp.stack(outs['kr_s']), jnp.stack(outs['ssm_s']))
```
